```python
import math
import jax, jax.numpy as jnp
from jax import lax
import numpy as np

D_MODEL = 2048
BATCH = 2
SEQ = 8192
DEPTH = 2

N_EVEN = (DEPTH + 1) // 2
N_ODD = DEPTH // 2
DEEPNORM_ALPHA = (2.0 * DEPTH) ** 0.25
DEEPNORM_BETA = (8.0 * DEPTH) ** -0.25
LN_EPS = 1e-5

DIFF_HEADS = 8
DIFF_HEAD_DIM = 64
DIFF_WIDTH = DIFF_HEADS * 2 * DIFF_HEAD_DIM
DIFF_Q_BLOCK = 128

SSM_INNER = D_MODEL // 2
SSM_HEAD_DIM = 64
SSM_HEADS = SSM_INNER // SSM_HEAD_DIM
SSM_GROUPS = 2
SSM_STATE = 128
SSM_CONV = 4
SSM_CHUNK = 128
SSM_CONV_CH = SSM_INNER + 2 * SSM_GROUPS * SSM_STATE
EVEN_IN = 3 * DIFF_WIDTH + SSM_INNER + SSM_CONV_CH + SSM_HEADS
EVEN_MIX = DIFF_WIDTH + SSM_INNER

NSA_HEADS = 16
NSA_KV_GROUPS = 4
NSA_REP = NSA_HEADS // NSA_KV_GROUPS
NSA_HEAD_DIM = 128
NSA_CMP_LEN = 32
NSA_CMP_STRIDE = 16
NSA_CMP_HIDDEN = 128
NSA_SEL_LEN = 64
NSA_SEL_COUNT = 16
NSA_WINDOW = 512
NSA_Q_BLOCK = 64
NSA_Q_WIDTH = NSA_HEADS * NSA_HEAD_DIM
NSA_KV_WIDTH = NSA_KV_GROUPS * NSA_HEAD_DIM
ODD_IN = NSA_Q_WIDTH + 6 * NSA_KV_WIDTH + 3 * NSA_HEADS
ODD_MIX = NSA_Q_WIDTH

MOE_GROUPS = 8
MOE_PER_GROUP = 4
MOE_EXPERTS = MOE_GROUPS * MOE_PER_GROUP
MOE_TOPK = 2
MOE_FF = 1024
MOE_ROW_BLOCK = 128

NEG = -1e30
BIG = 1e30

kernel_name = "hybrid_diffattn_ssd_nsa_grouped_moe"


def layer_norm(x, g, b):
    xf = x.astype(jnp.float32)
    mu = jnp.mean(xf, -1, keepdims=True)
    var = jnp.mean(jnp.square(xf - mu), -1, keepdims=True)
    return ((xf - mu) * lax.rsqrt(var + LN_EPS) * g + b).astype(x.dtype)


def rms_norm(x, g, eps=1e-5):
    xf = x.astype(jnp.float32)
    y = xf * lax.rsqrt(jnp.mean(xf * xf, -1, keepdims=True) + eps)
    return (y * g).astype(x.dtype)


def masked_softmax(s, mask):
    s = jnp.where(mask, s, -jnp.inf)
    m = jnp.max(s, -1, keepdims=True)
    m = jnp.where(jnp.isfinite(m), m, 0.0)
    e = jnp.exp(s - m)
    den = jnp.sum(e, -1, keepdims=True)
    return e / jnp.where(den > 0, den, 1.0)


def diff_attention(q, k, v, lam, lam_init, subln_g):
    b, s = q.shape[:2]
    nb = s // DIFF_Q_BLOCK
    scale = DIFF_HEAD_DIM ** -0.5
    kpos = jnp.arange(s)

    def one_block(i):
        q0 = i * DIFF_Q_BLOCK
        qb = lax.dynamic_slice_in_dim(q, q0, DIFF_Q_BLOCK, axis=1)
        sc = jnp.einsum('bqhcd,bkhcd->bhcqk', qb, k, preferred_element_type=jnp.float32) * scale
        qpos = q0 + jnp.arange(DIFF_Q_BLOCK)
        causal = kpos[None, :] <= qpos[:, None]
        p = jax.nn.softmax(jnp.where(causal, sc, -jnp.inf), axis=-1)
        a = p[:, :, 0] - lam * p[:, :, 1]
        return jnp.einsum('bhqk,bkhd->bqhd', a.astype(v.dtype), v)

    o = lax.map(one_block, jnp.arange(nb))
    o = jnp.moveaxis(o, 0, 1).reshape(b, s, DIFF_HEADS, 2 * DIFF_HEAD_DIM)
    o = rms_norm(o, subln_g) * (1.0 - lam_init)
    return o.reshape(b, s, DIFF_WIDTH)


def causal_depthwise_conv(x, w, bias):
    c = x.shape[-1]
    y = lax.conv_general_dilated(x, w[:, None, :].astype(x.dtype), window_strides=(1,),
                                 padding=[(SSM_CONV - 1, 0)],
                                 dimension_numbers=('NWC', 'WIO', 'NWC'),
                                 feature_group_count=c)
    return y + bias


def segsum_exp(a):
    l = a.shape[-1]
    cs = jnp.cumsum(a, -1)
    diff = cs[..., :, None] - cs[..., None, :]
    mask = jnp.tril(jnp.ones((l, l), bool))
    return jnp.exp(jnp.where(mask, diff, -jnp.inf))


def ssd_chunked(x, dt, a, bm, cm):
    b, s = x.shape[:2]
    nc = s // SSM_CHUNK
    L = SSM_CHUNK
    J = SSM_HEADS // SSM_GROUPS
    xd = (x.astype(jnp.float32) * dt[..., None]).reshape(b, nc, L, SSM_GROUPS, J, SSM_HEAD_DIM)
    adt = jnp.transpose((dt * a).reshape(b, nc, L, SSM_GROUPS, J), (0, 3, 4, 1, 2))
    bc = bm.astype(jnp.float32).reshape(b, nc, L, SSM_GROUPS, SSM_STATE)
    cc = cm.astype(jnp.float32).reshape(b, nc, L, SSM_GROUPS, SSM_STATE)
    a_cum = jnp.cumsum(adt, -1)
    cb = jnp.einsum('bclgn,bcsgn->bgcls', cc, bc)
    w = cb[:, :, None] * segsum_exp(adt)
    y_diag = jnp.einsum('bgjcls,bcsgjp->bclgjp', w, xd)
    decay_states = jnp.exp(a_cum[..., -1:] - a_cum)
    states = jnp.einsum('bclgn,bgjcl,bclgjp->cbgjpn', bc, decay_states, xd)
    chunk_decay = jnp.moveaxis(jnp.exp(a_cum[..., -1]), -1, 0)

    def step(h, inp):
        dec, st = inp
        return h * dec[..., None, None] + st, h

    h0 = jnp.zeros(states.shape[1:], jnp.float32)
    _, prev = lax.scan(step, h0, (chunk_decay, states))
    y_off = jnp.einsum('bclgn,cbgjpn,bgjcl->bclgjp', cc, prev, jnp.exp(a_cum))
    return (y_diag + y_off).reshape(b, s, SSM_HEADS, SSM_HEAD_DIM)


def mamba2_mixer(z, xbc, dt_raw, conv_w, conv_b, dt_bias, a_log, d_skip, norm_g):
    b, s, _ = z.shape
    xbc = jax.nn.silu(causal_depthwise_conv(xbc, conv_w, conv_b))
    xs, bm, cm = jnp.split(xbc, [SSM_INNER, SSM_INNER + SSM_GROUPS * SSM_STATE], axis=-1)
    xs = xs.reshape(b, s, SSM_HEADS, SSM_HEAD_DIM)
    bm = bm.reshape(b, s, SSM_GROUPS, SSM_STATE)
    cm = cm.reshape(b, s, SSM_GROUPS, SSM_STATE)
    dt = jax.nn.softplus(dt_raw.astype(jnp.float32) + dt_bias.astype(jnp.float32))
    a = -jnp.exp(a_log.astype(jnp.float32))
    y = ssd_chunked(xs, dt, a, bm, cm)
    y = y + d_skip.astype(jnp.float32)[:, None] * xs.astype(jnp.float32)
    y = y.reshape(b, s, SSM_INNER) * jax.nn.silu(z.astype(jnp.float32))
    yg = y.reshape(b, s, SSM_GROUPS, SSM_INNER // SSM_GROUPS)
    yg = yg * lax.rsqrt(jnp.mean(yg * yg, -1, keepdims=True) + 1e-5)
    return (yg.reshape(b, s, SSM_INNER) * norm_g).astype(z.dtype)


def even_mixer(x, w_in, w_out, lam_q1, lam_k1, lam_q2, lam_k2, subln_g,
               conv_w, conv_b, dt_bias, a_log, d_skip, ssm_norm_g, lam_init):
    b, s, _ = x.shape
    proj = x @ w_in
    splits = np.cumsum([DIFF_WIDTH] * 4 + [SSM_CONV_CH]).tolist()
    q, k, v, z, xbc, dt_raw = jnp.split(proj, splits, axis=-1)
    q = q.reshape(b, s, DIFF_HEADS, 2, DIFF_HEAD_DIM)
    k = k.reshape(b, s, DIFF_HEADS, 2, DIFF_HEAD_DIM)
    v = v.reshape(b, s, DIFF_HEADS, 2 * DIFF_HEAD_DIM)
    lam = (jnp.exp(jnp.sum(lam_q1.astype(jnp.float32) * lam_k1))
           - jnp.exp(jnp.sum(lam_q2.astype(jnp.float32) * lam_k2)) + lam_init)
    y_attn = diff_attention(q, k, v, lam, lam_init, subln_g)
    y_ssm = mamba2_mixer(z, xbc, dt_raw, conv_w, conv_b, dt_bias, a_log, d_skip, ssm_norm_g)
    return jnp.concatenate([y_attn, y_ssm], axis=-1) @ w_out


def compress_tokens(kv, pe, w1, w2):
    b, s, g, d = kv.shape
    nc = (s - NSA_CMP_LEN) // NSA_CMP_STRIDE + 1
    idx = jnp.arange(nc)[:, None] * NSA_CMP_STRIDE + jnp.arange(NSA_CMP_LEN)[None, :]
    blocks = kv[:, idx] + pe[:, None, :]
    blocks = jnp.moveaxis(blocks, 3, 2).reshape(b, nc, g, NSA_CMP_LEN * d)
    return jax.nn.gelu(blocks @ w1) @ w2


def cmp_to_sel_overlap(nc, ns):
    cs = jnp.arange(nc) * NSA_CMP_STRIDE
    ss = jnp.arange(ns) * NSA_SEL_LEN
    lo = jnp.maximum(cs[:, None], ss[None, :])
    hi = jnp.minimum(cs[:, None] + NSA_CMP_LEN, ss[None, :] + NSA_SEL_LEN)
    return jnp.maximum(hi - lo, 0).astype(jnp.float32) / NSA_CMP_LEN


def nsa_attention(q, k_cmp, v_cmp, k_sel, v_sel, k_win, v_win, gates):
    b, s = q.shape[:2]
    nc = k_cmp.shape[1]
    ns = s // NSA_SEL_LEN
    n_pick = min(NSA_SEL_COUNT, ns)
    overlap = cmp_to_sel_overlap(nc, ns)
    cmp_end = jnp.arange(nc) * NSA_CMP_STRIDE + NSA_CMP_LEN - 1
    sel_ids = jnp.arange(ns)
    sel_start = sel_ids * NSA_SEL_LEN
    scale = NSA_HEAD_DIM ** -0.5
    ks_g = jnp.moveaxis(k_sel, 2, 1)
    vs_g = jnp.moveaxis(v_sel, 2, 1)
    pad = jnp.zeros((b, NSA_WINDOW) + k_win.shape[2:], k_win.dtype)
    kw_p = jnp.concatenate([pad, k_win], axis=1)
    vw_p = jnp.concatenate([pad, v_win], axis=1)
    gather = jax.vmap(jax.vmap(lambda tab, ii: tab[ii]))
    offs = jnp.arange(NSA_SEL_LEN)
    nwin = NSA_WINDOW + NSA_Q_BLOCK

    def one_block(i):
        q0 = i * NSA_Q_BLOCK
        qb = lax.dynamic_slice_in_dim(q, q0, NSA_Q_BLOCK, axis=1)
        gb = lax.dynamic_slice_in_dim(gates, q0, NSA_Q_BLOCK, axis=1)
        t = q0 + jnp.arange(NSA_Q_BLOCK)
        sc = jnp.einsum('bqgrd,bngd->bgrqn', qb, k_cmp, preferred_element_type=jnp.float32) * scale
        p_cmp = masked_softmax(sc, cmp_end[None, :] <= t[:, None])
        o_cmp = jnp.einsum('bgrqn,bngd->bqgrd', p_cmp.astype(v_cmp.dtype), v_cmp)
        imp = jnp.einsum('bgrqn,nm->bgqm', p_cmp, overlap)
        cur = t // NSA_SEL_LEN
        valid = sel_start[None, :] <= t[:, None]
        forced = ((sel_ids[None, :] == 0) | (sel_ids[None, :] == cur[:, None])
                  | (sel_ids[None, :] == cur[:, None] - 1))
        imp = jnp.where(valid, jnp.where(forced, BIG, imp), NEG)
        _, top = lax.top_k(imp, n_pick)
        tok = (top[..., None] * NSA_SEL_LEN + offs).reshape(b, NSA_KV_GROUPS, NSA_Q_BLOCK, n_pick * NSA_SEL_LEN)
        kg = gather(ks_g, tok)
        vg = gather(vs_g, tok)
        ss = jnp.einsum('bqgrd,bgqkd->bgrqk', qb, kg, preferred_element_type=jnp.float32) * scale
        p_sel = masked_softmax(ss, (tok <= t[:, None])[:, :, None])
        o_sel = jnp.einsum('bgrqk,bgqkd->bqgrd', p_sel.astype(vg.dtype), vg)
        kw = lax.dynamic_slice_in_dim(kw_p, q0, nwin, axis=1)
        vw = lax.dynamic_slice_in_dim(vw_p, q0, nwin, axis=1)
        kpos = q0 - NSA_WINDOW + jnp.arange(nwin)
        wmask = ((kpos[None, :] <= t[:, None]) & (kpos[None, :] > t[:, None] - NSA_WINDOW)
                 & (kpos[None, :] >= 0))
        sw = jnp.einsum('bqgrd,bkgd->bgrqk', qb, kw, preferred_element_type=jnp.float32) * scale
        p_win = masked_softmax(sw, wmask)
        o_win = jnp.einsum('bgrqk,bkgd->bqgrd', p_win.astype(vw.dtype), vw)
        return gb[..., 0:1] * o_cmp + gb[..., 1:2] * o_sel + gb[..., 2:3] * o_win

    o = lax.map(one_block, jnp.arange(s // NSA_Q_BLOCK))
    return jnp.moveaxis(o, 0, 1).reshape(b, s, NSA_HEADS * NSA_HEAD_DIM)


def odd_mixer(x, w_in, w_out, pe_k, pe_v, ck_w1, ck_w2, cv_w1, cv_w2):
    b, s, _ = x.shape
    proj = x @ w_in
    splits = np.cumsum([NSA_Q_WIDTH] + [NSA_KV_WIDTH] * 6).tolist()
    q, kc, vc, ks, vs, kw, vw, g = jnp.split(proj, splits, axis=-1)
    q = q.reshape(b, s, NSA_KV_GROUPS, NSA_REP, NSA_HEAD_DIM)
    kvr = lambda a: a.reshape(b, s, NSA_KV_GROUPS, NSA_HEAD_DIM)
    k_cmp = compress_tokens(kvr(kc), pe_k, ck_w1, ck_w2)
    v_cmp = compress_tokens(kvr(vc), pe_v, cv_w1, cv_w2)
    gates = jax.nn.sigmoid(g.astype(jnp.float32)).reshape(b, s, NSA_KV_GROUPS, NSA_REP, 3).astype(x.dtype)
    o = nsa_attention(q, k_cmp, v_cmp, kvr(ks), kvr(vs), kvr(kw), kvr(vw), gates)
    return o @ w_out


def route(h, router_w, router_b):
    s = jax.nn.sigmoid(jnp.dot(h, router_w, preferred_element_type=jnp.float32))
    biased = (s + router_b.astype(jnp.float32)).reshape(-1, MOE_GROUPS, MOE_PER_GROUP)
    top_val, top_idx = lax.top_k(biased, MOE_TOPK)
    grp = jnp.argmax(jnp.sum(top_val, -1), axis=-1)
    local = jnp.take_along_axis(top_idx, grp[:, None, None], axis=1)[:, 0]
    expert = grp[:, None] * MOE_PER_GROUP + local
    w = jnp.take_along_axis(s, expert, axis=1)
    return expert.astype(jnp.int32), w / jnp.sum(w, -1, keepdims=True)


def moe_ffn(h, router_w, router_b, w_gate, w_up, w_down):
    t, d = h.shape
    expert, gate = route(h, router_w, router_b)
    tk = t * MOE_TOPK
    m = MOE_ROW_BLOCK
    n_blocks = -(-tk // m) + MOE_EXPERTS
    rows = n_blocks * m
    flat_e = expert.reshape(-1)
    flat_tok = jnp.arange(tk, dtype=jnp.int32) // MOE_TOPK
    order = jnp.argsort(flat_e)
    sorted_e = flat_e[order]
    counts = jnp.zeros((MOE_EXPERTS,), jnp.int32).at[flat_e].add(1)
    padded = (counts + m - 1) // m * m
    pad_end = jnp.cumsum(padded)
    start = jnp.cumsum(counts) - counts
    dest = (pad_end - padded)[sorted_e] + jnp.arange(tk, dtype=jnp.int32) - start[sorted_e]
    buf_tok = jnp.full((rows,), t, jnp.int32).at[dest].set(flat_tok[order])
    buf_w = jnp.zeros((rows,), h.dtype).at[dest].set(gate.reshape(-1)[order].astype(h.dtype))
    block_e = jnp.minimum(jnp.searchsorted(pad_end, jnp.arange(n_blocks, dtype=jnp.int32) * m, side='right'),
                          MOE_EXPERTS - 1)
    xg = jnp.concatenate([h, jnp.zeros((1, d), h.dtype)], axis=0)[buf_tok].reshape(n_blocks, m, d)

    def expert_block(args):
        xb, e = args
        hid = jax.nn.silu(xb @ w_gate[e]) * (xb @ w_up[e])
        return hid @ w_down[e]

    yb = lax.map(expert_block, (xg, block_e)).reshape(rows, d)
    return jax.ops.segment_sum(yb * buf_w[:, None], buf_tok, num_segments=t + 1)[:t]


def setup_inputs(seed: int = 0) -> dict:
    key = jax.random.key(seed)
    keys = iter(jax.random.split(key, 64))
    nrm = lambda shape, sc: jax.random.normal(next(keys), shape, jnp.float32) * sc
    dt = jnp.exp(jax.random.uniform(next(keys), (N_EVEN, SSM_HEADS), jnp.float32)
                 * (math.log(0.1) - math.log(1e-3)) + math.log(1e-3))
    dt = jnp.maximum(dt, 1e-4)
    return {
        "x": nrm((BATCH, SEQ, D_MODEL), 1.0),
        "router_w": nrm((D_MODEL, MOE_EXPERTS), D_MODEL ** -0.5),
        "router_b": nrm((MOE_EXPERTS,), 0.01),
        "even_w_in": nrm((N_EVEN, D_MODEL, EVEN_IN), D_MODEL ** -0.5),
        "even_w_out": nrm((N_EVEN, EVEN_MIX, D_MODEL), DEEPNORM_BETA * EVEN_MIX ** -0.5),
        "diff_lam_q1": nrm((N_EVEN, DIFF_HEAD_DIM), 0.1),
        "diff_lam_k1": nrm((N_EVEN, DIFF_HEAD_DIM), 0.1),
        "diff_lam_q2": nrm((N_EVEN, DIFF_HEAD_DIM), 0.1),
        "diff_lam_k2": nrm((N_EVEN, DIFF_HEAD_DIM), 0.1),
        "diff_subln_g": 1.0 + nrm((N_EVEN, 2 * DIFF_HEAD_DIM), 0.02),
        "ssm_conv_w": nrm((N_EVEN, SSM_CONV, SSM_CONV_CH), SSM_CONV ** -0.5),
        "ssm_conv_b": nrm((N_EVEN, SSM_CONV_CH), 0.02),
        "ssm_dt_bias": dt + jnp.log(-jnp.expm1(-dt)),
        "ssm_a_log": jnp.log(jax.random.uniform(next(keys), (N_EVEN, SSM_HEADS), jnp.float32, 1.0, 16.0)),
        "ssm_d": 1.0 + nrm((N_EVEN, SSM_HEADS), 0.02),
        "ssm_norm_g": 1.0 + nrm((N_EVEN, SSM_INNER), 0.02),
        "odd_w_in": nrm((N_ODD, D_MODEL, ODD_IN), D_MODEL ** -0.5),
        "odd_w_out": nrm((N_ODD, ODD_MIX, D_MODEL), DEEPNORM_BETA * ODD_MIX ** -0.5),
        "nsa_pe_k": nrm((N_ODD, NSA_CMP_LEN, NSA_HEAD_DIM), 0.1),
        "nsa_pe_v": nrm((N_ODD, NSA_CMP_LEN, NSA_HEAD_DIM), 0.1),
        "nsa_ck_w1": nrm((N_ODD, NSA_CMP_LEN * NSA_HEAD_DIM, NSA_CMP_HIDDEN), (NSA_CMP_LEN * NSA_HEAD_DIM) ** -0.5),
        "nsa_ck_w2": nrm((N_ODD, NSA_CMP_HIDDEN, NSA_HEAD_DIM), NSA_CMP_HIDDEN ** -0.5),
        "nsa_cv_w1": nrm((N_ODD, NSA_CMP_LEN * NSA_HEAD_DIM, NSA_CMP_HIDDEN), (NSA_CMP_LEN * NSA_HEAD_DIM) ** -0.5),
        "nsa_cv_w2": nrm((N_ODD, NSA_CMP_HIDDEN, NSA_HEAD_DIM), NSA_CMP_HIDDEN ** -0.5),
        "ln_mix_g": 1.0 + nrm((DEPTH, D_MODEL), 0.02),
        "ln_mix_b": nrm((DEPTH, D_MODEL), 0.02),
        "ln_ffn_g": 1.0 + nrm((DEPTH, D_MODEL), 0.02),
        "ln_ffn_b": nrm((DEPTH, D_MODEL), 0.02),
        "moe_w_gate": nrm((DEPTH, MOE_EXPERTS, D_MODEL, MOE_FF), D_MODEL ** -0.5),
        "moe_w_up": nrm((DEPTH, MOE_EXPERTS, D_MODEL, MOE_FF), D_MODEL ** -0.5),
        "moe_w_down": nrm((DEPTH, MOE_EXPERTS, MOE_FF, D_MODEL), DEEPNORM_BETA * MOE_FF ** -0.5),
    }


def reference(x, router_w, router_b, even_w_in, even_w_out, diff_lam_q1, diff_lam_k1, diff_lam_q2,
              diff_lam_k2, diff_subln_g, ssm_conv_w, ssm_conv_b, ssm_dt_bias, ssm_a_log, ssm_d,
              ssm_norm_g, odd_w_in, odd_w_out, nsa_pe_k, nsa_pe_v, nsa_ck_w1, nsa_ck_w2, nsa_cv_w1,
              nsa_cv_w2, ln_mix_g, ln_mix_b, ln_ffn_g, ln_ffn_b, moe_w_gate, moe_w_up, moe_w_down):
    b, s, d = x.shape
    for layer in range(DEPTH):
        i = layer // 2
        if layer % 2 == 0:
            lam_init = 0.8 - 0.6 * math.exp(-0.3 * layer)
            mix = even_mixer(x, even_w_in[i], even_w_out[i], diff_lam_q1[i], diff_lam_k1[i],
                             diff_lam_q2[i], diff_lam_k2[i], diff_subln_g[i], ssm_conv_w[i],
                             ssm_conv_b[i], ssm_dt_bias[i], ssm_a_log[i], ssm_d[i], ssm_norm_g[i],
                             lam_init)
        else:
            mix = odd_mixer(x, odd_w_in[i], odd_w_out[i], nsa_pe_k[i], nsa_pe_v[i], nsa_ck_w1[i],
                            nsa_ck_w2[i], nsa_cv_w1[i], nsa_cv_w2[i])
        h = layer_norm(DEEPNORM_ALPHA * x + mix, ln_mix_g[layer], ln_mix_b[layer])
        f = moe_ffn(h.reshape(b * s, d), router_w, router_b, moe_w_gate[layer], moe_w_up[layer],
                    moe_w_down[layer]).reshape(b, s, d)
        x = layer_norm(DEEPNORM_ALPHA * h + f, ln_ffn_g[layer], ln_ffn_b[layer])
    return x
```

```python
import functools
import math

import jax
import jax.numpy as jnp
import numpy as np
from jax import lax
from jax.experimental import pallas as pl
from jax.experimental.pallas import tpu as pltpu

F32 = jnp.float32
BF16 = jnp.bfloat16

D_MODEL = 2048
DEPTH = 2
DEEPNORM_ALPHA = (2.0 * DEPTH) ** 0.25
LN_EPS = 1e-5

DIFF_HEADS = 8
DIFF_HEAD_DIM = 64
DIFF_WIDTH = DIFF_HEADS * 2 * DIFF_HEAD_DIM

SSM_INNER = D_MODEL // 2
SSM_HEAD_DIM = 64
SSM_HEADS = SSM_INNER // SSM_HEAD_DIM
SSM_GROUPS = 2
SSM_STATE = 128
SSM_CONV = 4
SSM_CHUNK = 128
SSM_CONV_CH = SSM_INNER + 2 * SSM_GROUPS * SSM_STATE

NSA_HEADS = 16
NSA_KV_GROUPS = 4
NSA_REP = NSA_HEADS // NSA_KV_GROUPS
NSA_HEAD_DIM = 128
NSA_CMP_LEN = 32
NSA_CMP_STRIDE = 16
NSA_SEL_LEN = 64
NSA_SEL_COUNT = 16
NSA_WINDOW = 512
NSA_Q_WIDTH = NSA_HEADS * NSA_HEAD_DIM
NSA_KV_WIDTH = NSA_KV_GROUPS * NSA_HEAD_DIM

MOE_GROUPS = 8
MOE_PER_GROUP = 4
MOE_EXPERTS = MOE_GROUPS * MOE_PER_GROUP
MOE_TOPK = 2
MOE_FF = 1024

NEG = -1e30
BIG = 1e30
LANES = 128
VMEM_LIMIT = 56 * 1024 * 1024


def _params(sem):
    return pltpu.CompilerParams(dimension_semantics=sem, vmem_limit_bytes=VMEM_LIMIT)


def _dot(a, b):
    return jnp.dot(a, b, preferred_element_type=F32)


def _dot_nt(a, b):
    return lax.dot_general(a, b, (((1,), (1,)), ((), ())), preferred_element_type=F32)


def _mm_kernel(x_ref, w_ref, o_ref):
    o_ref[...] = _dot(x_ref[...], w_ref[...]).astype(o_ref.dtype)


def _matmul(x, w, out_dtype, tm, tn):
    m, k = x.shape
    n = w.shape[1]
    return pl.pallas_call(
        _mm_kernel,
        grid=(n // tn, m // tm),
        in_specs=[pl.BlockSpec((tm, k), lambda j, i: (i, 0)),
                  pl.BlockSpec((k, tn), lambda j, i: (0, j))],
        out_specs=pl.BlockSpec((tm, tn), lambda j, i: (i, j)),
        out_shape=jax.ShapeDtypeStruct((m, n), out_dtype),
        compiler_params=_params(("parallel", "parallel")),
        name="proj_matmul",
    )(x, w)


def _layer_norm_rows(y, g, b):
    mu = jnp.mean(y, -1, keepdims=True)
    yc = y - mu
    var = jnp.mean(yc * yc, -1, keepdims=True)
    return yc * lax.rsqrt(var + LN_EPS) * g + b


def _mm_ln_kernel(a_ref, w_ref, res_ref, g_ref, b_ref, o_ref, ob_ref):
    y = _dot(a_ref[...], w_ref[...]) + DEEPNORM_ALPHA * res_ref[...]
    h = _layer_norm_rows(y, g_ref[...], b_ref[...])
    o_ref[...] = h
    ob_ref[...] = h.astype(BF16)


def _matmul_res_ln(a, w, res, g, b, tm):
    m, k = a.shape
    n = w.shape[1]
    return pl.pallas_call(
        _mm_ln_kernel,
        grid=(m // tm,),
        in_specs=[pl.BlockSpec((tm, k), lambda i: (i, 0)),
                  pl.BlockSpec((k, n), lambda i: (0, 0)),
                  pl.BlockSpec((tm, n), lambda i: (i, 0)),
                  pl.BlockSpec((1, n), lambda i: (0, 0)),
                  pl.BlockSpec((1, n), lambda i: (0, 0))],
        out_specs=[pl.BlockSpec((tm, n), lambda i: (i, 0)),
                   pl.BlockSpec((tm, n), lambda i: (i, 0))],
        out_shape=[jax.ShapeDtypeStruct((m, n), F32), jax.ShapeDtypeStruct((m, n), BF16)],
        compiler_params=_params(("parallel",)),
        name="outproj_ln",
    )(a, w, res, g.reshape(1, n), b.reshape(1, n))


def _add_ln_kernel(h_ref, f_ref, g_ref, b_ref, o_ref, ob_ref):
    y = DEEPNORM_ALPHA * h_ref[...] + f_ref[...]
    x = _layer_norm_rows(y, g_ref[...], b_ref[...])
    o_ref[...] = x
    ob_ref[...] = x.astype(BF16)


def _add_ln(h, f, g, b, tm):
    m, n = h.shape
    row = pl.BlockSpec((tm, n), lambda i: (i, 0))
    vec = pl.BlockSpec((1, n), lambda i: (0, 0))
    return pl.pallas_call(
        _add_ln_kernel,
        grid=(m // tm,),
        in_specs=[row, row, vec, vec],
        out_specs=[row, row],
        out_shape=[jax.ShapeDtypeStruct((m, n), F32), jax.ShapeDtypeStruct((m, n), BF16)],
        compiler_params=_params(("parallel",)),
        name="ffn_add_ln",
    )(h, f, g.reshape(1, n), b.reshape(1, n))


def _diff_kernel(lq1_ref, lk1_ref, lq2_ref, lk2_ref, g_ref, q_ref, k_ref, v_ref, o_ref, *, tq, lam_init):
    i = pl.program_id(2)
    rows = 2 * tq
    q = q_ref[...]
    lane = lax.broadcasted_iota(jnp.int32, (tq, LANES), 1)
    zero = jnp.zeros_like(q)
    q2 = jnp.concatenate([jnp.where(lane < DIFF_HEAD_DIM, q, zero),
                          jnp.where(lane >= DIFF_HEAD_DIM, q, zero)], axis=0)
    q2 = q2 * jnp.asarray(DIFF_HEAD_DIM ** -0.5, BF16)

    def step(k0, carry, masked):
        m, l, acc = carry
        kt = k_ref[pl.ds(k0, tq), :]
        vt = v_ref[pl.ds(k0, tq), :]
        s = _dot_nt(q2, kt)
        if masked:
            r_id = lax.broadcasted_iota(jnp.int32, (rows, tq), 0) & (tq - 1)
            c_id = lax.broadcasted_iota(jnp.int32, (rows, tq), 1)
            s = jnp.where(c_id <= r_id, s, NEG)
        m_new = jnp.maximum(m, jnp.max(s, -1, keepdims=True))
        alpha = jnp.exp(m - m_new)
        p = jnp.exp(s - m_new)
        l = alpha * l + jnp.sum(p, -1, keepdims=True)
        acc = alpha * acc + _dot(p.astype(BF16), vt)
        return m_new, l, acc

    init = (jnp.full((rows, 1), NEG, F32), jnp.zeros((rows, 1), F32), jnp.zeros((rows, LANES), F32))
    carry = lax.fori_loop(0, i, lambda j, c: step(pl.multiple_of(j * tq, tq), c, False), init)
    _, l, acc = step(pl.multiple_of(i * tq, tq), carry, True)
    o = acc / l
    lam = (jnp.exp(jnp.sum(lq1_ref[...] * lk1_ref[...], -1, keepdims=True))
           - jnp.exp(jnp.sum(lq2_ref[...] * lk2_ref[...], -1, keepdims=True)) + lam_init)
    od = o[:tq] - lam * o[tq:]
    y = od * lax.rsqrt(jnp.mean(od * od, -1, keepdims=True) + 1e-5) * g_ref[...]
    o_ref[...] = (y * (1.0 - lam_init)).astype(o_ref.dtype)


def _diff_attention(qkv, lq1, lk1, lq2, lk2, subln_g, lam_init, tq):
    b, s, _ = qkv.shape
    vec64 = pl.BlockSpec((1, DIFF_HEAD_DIM), lambda bb, h, i: (0, 0))
    kv_spec = lambda off: pl.BlockSpec((None, s, LANES), lambda bb, h, i: (bb, 0, off + h))
    return pl.pallas_call(
        functools.partial(_diff_kernel, tq=tq, lam_init=lam_init),
        grid=(b, DIFF_HEADS, s // tq),
        in_specs=[vec64, vec64, vec64, vec64,
                  pl.BlockSpec((1, LANES), lambda bb, h, i: (0, 0)),
                  pl.BlockSpec((None, tq, LANES), lambda bb, h, i: (bb, i, h)),
                  kv_spec(DIFF_HEADS), kv_spec(2 * DIFF_HEADS)],
        out_specs=pl.BlockSpec((None, tq, LANES), lambda bb, h, i: (bb, i, h)),
        out_shape=jax.ShapeDtypeStruct((b, s, DIFF_WIDTH), BF16),
        compiler_params=_params(("parallel", "parallel", "arbitrary")),
        name="diff_attention",
    )(lq1.reshape(1, -1), lk1.reshape(1, -1), lq2.reshape(1, -1), lk2.reshape(1, -1),
      subln_g.reshape(1, -1), qkv, qkv, qkv)


def _segsum_exp(a):
    l = a.shape[-1]
    cs = jnp.cumsum(a, -1)
    diff = cs[..., :, None] - cs[..., None, :]
    mask = jnp.tril(jnp.ones((l, l), bool))
    return jnp.exp(jnp.where(mask, diff, -jnp.inf))


def _ssd_chunked(x, dt, a, bm, cm):
    b, s = x.shape[:2]
    nc = s // SSM_CHUNK
    L = SSM_CHUNK
    J = SSM_HEADS // SSM_GROUPS
    xd = (x.astype(F32) * dt[..., None]).reshape(b, nc, L, SSM_GROUPS, J, SSM_HEAD_DIM)
    adt = jnp.transpose((dt * a).reshape(b, nc, L, SSM_GROUPS, J), (0, 3, 4, 1, 2))
    bc = bm.astype(F32).reshape(b, nc, L, SSM_GROUPS, SSM_STATE)
    cc = cm.astype(F32).reshape(b, nc, L, SSM_GROUPS, SSM_STATE)
    a_cum = jnp.cumsum(adt, -1)
    cb = jnp.einsum('bclgn,bcsgn->bgcls', cc, bc)
    w = cb[:, :, None] * _segsum_exp(adt)
    y_diag = jnp.einsum('bgjcls,bcsgjp->bclgjp', w, xd)
    decay_states = jnp.exp(a_cum[..., -1:] - a_cum)
    states = jnp.einsum('bclgn,bgjcl,bclgjp->cbgjpn', bc, decay_states, xd)
    chunk_decay = jnp.moveaxis(jnp.exp(a_cum[..., -1]), -1, 0)

    def step(h, inp):
        dec, st = inp
        return h * dec[..., None, None] + st, h

    h0 = jnp.zeros(states.shape[1:], F32)
    _, prev = lax.scan(step, h0, (chunk_decay, states))
    y_off = jnp.einsum('bclgn,cbgjpn,bgjcl->bclgjp', cc, prev, jnp.exp(a_cum))
    return (y_diag + y_off).reshape(b, s, SSM_HEADS, SSM_HEAD_DIM)


def _mamba2_mixer(z, xbc, dt_raw, conv_w, conv_b, dt_bias, a_log, d_skip, norm_g):
    b, s, _ = z.shape
    c = xbc.shape[-1]
    y = lax.conv_general_dilated(xbc, conv_w[:, None, :], window_strides=(1,),
                                 padding=[(SSM_CONV - 1, 0)],
                                 dimension_numbers=('NWC', 'WIO', 'NWC'),
                                 feature_group_count=c)
    xbc = jax.nn.silu(y + conv_b)
    xs, bm, cm = jnp.split(xbc, [SSM_INNER, SSM_INNER + SSM_GROUPS * SSM_STATE], axis=-1)
    xs = xs.reshape(b, s, SSM_HEADS, SSM_HEAD_DIM)
    bm = bm.reshape(b, s, SSM_GROUPS, SSM_STATE)
    cm = cm.reshape(b, s, SSM_GROUPS, SSM_STATE)
    dt = jax.nn.softplus(dt_raw + dt_bias)
    a = -jnp.exp(a_log)
    y = _ssd_chunked(xs, dt, a, bm, cm)
    y = y + d_skip[:, None] * xs
    y = y.reshape(b, s, SSM_INNER) * jax.nn.silu(z)
    yg = y.reshape(b, s, SSM_GROUPS, SSM_INNER // SSM_GROUPS)
    yg = yg * lax.rsqrt(jnp.mean(yg * yg, -1, keepdims=True) + 1e-5)
    return yg.reshape(b, s, SSM_INNER) * norm_g


def _cmp_kernel(x_ref, pek_ref, pev_ref, w1k_ref, w1v_ref, w2k_ref, w2v_ref, ko_ref, vo_ref,
                top_ref, bot_ref, *, nb):
    l = pl.program_id(1)
    g4 = NSA_KV_GROUPS

    @pl.when(l == 0)
    def _():
        top_ref[...] = jnp.zeros_like(top_ref)
        bot_ref[...] = jnp.zeros_like(bot_ref)

    x = x_ref[...]
    for kv, (pe_ref, w1_ref) in enumerate(((pek_ref, w1k_ref), (pev_ref, w1v_ref))):
        x4 = jnp.concatenate([x[:, kv * NSA_KV_WIDTH + g * LANES: kv * NSA_KV_WIDTH + (g + 1) * LANES]
                              for g in range(g4)], axis=0)
        top_ref[kv] += _dot((x4 + pe_ref[pl.ds(l, 1), :]).astype(BF16), w1_ref[l])
        bot_ref[kv] += _dot((x4 + pe_ref[pl.ds(l + NSA_CMP_STRIDE, 1), :]).astype(BF16),
                            w1_ref[l + NSA_CMP_STRIDE])

    @pl.when(l == NSA_CMP_STRIDE - 1)
    def _():
        last = lax.broadcasted_iota(jnp.int32, (nb, LANES), 0) == nb - 1
        for kv, (w2_ref, o_ref) in enumerate(((w2k_ref, ko_ref), (w2v_ref, vo_ref))):
            for g in range(g4):
                top = top_ref[kv, g * nb:(g + 1) * nb, :]
                bot = bot_ref[kv, g * nb:(g + 1) * nb, :]
                hid = top + pltpu.roll(bot, nb - 1, 0)
                out = _dot(jax.nn.gelu(hid, approximate=True).astype(BF16), w2_ref[...])
                o_ref[g] = jnp.where(last, 0.0, out).astype(o_ref.dtype)


def _compress(bf, pe_k, pe_v, w1k, w1v, w2k, w2v):
    b, s, w = bf.shape
    nb = s // NSA_CMP_STRIDE
    xv = bf.reshape(b, nb, NSA_CMP_STRIDE * w)
    full = lambda shape: pl.BlockSpec(shape, lambda bb, l: (0,) * len(shape))
    out_spec = pl.BlockSpec((None, NSA_KV_GROUPS, nb, LANES), lambda bb, l: (bb, 0, 0, 0))
    out_sd = jax.ShapeDtypeStruct((b, NSA_KV_GROUPS, nb, LANES), BF16)
    w1 = lambda a: a.astype(BF16).reshape(NSA_CMP_LEN, NSA_HEAD_DIM, -1)
    return pl.pallas_call(
        functools.partial(_cmp_kernel, nb=nb),
        grid=(b, NSA_CMP_STRIDE),
        in_specs=[pl.BlockSpec((None, nb, w), lambda bb, l: (bb, 0, l)),
                  full((NSA_CMP_LEN, LANES)), full((NSA_CMP_LEN, LANES)),
                  full((NSA_CMP_LEN, NSA_HEAD_DIM, LANES)), full((NSA_CMP_LEN, NSA_HEAD_DIM, LANES)),
                  full((LANES, LANES)), full((LANES, LANES))],
        out_specs=[out_spec, out_spec],
        out_shape=[out_sd, out_sd],
        scratch_shapes=[pltpu.VMEM((2, NSA_KV_GROUPS * nb, LANES), F32),
                        pltpu.VMEM((2, NSA_KV_GROUPS * nb, LANES), F32)],
        compiler_params=_params(("parallel", "arbitrary")),
        name="nsa_compress",
    )(xv, pe_k, pe_v, w1(w1k), w1(w1v), w2k.astype(BF16), w2v.astype(BF16))


def _nsa_kernel(q_ref, gate_ref, kc_ref, vc_ref, ks_ref, vs_ref, kw_ref, vw_ref, blk_ref, ov_ref, o_ref,
                *, tq, tk, seq, slab):
    i = pl.program_id(2)
    t0 = i * tq
    rep = NSA_REP
    rows = rep * tq
    ncp = seq // NSA_CMP_STRIDE
    scale = NSA_HEAD_DIM ** -0.5
    q = q_ref[...]
    qs = jnp.concatenate([q[:, r * LANES:(r + 1) * LANES] for r in range(rep)], axis=0)
    row_t = t0 + (lax.broadcasted_iota(jnp.int32, (rows, 1), 0) & (tq - 1))

    sc = _dot_nt(qs, kc_ref[...]) * scale
    n_end = lax.broadcasted_iota(jnp.int32, (rows, ncp), 1) * NSA_CMP_STRIDE + (NSA_CMP_LEN - 1)
    cmask = n_end <= row_t
    scm = jnp.where(cmask, sc, NEG)
    e = jnp.where(cmask, jnp.exp(scm - jnp.max(scm, -1, keepdims=True)), 0.0)
    den = jnp.sum(e, -1, keepdims=True)
    p_cmp = e / jnp.where(den > 0, den, 1.0)
    o_cmp = _dot(p_cmp.astype(BF16), vc_ref[...])

    psum = p_cmp[0:tq]
    for r in range(1, rep):
        psum = psum + p_cmp[r * tq:(r + 1) * tq]
    p_hi = psum.astype(BF16)
    p_lo = (psum - p_hi.astype(F32)).astype(BF16)
    imp = _dot(p_hi, ov_ref[...]) + _dot(p_lo, ov_ref[...])
    qt = t0 + lax.broadcasted_iota(jnp.int32, (tq, 1), 0)
    cur = qt >> 6
    mi = lax.broadcasted_iota(jnp.int32, (tq, LANES), 1)
    forced = (mi == 0) | (mi == cur) | (mi == cur - 1)
    vals = jnp.where(mi <= cur, jnp.where(forced, BIG, imp), NEG)
    pen = jnp.full((tq, LANES), -1e9, F32)
    mf = mi.astype(F32)
    for _ in range(NSA_SEL_COUNT):
        mx = jnp.max(vals, -1, keepdims=True)
        first = jnp.min(jnp.where(vals == mx, mf, float(LANES)), -1, keepdims=True)
        hit = mf == first
        pen = jnp.where(hit, 0.0, pen)
        vals = jnp.where(hit, -jnp.inf, vals)
    pen = pen.astype(BF16)
    qa = jnp.concatenate([qs, jnp.concatenate([pen] * rep, axis=0)], axis=1)

    def sel_step(j, carry):
        m, l, acc = carry
        k0 = pl.multiple_of(j * tk, tk)
        ka = jnp.concatenate([ks_ref[pl.ds(k0, tk), :], blk_ref[pl.ds(k0, tk), :]], axis=1)
        s = _dot_nt(qa, ka) * scale
        kpos = k0 + lax.broadcasted_iota(jnp.int32, (rows, tk), 1)
        s = jnp.where(kpos <= row_t, s, NEG)
        m_new = jnp.maximum(m, jnp.max(s, -1, keepdims=True))
        alpha = jnp.exp(m - m_new)
        p = jnp.exp(s - m_new)
        l = alpha * l + jnp.sum(p, -1, keepdims=True)
        acc = alpha * acc + _dot(p.astype(BF16), vs_ref[pl.ds(k0, tk), :])
        return m_new, l, acc

    init = (jnp.full((rows, 1), NEG, F32), jnp.zeros((rows, 1), F32), jnp.zeros((rows, LANES), F32))
    _, l_sel, acc_sel = lax.fori_loop(0, (t0 + tq + tk - 1) // tk, sel_step, init)
    o_sel = acc_sel / l_sel

    ws = pl.multiple_of(jnp.clip(t0 - NSA_WINDOW, 0, seq - slab), NSA_SEL_LEN)
    sw = _dot_nt(qs, kw_ref[pl.ds(ws, slab), :]) * scale
    kpos = ws + lax.broadcasted_iota(jnp.int32, (rows, slab), 1)
    sw = jnp.where(kpos <= row_t, jnp.where(kpos > row_t - NSA_WINDOW, sw, NEG), NEG)
    ew = jnp.exp(sw - jnp.max(sw, -1, keepdims=True))
    o_win = _dot(ew.astype(BF16), vw_ref[pl.ds(ws, slab), :]) / jnp.sum(ew, -1, keepdims=True)

    gate = jax.nn.sigmoid(gate_ref[...])
    for r in range(rep):
        rs = slice(r * tq, (r + 1) * tq)
        out = (gate[:, 3 * r:3 * r + 1] * o_cmp[rs] + gate[:, 3 * r + 1:3 * r + 2] * o_sel[rs]
               + gate[:, 3 * r + 2:3 * r + 3] * o_win[rs])
        o_ref[:, r * LANES:(r + 1) * LANES] = out.astype(o_ref.dtype)


def _nsa_attention(a_bf, b_f32, k_cmp, v_cmp, tq, tk):
    b, s, _ = a_bf.shape
    g4 = NSA_KV_GROUPS
    ncp = s // NSA_CMP_STRIDE
    slab = -(-(NSA_WINDOW + tq) // LANES) * LANES
    assert s >= slab and s % tk == 0 and s % tq == 0 and tq % NSA_SEL_LEN == 0
    blk = (jnp.arange(s)[:, None] // NSA_SEL_LEN == jnp.arange(LANES)[None, :]).astype(BF16)
    cs = np.arange(ncp)[:, None] * NSA_CMP_STRIDE
    ss = np.arange(LANES)[None, :] * NSA_SEL_LEN
    ov = np.maximum(np.minimum(cs + NSA_CMP_LEN, ss + NSA_SEL_LEN) - np.maximum(cs, ss), 0) / NSA_CMP_LEN
    ov[ncp - 1] = 0.0
    qoff = NSA_Q_WIDTH // LANES
    kv_spec = lambda n: pl.BlockSpec((None, s, LANES), lambda bb, g, i: (bb, 0, qoff + n * g4 + g))
    cmp_spec = pl.BlockSpec((None, None, ncp, LANES), lambda bb, g, i: (bb, g, 0, 0))
    return pl.pallas_call(
        functools.partial(_nsa_kernel, tq=tq, tk=tk, seq=s, slab=slab),
        grid=(b, g4, s // tq),
        in_specs=[pl.BlockSpec((None, tq, NSA_REP * LANES), lambda bb, g, i: (bb, i, g)),
                  pl.BlockSpec((None, tq, LANES), lambda bb, g, i: (bb, i, 2 * g4 + g)),
                  cmp_spec, cmp_spec, kv_spec(0), kv_spec(1), kv_spec(2), kv_spec(3),
                  pl.BlockSpec((s, LANES), lambda bb, g, i: (0, 0)),
                  pl.BlockSpec((ncp, LANES), lambda bb, g, i: (0, 0))],
        out_specs=pl.BlockSpec((None, tq, NSA_REP * LANES), lambda bb, g, i: (bb, i, g)),
        out_shape=jax.ShapeDtypeStruct((b, s, NSA_Q_WIDTH), BF16),
        compiler_params=_params(("parallel", "parallel", "arbitrary")),
        name="nsa_attention",
    )(a_bf, b_f32, k_cmp, v_cmp, a_bf, a_bf, a_bf, a_bf, blk, jnp.asarray(ov, BF16))


def _moe_kernel(be_ref, nu_ref, x_ref, w_ref, wg_ref, wu_ref, wd_ref, o_ref):
    i = pl.program_id(0)

    @pl.when(i < nu_ref[0])
    def _():
        x = x_ref[...]
        hid = jax.nn.silu(_dot(x, wg_ref[...])) * _dot(x, wu_ref[...])
        o_ref[...] = _dot(hid.astype(BF16), wd_ref[...]) * w_ref[...]

    @pl.when(i >= nu_ref[0])
    def _():
        o_ref[...] = jnp.zeros_like(o_ref)


def _moe_blocks(block_e, n_used, xg, bw, wg, wu, wd, tm):
    rows, d = xg.shape
    ff = wg.shape[-1]
    grid_spec = pltpu.PrefetchScalarGridSpec(
        num_scalar_prefetch=2,
        grid=(rows // tm,),
        in_specs=[pl.BlockSpec((tm, d), lambda i, be, nu: (i, 0)),
                  pl.BlockSpec((tm, 1), lambda i, be, nu: (i, 0)),
                  pl.BlockSpec((None, d, ff), lambda i, be, nu: (be[i], 0, 0)),
                  pl.BlockSpec((None, d, ff), lambda i, be, nu: (be[i], 0, 0)),
                  pl.BlockSpec((None, ff, d), lambda i, be, nu: (be[i], 0, 0))],
        out_specs=pl.BlockSpec((tm, d), lambda i, be, nu: (i, 0)),
    )
    return pl.pallas_call(
        _moe_kernel,
        grid_spec=grid_spec,
        out_shape=jax.ShapeDtypeStruct((rows, d), F32),
        compiler_params=_params(("arbitrary",)),
        name="moe_experts",
    )(block_e, n_used, xg, bw, wg, wu, wd)


def _route(logits, router_b):
    s = jax.nn.sigmoid(logits)
    biased = (s + router_b).reshape(-1, MOE_GROUPS, MOE_PER_GROUP)
    top_val, top_idx = lax.top_k(biased, MOE_TOPK)
    grp = jnp.argmax(jnp.sum(top_val, -1), axis=-1)
    local = jnp.take_along_axis(top_idx, grp[:, None, None], axis=1)[:, 0]
    expert = grp[:, None] * MOE_PER_GROUP + local
    w = jnp.take_along_axis(s, expert, axis=1)
    return expert.astype(jnp.int32), w / jnp.sum(w, -1, keepdims=True)


def _moe_ffn(h, h_bf, router_w, router_b, wg, wu, wd, tm):
    t, d = h.shape
    logits = _router_logits(h, router_w)
    expert, gate = _route(logits, router_b)
    tk = t * MOE_TOPK
    n_blocks = tk // tm + MOE_EXPERTS
    rows = n_blocks * tm
    flat_e = expert.reshape(-1)
    order = jnp.argsort(flat_e)
    sorted_e = flat_e[order]
    counts = jnp.zeros((MOE_EXPERTS,), jnp.int32).at[flat_e].add(1)
    padded = (counts + tm - 1) // tm * tm
    pad_end = jnp.cumsum(padded)
    start = jnp.cumsum(counts) - counts
    dest = (pad_end - padded)[sorted_e] + jnp.arange(tk, dtype=jnp.int32) - start[sorted_e]
    buf_tok = jnp.full((rows,), t, jnp.int32).at[dest].set((order // MOE_TOPK).astype(jnp.int32))
    buf_w = jnp.zeros((rows,), F32).at[dest].set(gate.reshape(-1)[order])
    pos = jnp.zeros((tk,), jnp.int32).at[order].set(dest)
    block_e = jnp.minimum(jnp.searchsorted(pad_end, jnp.arange(n_blocks, dtype=jnp.int32) * tm, side='right'),
                          MOE_EXPERTS - 1).astype(jnp.int32)
    n_used = (pad_end[-1:] // tm).astype(jnp.int32)
    xg = jnp.concatenate([h_bf, jnp.zeros((1, d), BF16)], axis=0)[buf_tok]
    yb = _moe_blocks(block_e, n_used, xg, buf_w[:, None], wg, wu, wd, tm)
    return jnp.sum(yb[pos].reshape(t, MOE_TOPK, d), axis=1)


def _router_kernel(h_ref, w_ref, o_ref):
    o_ref[...] = jnp.dot(h_ref[...], w_ref[...], preferred_element_type=F32, precision=lax.Precision.HIGHEST)


def _router_logits(h, router_w):
    t, d = h.shape
    e = router_w.shape[1]
    tm = min(1024, t)
    return pl.pallas_call(
        _router_kernel,
        grid=(t // tm,),
        in_specs=[pl.BlockSpec((tm, d), lambda i: (i, 0)), pl.BlockSpec((d, e), lambda i: (0, 0))],
        out_specs=pl.BlockSpec((tm, e), lambda i: (i, 0)),
        out_shape=jax.ShapeDtypeStruct((t, e), F32),
        compiler_params=_params(("parallel",)),
        name="router_logits",
    )(h, router_w)


def _pad_cols(w, n):
    return jnp.pad(w, ((0, 0), (0, n - w.shape[1])))


def kernel(x, router_w, router_b, even_w_in, even_w_out, diff_lam_q1, diff_lam_k1, diff_lam_q2, diff_lam_k2,
           diff_subln_g, ssm_conv_w, ssm_conv_b, ssm_dt_bias, ssm_a_log, ssm_d, ssm_norm_g, odd_w_in,
           odd_w_out, nsa_pe_k, nsa_pe_v, nsa_ck_w1, nsa_ck_w2, nsa_cv_w1, nsa_cv_w2, ln_mix_g, ln_mix_b,
           ln_ffn_g, ln_ffn_b, moe_w_gate, moe_w_up, moe_w_down):
    b, s, d = x.shape
    t = b * s
    xf = x.reshape(t, d)
    xb = xf.astype(BF16)
    for layer in range(DEPTH):
        i = layer // 2
        if layer % 2 == 0:
            lam_init = 0.8 - 0.6 * math.exp(-0.3 * layer)
            w_in = even_w_in[i]
            n_attn = 3 * DIFF_WIDTH
            qkv = _matmul(xb, w_in[:, :n_attn].astype(BF16), BF16, 1024, 1024).reshape(b, s, n_attn)
            n_ssm = -(-(w_in.shape[1] - n_attn) // LANES) * LANES
            pf = _matmul(xb, _pad_cols(w_in[:, n_attn:], n_ssm).astype(BF16), F32, 1024, n_ssm // 3)
            pf = pf.reshape(b, s, n_ssm)
            y_attn = _diff_attention(qkv, diff_lam_q1[i], diff_lam_k1[i], diff_lam_q2[i], diff_lam_k2[i],
                                     diff_subln_g[i], lam_init, 256)
            z = pf[..., :SSM_INNER]
            xbc = pf[..., SSM_INNER:SSM_INNER + SSM_CONV_CH]
            dt_raw = pf[..., SSM_INNER + SSM_CONV_CH:SSM_INNER + SSM_CONV_CH + SSM_HEADS]
            y_ssm = _mamba2_mixer(z, xbc, dt_raw, ssm_conv_w[i], ssm_conv_b[i], ssm_dt_bias[i], ssm_a_log[i],
                                  ssm_d[i], ssm_norm_g[i])
            mix_in = jnp.concatenate([y_attn, y_ssm.astype(BF16)], axis=-1).reshape(t, -1)
            w_out = even_w_out[i]
        else:
            w_in = odd_w_in[i]
            n_a = NSA_Q_WIDTH + 4 * NSA_KV_WIDTH
            wq = w_in[:, :NSA_Q_WIDTH]
            wkv = w_in[:, NSA_Q_WIDTH:NSA_Q_WIDTH + 6 * NSA_KV_WIDTH]
            wgate = w_in[:, NSA_Q_WIDTH + 6 * NSA_KV_WIDTH:]
            w_a = jnp.concatenate([wq, wkv[:, 2 * NSA_KV_WIDTH:]], axis=1)
            wgate = jnp.pad(wgate.reshape(d, NSA_KV_GROUPS, 3 * NSA_REP),
                            ((0, 0), (0, 0), (0, LANES - 3 * NSA_REP))).reshape(d, NSA_KV_GROUPS * LANES)
            w_b = jnp.concatenate([wkv[:, :2 * NSA_KV_WIDTH], wgate], axis=1)
            a_bf = _matmul(xb, w_a.astype(BF16), BF16, 1024, 1024).reshape(b, s, n_a)
            b_f32 = _matmul(xb, w_b.astype(BF16), F32, 1024, 512).reshape(b, s, -1)
            k_cmp, v_cmp = _compress(b_f32, nsa_pe_k[i], nsa_pe_v[i], nsa_ck_w1[i], nsa_cv_w1[i],
                                     nsa_ck_w2[i], nsa_cv_w2[i])
            mix_in = _nsa_attention(a_bf, b_f32, k_cmp, v_cmp, 128, 256).reshape(t, -1)
            w_out = odd_w_out[i]
        h, h_bf = _matmul_res_ln(mix_in, w_out.astype(BF16), xf, ln_mix_g[layer], ln_mix_b[layer], 256)
        f = _moe_ffn(h, h_bf, router_w, router_b, moe_w_gate[layer].astype(BF16),
                     moe_w_up[layer].astype(BF16), moe_w_down[layer].astype(BF16), 256)
        xf, xb = _add_ln(h, f, ln_ffn_g[layer], ln_ffn_b[layer], 512)
    return xf.reshape(b, s, d)
```

```python
import functools
import math

import jax
import jax.numpy as jnp
import numpy as np
from jax import lax
from jax.experimental import pallas as pl
from jax.experimental.pallas import tpu as pltpu

F32 = jnp.float32
BF16 = jnp.bfloat16

D_MODEL = 2048
DEPTH = 2
DEEPNORM_ALPHA = (2.0 * DEPTH) ** 0.25
LN_EPS = 1e-5

DIFF_HEADS = 8
DIFF_HEAD_DIM = 64
DIFF_WIDTH = DIFF_HEADS * 2 * DIFF_HEAD_DIM

SSM_INNER = D_MODEL // 2
SSM_HEAD_DIM = 64
SSM_HEADS = SSM_INNER // SSM_HEAD_DIM
SSM_GROUPS = 2
SSM_STATE = 128
SSM_CONV = 4
SSM_CHUNK = 128
SSM_CONV_CH = SSM_INNER + 2 * SSM_GROUPS * SSM_STATE

NSA_HEADS = 16
NSA_KV_GROUPS = 4
NSA_REP = NSA_HEADS // NSA_KV_GROUPS
NSA_HEAD_DIM = 128
NSA_CMP_LEN = 32
NSA_CMP_STRIDE = 16
NSA_SEL_LEN = 64
NSA_SEL_COUNT = 16
NSA_WINDOW = 512
NSA_Q_WIDTH = NSA_HEADS * NSA_HEAD_DIM
NSA_KV_WIDTH = NSA_KV_GROUPS * NSA_HEAD_DIM

MOE_GROUPS = 8
MOE_PER_GROUP = 4
MOE_EXPERTS = MOE_GROUPS * MOE_PER_GROUP
MOE_TOPK = 2
MOE_FF = 1024

NEG = -1e30
BIG = 1e30
LANES = 128
ONES_ROWS = 16
LOG2E = math.log2(math.e)
VMEM_LIMIT = 56 * 1024 * 1024


def _params(sem):
    return pltpu.CompilerParams(dimension_semantics=sem, vmem_limit_bytes=VMEM_LIMIT)


def _dot(a, b):
    return jnp.dot(a, b, preferred_element_type=F32)


def _dot_nt(a, b):
    return lax.dot_general(a, b, (((1,), (1,)), ((), ())), preferred_element_type=F32)


def _mm_kernel(x_ref, w_ref, o_ref):
    o_ref[...] = _dot(x_ref[...], w_ref[...]).astype(o_ref.dtype)


def _matmul(x, w, out_dtype, tm, tn):
    m, k = x.shape
    n = w.shape[1]
    return pl.pallas_call(
        _mm_kernel,
        grid=(n // tn, m // tm),
        in_specs=[pl.BlockSpec((tm, k), lambda j, i: (i, 0)),
                  pl.BlockSpec((k, tn), lambda j, i: (0, j))],
        out_specs=pl.BlockSpec((tm, tn), lambda j, i: (i, j)),
        out_shape=jax.ShapeDtypeStruct((m, n), out_dtype),
        compiler_params=_params(("parallel", "parallel")),
        name="proj_matmul",
    )(x, w)


def _layer_norm_rows(y, g, b):
    mu = jnp.mean(y, -1, keepdims=True)
    yc = y - mu
    var = jnp.mean(yc * yc, -1, keepdims=True)
    return yc * lax.rsqrt(var + LN_EPS) * g + b


def _mm_ln_kernel(a_ref, w_ref, res_ref, g_ref, b_ref, o_ref, ob_ref):
    y = _dot(a_ref[...], w_ref[...]) + DEEPNORM_ALPHA * res_ref[...]
    h = _layer_norm_rows(y, g_ref[...], b_ref[...])
    o_ref[...] = h
    ob_ref[...] = h.astype(BF16)


def _matmul_res_ln(a, w, res, g, b, tm):
    m, k = a.shape
    n = w.shape[1]
    return pl.pallas_call(
        _mm_ln_kernel,
        grid=(m // tm,),
        in_specs=[pl.BlockSpec((tm, k), lambda i: (i, 0)),
                  pl.BlockSpec((k, n), lambda i: (0, 0)),
                  pl.BlockSpec((tm, n), lambda i: (i, 0)),
                  pl.BlockSpec((1, n), lambda i: (0, 0)),
                  pl.BlockSpec((1, n), lambda i: (0, 0))],
        out_specs=[pl.BlockSpec((tm, n), lambda i: (i, 0)),
                   pl.BlockSpec((tm, n), lambda i: (i, 0))],
        out_shape=[jax.ShapeDtypeStruct((m, n), F32), jax.ShapeDtypeStruct((m, n), BF16)],
        compiler_params=_params(("parallel",)),
        name="outproj_ln",
    )(a, w, res, g.reshape(1, n), b.reshape(1, n))


def _add_ln_kernel(h_ref, f_ref, g_ref, b_ref, o_ref, ob_ref):
    y = DEEPNORM_ALPHA * h_ref[...] + f_ref[...]
    x = _layer_norm_rows(y, g_ref[...], b_ref[...])
    o_ref[...] = x
    ob_ref[...] = x.astype(BF16)


def _add_ln(h, f, g, b, tm):
    m, n = h.shape
    row = pl.BlockSpec((tm, n), lambda i: (i, 0))
    vec = pl.BlockSpec((1, n), lambda i: (0, 0))
    return pl.pallas_call(
        _add_ln_kernel,
        grid=(m // tm,),
        in_specs=[row, row, vec, vec],
        out_specs=[row, row],
        out_shape=[jax.ShapeDtypeStruct((m, n), F32), jax.ShapeDtypeStruct((m, n), BF16)],
        compiler_params=_params(("parallel",)),
        name="ffn_add_ln",
    )(h, f, g.reshape(1, n), b.reshape(1, n))


def _softmax_pv(st_ref, pt_ref, vt_aug, m_ref, al_ref, acc_ref, *, c_exp, mask=None):
    rows = st_ref.shape[1]
    for c in range(rows // LANES):
        cs = slice(c * LANES, (c + 1) * LANES)
        s = st_ref[:, cs]
        if mask is not None:
            s = mask(s, c)
        m_old = m_ref[:, cs]
        m_new = jnp.maximum(m_old, jnp.max(s, axis=0, keepdims=True))
        pt_ref[:, cs] = jnp.exp2((s - m_new[0:1]) * c_exp).astype(BF16)
        al_ref[:, cs] = jnp.exp2((m_old - m_new) * c_exp)
        m_ref[:, cs] = m_new
    acc_ref[...] = acc_ref[...] * al_ref[0:1, :] + _dot(vt_aug, pt_ref[...])


def _flash_loop(n_below, scores, process):
    scores(0, 0)

    def pair(jj, carry):
        j = 2 * jj
        scores(j + 1, 1)
        process(j, 0, False)
        scores(j + 2, 0)
        process(j + 1, 1, False)
        return carry

    lax.fori_loop(0, n_below // 2, pair, 0)

    @pl.when(n_below % 2 == 0)
    def _():
        process(n_below, 0, True)

    @pl.when(n_below % 2 == 1)
    def _():
        scores(n_below, 1)
        process(n_below - 1, 0, False)
        process(n_below, 1, True)


def _transpose_values(v, tk):
    b, s, h, dv = v.shape
    vt = jnp.transpose(v.reshape(b, s // tk, tk, h, dv), (0, 3, 1, 4, 2))
    return jnp.concatenate([vt, jnp.ones((b, h, s // tk, ONES_ROWS, tk), v.dtype)], axis=3)


def _diff_kernel(lq1_ref, lk1_ref, lq2_ref, lk2_ref, g_ref, q_ref, k_ref, vt_ref, o_ref,
                 q2_ref, st0_ref, st1_ref, pt0_ref, pt1_ref, m_ref, al_ref, acc_ref, *, tq, lam_init):
    i = pl.program_id(2)
    st, pt = (st0_ref, st1_ref), (pt0_ref, pt1_ref)
    q = q_ref[...]
    lane = lax.broadcasted_iota(jnp.int32, (tq, LANES), 1)
    zero = jnp.zeros_like(q)
    q2 = jnp.concatenate([jnp.where(lane < DIFF_HEAD_DIM, q, zero),
                          jnp.where(lane >= DIFF_HEAD_DIM, q, zero)], axis=0)
    q2_ref[...] = q2 * jnp.asarray(DIFF_HEAD_DIM ** -0.5, BF16)
    m_ref[...] = jnp.full_like(m_ref, NEG)
    acc_ref[...] = jnp.zeros_like(acc_ref)

    def scores(j, slot):
        kt = k_ref[pl.ds(pl.multiple_of(j * tq, tq), tq), :]
        st[slot][...] = _dot_nt(kt, q2_ref[...])

    def diagonal_mask(s, c):
        key = lax.broadcasted_iota(jnp.int32, (tq, LANES), 0)
        qry = (c * LANES + lax.broadcasted_iota(jnp.int32, (tq, LANES), 1)) & (tq - 1)
        return jnp.where(key <= qry, s, NEG)

    def process(j, slot, masked):
        _softmax_pv(st[slot], pt[slot], vt_ref[j], m_ref, al_ref, acc_ref, c_exp=LOG2E,
                    mask=diagonal_mask if masked else None)

    _flash_loop(i, scores, process)
    acc = acc_ref[...]
    o = (acc[:LANES] / acc[LANES:LANES + 1]).T
    lam = (jnp.exp(jnp.sum(lq1_ref[...] * lk1_ref[...], -1, keepdims=True))
           - jnp.exp(jnp.sum(lq2_ref[...] * lk2_ref[...], -1, keepdims=True)) + lam_init)
    od = o[:tq] - lam * o[tq:]
    y = od * lax.rsqrt(jnp.mean(od * od, -1, keepdims=True) + 1e-5) * g_ref[...]
    o_ref[...] = (y * (1.0 - lam_init)).astype(o_ref.dtype)


def _diff_attention(qkv, lq1, lk1, lq2, lk2, subln_g, lam_init, tq):
    b, s, _ = qkv.shape
    rows = 2 * tq
    dv = 2 * DIFF_HEAD_DIM
    vt = _transpose_values(qkv[..., 2 * DIFF_WIDTH:].reshape(b, s, DIFF_HEADS, dv), tq)
    vec64 = pl.BlockSpec((1, DIFF_HEAD_DIM), lambda bb, h, i: (0, 0))
    return pl.pallas_call(
        functools.partial(_diff_kernel, tq=tq, lam_init=lam_init),
        grid=(b, DIFF_HEADS, s // tq),
        in_specs=[vec64, vec64, vec64, vec64,
                  pl.BlockSpec((1, LANES), lambda bb, h, i: (0, 0)),
                  pl.BlockSpec((None, tq, LANES), lambda bb, h, i: (bb, i, h)),
                  pl.BlockSpec((None, s, LANES), lambda bb, h, i: (bb, 0, DIFF_HEADS + h)),
                  pl.BlockSpec((None, None, s // tq, dv + ONES_ROWS, tq), lambda bb, h, i: (bb, h, 0, 0, 0))],
        out_specs=pl.BlockSpec((None, tq, LANES), lambda bb, h, i: (bb, i, h)),
        out_shape=jax.ShapeDtypeStruct((b, s, DIFF_WIDTH), BF16),
        scratch_shapes=[pltpu.VMEM((rows, LANES), BF16), pltpu.VMEM((tq, rows), F32), pltpu.VMEM((tq, rows), F32),
                        pltpu.VMEM((tq, rows), BF16), pltpu.VMEM((tq, rows), BF16),
                        pltpu.VMEM((8, rows), F32), pltpu.VMEM((8, rows), F32),
                        pltpu.VMEM((dv + ONES_ROWS, rows), F32)],
        compiler_params=_params(("parallel", "parallel", "arbitrary")),
        name="diff_attention",
    )(lq1.reshape(1, -1), lk1.reshape(1, -1), lq2.reshape(1, -1), lk2.reshape(1, -1),
      subln_g.reshape(1, -1), qkv, qkv, vt)


def _segsum_exp(a):
    l = a.shape[-1]
    cs = jnp.cumsum(a, -1)
    diff = cs[..., :, None] - cs[..., None, :]
    mask = jnp.tril(jnp.ones((l, l), bool))
    return jnp.exp(jnp.where(mask, diff, -jnp.inf))


def _ssd_chunked(x, dt, a, bm, cm):
    b, s = x.shape[:2]
    nc = s // SSM_CHUNK
    L = SSM_CHUNK
    J = SSM_HEADS // SSM_GROUPS
    xd = (x.astype(F32) * dt[..., None]).reshape(b, nc, L, SSM_GROUPS, J, SSM_HEAD_DIM)
    adt = jnp.transpose((dt * a).reshape(b, nc, L, SSM_GROUPS, J), (0, 3, 4, 1, 2))
    bc = bm.astype(F32).reshape(b, nc, L, SSM_GROUPS, SSM_STATE)
    cc = cm.astype(F32).reshape(b, nc, L, SSM_GROUPS, SSM_STATE)
    a_cum = jnp.cumsum(adt, -1)
    cb = jnp.einsum('bclgn,bcsgn->bgcls', cc, bc)
    w = cb[:, :, None] * _segsum_exp(adt)
    y_diag = jnp.einsum('bgjcls,bcsgjp->bclgjp', w, xd)
    decay_states = jnp.exp(a_cum[..., -1:] - a_cum)
    states = jnp.einsum('bclgn,bgjcl,bclgjp->cbgjpn', bc, decay_states, xd)
    chunk_decay = jnp.moveaxis(jnp.exp(a_cum[..., -1]), -1, 0)

    def step(h, inp):
        dec, st = inp
        return h * dec[..., None, None] + st, h

    h0 = jnp.zeros(states.shape[1:], F32)
    _, prev = lax.scan(step, h0, (chunk_decay, states))
    y_off = jnp.einsum('bclgn,cbgjpn,bgjcl->bclgjp', cc, prev, jnp.exp(a_cum))
    return (y_diag + y_off).reshape(b, s, SSM_HEADS, SSM_HEAD_DIM)


def _mamba2_mixer(z, xbc, dt_raw, conv_w, conv_b, dt_bias, a_log, d_skip, norm_g):
    b, s, _ = z.shape
    c = xbc.shape[-1]
    y = lax.conv_general_dilated(xbc, conv_w[:, None, :], window_strides=(1,),
                                 padding=[(SSM_CONV - 1, 0)],
                                 dimension_numbers=('NWC', 'WIO', 'NWC'),
                                 feature_group_count=c)
    xbc = jax.nn.silu(y + conv_b)
    xs, bm, cm = jnp.split(xbc, [SSM_INNER, SSM_INNER + SSM_GROUPS * SSM_STATE], axis=-1)
    xs = xs.reshape(b, s, SSM_HEADS, SSM_HEAD_DIM)
    bm = bm.reshape(b, s, SSM_GROUPS, SSM_STATE)
    cm = cm.reshape(b, s, SSM_GROUPS, SSM_STATE)
    dt = jax.nn.softplus(dt_raw + dt_bias)
    a = -jnp.exp(a_log)
    y = _ssd_chunked(xs, dt, a, bm, cm)
    y = y + d_skip[:, None] * xs
    y = y.reshape(b, s, SSM_INNER) * jax.nn.silu(z)
    yg = y.reshape(b, s, SSM_GROUPS, SSM_INNER // SSM_GROUPS)
    yg = yg * lax.rsqrt(jnp.mean(yg * yg, -1, keepdims=True) + 1e-5)
    return yg.reshape(b, s, SSM_INNER) * norm_g


def _cmp_kernel(x_ref, pek_ref, pev_ref, w1k_ref, w1v_ref, w2k_ref, w2v_ref, ko_ref, vo_ref,
                top_ref, bot_ref, *, nb):
    l = pl.program_id(1)
    g4 = NSA_KV_GROUPS

    @pl.when(l == 0)
    def _():
        top_ref[...] = jnp.zeros_like(top_ref)
        bot_ref[...] = jnp.zeros_like(bot_ref)

    x = x_ref[...]
    for kv, (pe_ref, w1_ref) in enumerate(((pek_ref, w1k_ref), (pev_ref, w1v_ref))):
        x4 = jnp.concatenate([x[:, kv * NSA_KV_WIDTH + g * LANES: kv * NSA_KV_WIDTH + (g + 1) * LANES]
                              for g in range(g4)], axis=0)
        top_ref[kv] += _dot((x4 + pe_ref[pl.ds(l, 1), :]).astype(BF16), w1_ref[l])
        bot_ref[kv] += _dot((x4 + pe_ref[pl.ds(l + NSA_CMP_STRIDE, 1), :]).astype(BF16),
                            w1_ref[l + NSA_CMP_STRIDE])

    @pl.when(l == NSA_CMP_STRIDE - 1)
    def _():
        last = lax.broadcasted_iota(jnp.int32, (nb, LANES), 0) == nb - 1
        for kv, (w2_ref, o_ref) in enumerate(((w2k_ref, ko_ref), (w2v_ref, vo_ref))):
            for g in range(g4):
                top = top_ref[kv, g * nb:(g + 1) * nb, :]
                bot = bot_ref[kv, g * nb:(g + 1) * nb, :]
                hid = top + pltpu.roll(bot, nb - 1, 0)
                out = _dot(jax.nn.gelu(hid, approximate=True).astype(BF16), w2_ref[...])
                o_ref[g] = jnp.where(last, 0.0, out).astype(o_ref.dtype)


def _compress(bf, pe_k, pe_v, w1k, w1v, w2k, w2v):
    b, s, w = bf.shape
    nb = s // NSA_CMP_STRIDE
    xv = bf.reshape(b, nb, NSA_CMP_STRIDE * w)
    full = lambda shape: pl.BlockSpec(shape, lambda bb, l: (0,) * len(shape))
    out_spec = pl.BlockSpec((None, NSA_KV_GROUPS, nb, LANES), lambda bb, l: (bb, 0, 0, 0))
    out_sd = jax.ShapeDtypeStruct((b, NSA_KV_GROUPS, nb, LANES), BF16)
    w1 = lambda a: a.astype(BF16).reshape(NSA_CMP_LEN, NSA_HEAD_DIM, -1)
    return pl.pallas_call(
        functools.partial(_cmp_kernel, nb=nb),
        grid=(b, NSA_CMP_STRIDE),
        in_specs=[pl.BlockSpec((None, nb, w), lambda bb, l: (bb, 0, l)),
                  full((NSA_CMP_LEN, LANES)), full((NSA_CMP_LEN, LANES)),
                  full((NSA_CMP_LEN, NSA_HEAD_DIM, LANES)), full((NSA_CMP_LEN, NSA_HEAD_DIM, LANES)),
                  full((LANES, LANES)), full((LANES, LANES))],
        out_specs=[out_spec, out_spec],
        out_shape=[out_sd, out_sd],
        scratch_shapes=[pltpu.VMEM((2, NSA_KV_GROUPS * nb, LANES), F32),
                        pltpu.VMEM((2, NSA_KV_GROUPS * nb, LANES), F32)],
        compiler_params=_params(("parallel", "arbitrary")),
        name="nsa_compress",
    )(xv, pe_k, pe_v, w1(w1k), w1(w1v), w2k.astype(BF16), w2v.astype(BF16))


def _nsa_kernel(q_ref, gate_ref, kc_ref, vct_ref, ks_ref, vst_ref, kw_ref, vwt_ref, blk_ref, ovt_ref, o_ref,
                qa_ref, sc_ref, pc_ref, ps_ref, st0_ref, st1_ref, pt0_ref, pt1_ref, sw_ref, pw_ref,
                m_ref, al_ref, acc_ref, mw_ref, alw_ref, accw_ref, *, tk, seq, slab):
    i = pl.program_id(2)
    tq = LANES
    t0 = i * tq
    rep = NSA_REP
    ncp = seq // NSA_CMP_STRIDE
    scale = NSA_HEAD_DIM ** -0.5
    c_exp = scale * LOG2E
    q = q_ref[...]
    for r in range(rep):
        qa_ref[r * tq:(r + 1) * tq, 0:LANES] = q[:, r * LANES:(r + 1) * LANES]
    qs = qa_ref[:, 0:LANES]
    t_lane = t0 + lax.broadcasted_iota(jnp.int32, (1, LANES), 1)

    sc_ref[...] = _dot_nt(kc_ref[...], qs)
    n_end = lax.broadcasted_iota(jnp.int32, (ncp, LANES), 0) * NSA_CMP_STRIDE + (NSA_CMP_LEN - 1)
    cmask = n_end <= t_lane
    for r in range(rep):
        cs = slice(r * tq, (r + 1) * tq)
        s = jnp.where(cmask, sc_ref[:, cs] * scale, NEG)
        e = jnp.where(cmask, jnp.exp(s - jnp.max(s, axis=0, keepdims=True)), 0.0)
        den = jnp.sum(e, axis=0, keepdims=True)
        p = e / jnp.where(den > 0, den, 1.0)
        pc_ref[:, cs] = p.astype(BF16)
        if r == 0:
            ps_ref[...] = p
        else:
            ps_ref[...] += p
    o_cmp_t = _dot(vct_ref[...], pc_ref[...])

    ps = ps_ref[...]
    p_hi = ps.astype(BF16)
    p_lo = (ps - p_hi.astype(F32)).astype(BF16)
    imp_t = _dot(ovt_ref[...], p_hi) + _dot(ovt_ref[...], p_lo)
    cur = t_lane >> 6
    mi = lax.broadcasted_iota(jnp.int32, (LANES, LANES), 0)
    forced = (mi == 0) | (mi == cur) | (mi == cur - 1)
    vals = jnp.where(mi <= cur, jnp.where(forced, BIG, imp_t), NEG)
    pen = jnp.full((LANES, LANES), -1e9, F32)
    mf = mi.astype(F32)
    for _ in range(NSA_SEL_COUNT):
        mx = jnp.max(vals, axis=0, keepdims=True)
        first = jnp.min(jnp.where(vals == mx, mf, float(LANES)), axis=0, keepdims=True)
        hit = mf == first
        pen = jnp.where(hit, 0.0, pen)
        vals = jnp.where(hit, -jnp.inf, vals)
    pen_q = pen.T.astype(BF16)
    for r in range(rep):
        qa_ref[r * tq:(r + 1) * tq, LANES:2 * LANES] = pen_q

    m_ref[...] = jnp.full_like(m_ref, NEG)
    acc_ref[...] = jnp.zeros_like(acc_ref)
    st, pt = (st0_ref, st1_ref), (pt0_ref, pt1_ref)

    def scores(j, slot):
        k0 = pl.multiple_of(j * tk, tk)
        ka = jnp.concatenate([ks_ref[pl.ds(k0, tk), :], blk_ref[pl.ds(k0, tk), :]], axis=1)
        st[slot][...] = _dot_nt(ka, qa_ref[...])

    def process(j, slot, masked):
        def causal(s, c):
            kpos = j * tk + lax.broadcasted_iota(jnp.int32, (tk, LANES), 0)
            return jnp.where(kpos <= t_lane, s, NEG)
        _softmax_pv(st[slot], pt[slot], vst_ref[j], m_ref, al_ref, acc_ref, c_exp=c_exp,
                    mask=causal if masked else None)

    _flash_loop((t0 + tq + tk - 1) // tk - 1, scores, process)
    acc = acc_ref[...]
    o_sel_t = acc[:LANES] / acc[LANES:LANES + 1]

    ws = pl.multiple_of(jnp.clip(t0 - NSA_WINDOW, 0, seq - slab), LANES)
    sw_ref[...] = _dot_nt(kw_ref[pl.ds(ws, slab), :], qs)
    mw_ref[...] = jnp.full_like(mw_ref, NEG)
    accw_ref[...] = jnp.zeros_like(accw_ref)
    jb = ws // LANES
    v_slab = jnp.concatenate([vwt_ref[jb + u] for u in range(slab // LANES)], axis=1)

    def window(s, c):
        kpos = ws + lax.broadcasted_iota(jnp.int32, (slab, LANES), 0)
        return jnp.where(kpos <= t_lane, jnp.where(kpos > t_lane - NSA_WINDOW, s, NEG), NEG)

    _softmax_pv(sw_ref, pw_ref, v_slab, mw_ref, alw_ref, accw_ref, c_exp=c_exp, mask=window)
    accw = accw_ref[...]
    o_win_t = accw[:LANES] / accw[LANES:LANES + 1]

    g_t = jax.nn.sigmoid(gate_ref[...]).T
    for r in range(rep):
        cs = slice(r * tq, (r + 1) * tq)
        out_t = (g_t[3 * r:3 * r + 1] * o_cmp_t[:, cs] + g_t[3 * r + 1:3 * r + 2] * o_sel_t[:, cs]
                 + g_t[3 * r + 2:3 * r + 3] * o_win_t[:, cs])
        o_ref[:, r * LANES:(r + 1) * LANES] = out_t.T.astype(o_ref.dtype)


def _nsa_attention(a_bf, b_f32, k_cmp, v_cmp, tk):
    b, s, _ = a_bf.shape
    g4 = NSA_KV_GROUPS
    tq = LANES
    rows = NSA_REP * tq
    ncp = s // NSA_CMP_STRIDE
    slab = NSA_WINDOW + tq
    dva = NSA_HEAD_DIM + ONES_ROWS
    assert s >= slab and s % tk == 0 and tk % tq == 0
    blk = (jnp.arange(s)[:, None] // NSA_SEL_LEN == jnp.arange(LANES)[None, :]).astype(BF16)
    cs = np.arange(ncp)[:, None] * NSA_CMP_STRIDE
    ss = np.arange(LANES)[None, :] * NSA_SEL_LEN
    ov = np.maximum(np.minimum(cs + NSA_CMP_LEN, ss + NSA_SEL_LEN) - np.maximum(cs, ss), 0) / NSA_CMP_LEN
    ov[ncp - 1] = 0.0
    qoff = NSA_Q_WIDTH // LANES
    group_values = lambda n: a_bf[..., NSA_Q_WIDTH + n * NSA_KV_WIDTH:NSA_Q_WIDTH + (n + 1) * NSA_KV_WIDTH].reshape(
        b, s, g4, NSA_HEAD_DIM)
    vst = _transpose_values(group_values(1), tk)
    vwt = _transpose_values(group_values(3), LANES)
    vct = jnp.swapaxes(v_cmp, 2, 3)
    k_spec = lambda n: pl.BlockSpec((None, s, LANES), lambda bb, g, i: (bb, 0, qoff + n * g4 + g))
    vt_spec = lambda t: pl.BlockSpec((None, None, s // t, dva, t), lambda bb, g, i: (bb, g, 0, 0, 0))
    return pl.pallas_call(
        functools.partial(_nsa_kernel, tk=tk, seq=s, slab=slab),
        grid=(b, g4, s // tq),
        in_specs=[pl.BlockSpec((None, tq, NSA_REP * LANES), lambda bb, g, i: (bb, i, g)),
                  pl.BlockSpec((None, tq, LANES), lambda bb, g, i: (bb, i, 2 * g4 + g)),
                  pl.BlockSpec((None, None, ncp, LANES), lambda bb, g, i: (bb, g, 0, 0)),
                  pl.BlockSpec((None, None, LANES, ncp), lambda bb, g, i: (bb, g, 0, 0)),
                  k_spec(0), vt_spec(tk), k_spec(2), vt_spec(LANES),
                  pl.BlockSpec((s, LANES), lambda bb, g, i: (0, 0)),
                  pl.BlockSpec((LANES, ncp), lambda bb, g, i: (0, 0))],
        out_specs=pl.BlockSpec((None, tq, NSA_REP * LANES), lambda bb, g, i: (bb, i, g)),
        out_shape=jax.ShapeDtypeStruct((b, s, NSA_Q_WIDTH), BF16),
        scratch_shapes=[pltpu.VMEM((rows, 2 * LANES), BF16),
                        pltpu.VMEM((ncp, rows), F32), pltpu.VMEM((ncp, rows), BF16), pltpu.VMEM((ncp, LANES), F32),
                        pltpu.VMEM((tk, rows), F32), pltpu.VMEM((tk, rows), F32),
                        pltpu.VMEM((tk, rows), BF16), pltpu.VMEM((tk, rows), BF16),
                        pltpu.VMEM((slab, rows), F32), pltpu.VMEM((slab, rows), BF16),
                        pltpu.VMEM((8, rows), F32), pltpu.VMEM((8, rows), F32), pltpu.VMEM((dva, rows), F32),
                        pltpu.VMEM((8, rows), F32), pltpu.VMEM((8, rows), F32), pltpu.VMEM((dva, rows), F32)],
        compiler_params=_params(("parallel", "parallel", "arbitrary")),
        name="nsa_attention",
    )(a_bf, b_f32, k_cmp, vct, a_bf, vst, a_bf, vwt, blk, jnp.asarray(ov.T, BF16))


def _moe_kernel(be_ref, nu_ref, x_ref, w_ref, wg_ref, wu_ref, wd_ref, o_ref):
    i = pl.program_id(0)

    @pl.when(i < nu_ref[0])
    def _():
        x = x_ref[...]
        hid = jax.nn.silu(_dot(x, wg_ref[...])) * _dot(x, wu_ref[...])
        o_ref[...] = _dot(hid.astype(BF16), wd_ref[...]) * w_ref[...]

    @pl.when(i >= nu_ref[0])
    def _():
        o_ref[...] = jnp.zeros_like(o_ref)


def _moe_blocks(block_e, n_used, xg, bw, wg, wu, wd, tm):
    rows, d = xg.shape
    ff = wg.shape[-1]
    grid_spec = pltpu.PrefetchScalarGridSpec(
        num_scalar_prefetch=2,
        grid=(rows // tm,),
        in_specs=[pl.BlockSpec((tm, d), lambda i, be, nu: (i, 0)),
                  pl.BlockSpec((tm, 1), lambda i, be, nu: (i, 0)),
                  pl.BlockSpec((None, d, ff), lambda i, be, nu: (be[i], 0, 0)),
                  pl.BlockSpec((None, d, ff), lambda i, be, nu: (be[i], 0, 0)),
                  pl.BlockSpec((None, ff, d), lambda i, be, nu: (be[i], 0, 0))],
        out_specs=pl.BlockSpec((tm, d), lambda i, be, nu: (i, 0)),
    )
    return pl.pallas_call(
        _moe_kernel,
        grid_spec=grid_spec,
        out_shape=jax.ShapeDtypeStruct((rows, d), F32),
        compiler_params=_params(("arbitrary",)),
        name="moe_experts",
    )(block_e, n_used, xg, bw, wg, wu, wd)


def _route(logits, router_b):
    s = jax.nn.sigmoid(logits)
    biased = (s + router_b).reshape(-1, MOE_GROUPS, MOE_PER_GROUP)
    top_val, top_idx = lax.top_k(biased, MOE_TOPK)
    grp = jnp.argmax(jnp.sum(top_val, -1), axis=-1)
    local = jnp.take_along_axis(top_idx, grp[:, None, None], axis=1)[:, 0]
    expert = grp[:, None] * MOE_PER_GROUP + local
    w = jnp.take_along_axis(s, expert, axis=1)
    return expert.astype(jnp.int32), w / jnp.sum(w, -1, keepdims=True)


def _moe_ffn(h, h_bf, router_w, router_b, wg, wu, wd, tm):
    t, d = h.shape
    logits = _router_logits(h, router_w)
    expert, gate = _route(logits, router_b)
    tk = t * MOE_TOPK
    n_blocks = tk // tm + MOE_EXPERTS
    rows = n_blocks * tm
    flat_e = expert.reshape(-1)
    onehot = (flat_e[:, None] == jnp.arange(MOE_EXPERTS, dtype=jnp.int32)[None, :]).astype(jnp.int32)
    csum = jnp.cumsum(onehot, axis=0)
    counts = csum[-1]
    padded = (counts + tm - 1) // tm * tm
    pad_end = jnp.cumsum(padded)
    pos = jnp.sum(onehot * (csum - 1 + (pad_end - padded)[None, :]), axis=1)
    buf_tok = jnp.full((rows,), t, jnp.int32).at[pos].set(jnp.arange(tk, dtype=jnp.int32) // MOE_TOPK)
    buf_w = jnp.zeros((rows,), F32).at[pos].set(gate.reshape(-1))
    block_e = jnp.minimum(jnp.searchsorted(pad_end, jnp.arange(n_blocks, dtype=jnp.int32) * tm, side='right'),
                          MOE_EXPERTS - 1).astype(jnp.int32)
    n_used = (pad_end[-1:] // tm).astype(jnp.int32)
    xg = jnp.concatenate([h_bf, jnp.zeros((1, d), BF16)], axis=0)[buf_tok]
    yb = _moe_blocks(block_e, n_used, xg, buf_w[:, None], wg, wu, wd, tm)
    return jnp.sum(yb[pos].reshape(t, MOE_TOPK, d), axis=1)


def _router_kernel(h_ref, w_ref, o_ref):
    o_ref[...] = jnp.dot(h_ref[...], w_ref[...], preferred_element_type=F32, precision=lax.Precision.HIGHEST)


def _router_logits(h, router_w):
    t, d = h.shape
    e = router_w.shape[1]
    tm = min(1024, t)
    return pl.pallas_call(
        _router_kernel,
        grid=(t // tm,),
        in_specs=[pl.BlockSpec((tm, d), lambda i: (i, 0)), pl.BlockSpec((d, e), lambda i: (0, 0))],
        out_specs=pl.BlockSpec((tm, e), lambda i: (i, 0)),
        out_shape=jax.ShapeDtypeStruct((t, e), F32),
        compiler_params=_params(("parallel",)),
        name="router_logits",
    )(h, router_w)


def _pad_cols(w, n):
    return jnp.pad(w, ((0, 0), (0, n - w.shape[1])))


def kernel(x, router_w, router_b, even_w_in, even_w_out, diff_lam_q1, diff_lam_k1, diff_lam_q2, diff_lam_k2,
           diff_subln_g, ssm_conv_w, ssm_conv_b, ssm_dt_bias, ssm_a_log, ssm_d, ssm_norm_g, odd_w_in,
           odd_w_out, nsa_pe_k, nsa_pe_v, nsa_ck_w1, nsa_ck_w2, nsa_cv_w1, nsa_cv_w2, ln_mix_g, ln_mix_b,
           ln_ffn_g, ln_ffn_b, moe_w_gate, moe_w_up, moe_w_down):
    b, s, d = x.shape
    t = b * s
    xf = x.reshape(t, d)
    xb = xf.astype(BF16)
    for layer in range(DEPTH):
        i = layer // 2
        if layer % 2 == 0:
            lam_init = 0.8 - 0.6 * math.exp(-0.3 * layer)
            w_in = even_w_in[i]
            n_attn = 3 * DIFF_WIDTH
            qkv = _matmul(xb, w_in[:, :n_attn].astype(BF16), BF16, 1024, 1024).reshape(b, s, n_attn)
            n_ssm = -(-(w_in.shape[1] - n_attn) // LANES) * LANES
            pf = _matmul(xb, _pad_cols(w_in[:, n_attn:], n_ssm).astype(BF16), F32, 1024, n_ssm // 3)
            pf = pf.reshape(b, s, n_ssm)
            y_attn = _diff_attention(qkv, diff_lam_q1[i], diff_lam_k1[i], diff_lam_q2[i], diff_lam_k2[i],
                                     diff_subln_g[i], lam_init, 256)
            z = pf[..., :SSM_INNER]
            xbc = pf[..., SSM_INNER:SSM_INNER + SSM_CONV_CH]
            dt_raw = pf[..., SSM_INNER + SSM_CONV_CH:SSM_INNER + SSM_CONV_CH + SSM_HEADS]
            y_ssm = _mamba2_mixer(z, xbc, dt_raw, ssm_conv_w[i], ssm_conv_b[i], ssm_dt_bias[i], ssm_a_log[i],
                                  ssm_d[i], ssm_norm_g[i])
            mix_in = jnp.concatenate([y_attn, y_ssm.astype(BF16)], axis=-1).reshape(t, -1)
            w_out = even_w_out[i]
        else:
            w_in = odd_w_in[i]
            n_a = NSA_Q_WIDTH + 4 * NSA_KV_WIDTH
            wq = w_in[:, :NSA_Q_WIDTH]
            wkv = w_in[:, NSA_Q_WIDTH:NSA_Q_WIDTH + 6 * NSA_KV_WIDTH]
            wgate = w_in[:, NSA_Q_WIDTH + 6 * NSA_KV_WIDTH:]
            w_a = jnp.concatenate([wq, wkv[:, 2 * NSA_KV_WIDTH:]], axis=1)
            wgate = jnp.pad(wgate.reshape(d, NSA_KV_GROUPS, 3 * NSA_REP),
                            ((0, 0), (0, 0), (0, LANES - 3 * NSA_REP))).reshape(d, NSA_KV_GROUPS * LANES)
            w_b = jnp.concatenate([wkv[:, :2 * NSA_KV_WIDTH], wgate], axis=1)
            a_bf = _matmul(xb, w_a.astype(BF16), BF16, 1024, 1024).reshape(b, s, n_a)
            b_f32 = _matmul(xb, w_b.astype(BF16), F32, 1024, 512).reshape(b, s, -1)
            k_cmp, v_cmp = _compress(b_f32, nsa_pe_k[i], nsa_pe_v[i], nsa_ck_w1[i], nsa_cv_w1[i],
                                     nsa_ck_w2[i], nsa_cv_w2[i])
            mix_in = _nsa_attention(a_bf, b_f32, k_cmp, v_cmp, 256).reshape(t, -1)
            w_out = odd_w_out[i]
        h, h_bf = _matmul_res_ln(mix_in, w_out.astype(BF16), xf, ln_mix_g[layer], ln_mix_b[layer], 256)
        f = _moe_ffn(h, h_bf, router_w, router_b, moe_w_gate[layer].astype(BF16),
                     moe_w_up[layer].astype(BF16), moe_w_down[layer].astype(BF16), 256)
        xf, xb = _add_ln(h, f, ln_ffn_g[layer], ln_ffn_b[layer], 512)
    return xf.reshape(b, s, d)
```

```python
import functools
import math

import jax
import jax.numpy as jnp
import numpy as np
from jax import lax
from jax.experimental import pallas as pl
from jax.experimental.pallas import tpu as pltpu

F32 = jnp.float32
BF16 = jnp.bfloat16

D_MODEL = 2048
DEPTH = 2
DEEPNORM_ALPHA = (2.0 * DEPTH) ** 0.25
LN_EPS = 1e-5

DIFF_HEADS = 8
DIFF_HEAD_DIM = 64
DIFF_WIDTH = DIFF_HEADS * 2 * DIFF_HEAD_DIM

SSM_INNER = D_MODEL // 2
SSM_HEAD_DIM = 64
SSM_HEADS = SSM_INNER // SSM_HEAD_DIM
SSM_GROUPS = 2
SSM_STATE = 128
SSM_CONV = 4
SSM_CHUNK = 128
SSM_CONV_CH = SSM_INNER + 2 * SSM_GROUPS * SSM_STATE

NSA_HEADS = 16
NSA_KV_GROUPS = 4
NSA_REP = NSA_HEADS // NSA_KV_GROUPS
NSA_HEAD_DIM = 128
NSA_CMP_LEN = 32
NSA_CMP_STRIDE = 16
NSA_SEL_LEN = 64
NSA_SEL_COUNT = 16
NSA_WINDOW = 512
NSA_Q_WIDTH = NSA_HEADS * NSA_HEAD_DIM
NSA_KV_WIDTH = NSA_KV_GROUPS * NSA_HEAD_DIM

MOE_GROUPS = 8
MOE_PER_GROUP = 4
MOE_EXPERTS = MOE_GROUPS * MOE_PER_GROUP
MOE_TOPK = 2
MOE_FF = 1024

NEG = -1e30
BIG = 1e30
LANES = 128
ONES_ROWS = 16
LOG2E = math.log2(math.e)
VMEM_LIMIT = 56 * 1024 * 1024


def _params(sem):
    return pltpu.CompilerParams(dimension_semantics=sem, vmem_limit_bytes=VMEM_LIMIT)


def _dot(a, b):
    return jnp.dot(a, b, preferred_element_type=F32)


def _dot_nt(a, b):
    return lax.dot_general(a, b, (((1,), (1,)), ((), ())), preferred_element_type=F32)


def _mm_kernel(x_ref, w_ref, o_ref):
    o_ref[...] = _dot(x_ref[...], w_ref[...]).astype(o_ref.dtype)


def _matmul(x, w, out_dtype, tm, tn):
    m, k = x.shape
    n = w.shape[1]
    return pl.pallas_call(
        _mm_kernel,
        grid=(n // tn, m // tm),
        in_specs=[pl.BlockSpec((tm, k), lambda j, i: (i, 0)),
                  pl.BlockSpec((k, tn), lambda j, i: (0, j))],
        out_specs=pl.BlockSpec((tm, tn), lambda j, i: (i, j)),
        out_shape=jax.ShapeDtypeStruct((m, n), out_dtype),
        compiler_params=_params(("parallel", "parallel")),
        name="proj_matmul",
    )(x, w)


def _layer_norm_rows(y, g, b):
    mu = jnp.mean(y, -1, keepdims=True)
    yc = y - mu
    var = jnp.mean(yc * yc, -1, keepdims=True)
    return yc * lax.rsqrt(var + LN_EPS) * g + b


def _mm_ln_kernel(a_ref, w_ref, res_ref, g_ref, b_ref, o_ref, ob_ref):
    y = _dot(a_ref[...], w_ref[...]) + DEEPNORM_ALPHA * res_ref[...]
    h = _layer_norm_rows(y, g_ref[...], b_ref[...])
    o_ref[...] = h
    ob_ref[...] = h.astype(BF16)


def _matmul_res_ln(a, w, res, g, b, tm):
    m, k = a.shape
    n = w.shape[1]
    return pl.pallas_call(
        _mm_ln_kernel,
        grid=(m // tm,),
        in_specs=[pl.BlockSpec((tm, k), lambda i: (i, 0)),
                  pl.BlockSpec((k, n), lambda i: (0, 0)),
                  pl.BlockSpec((tm, n), lambda i: (i, 0)),
                  pl.BlockSpec((1, n), lambda i: (0, 0)),
                  pl.BlockSpec((1, n), lambda i: (0, 0))],
        out_specs=[pl.BlockSpec((tm, n), lambda i: (i, 0)),
                   pl.BlockSpec((tm, n), lambda i: (i, 0))],
        out_shape=[jax.ShapeDtypeStruct((m, n), F32), jax.ShapeDtypeStruct((m, n), BF16)],
        compiler_params=_params(("parallel",)),
        name="outproj_ln",
    )(a, w, res, g.reshape(1, n), b.reshape(1, n))


def _combine_ln_kernel(h_ref, y_ref, w_ref, g_ref, b_ref, o_ref, ob_ref):
    w = w_ref[...]
    y = DEEPNORM_ALPHA * h_ref[...] + w[:, 0:1] * y_ref[0] + w[:, 1:2] * y_ref[1]
    x = _layer_norm_rows(y, g_ref[...], b_ref[...])
    o_ref[...] = x
    ob_ref[...] = x.astype(BF16)


def _combine_ln(h, y2, w, g, b, tm):
    m, n = h.shape
    row = pl.BlockSpec((tm, n), lambda i: (i, 0))
    vec = pl.BlockSpec((1, n), lambda i: (0, 0))
    return pl.pallas_call(
        _combine_ln_kernel,
        grid=(m // tm,),
        in_specs=[row, pl.BlockSpec((MOE_TOPK, tm, n), lambda i: (0, i, 0)),
                  pl.BlockSpec((tm, MOE_TOPK), lambda i: (i, 0)), vec, vec],
        out_specs=[row, row],
        out_shape=[jax.ShapeDtypeStruct((m, n), F32), jax.ShapeDtypeStruct((m, n), BF16)],
        compiler_params=_params(("parallel",)),
        name="ffn_combine_ln",
    )(h, y2, w, g.reshape(1, n), b.reshape(1, n))


def _softmax_pv(st_ref, pt_ref, vt_aug, m_ref, al_ref, acc_ref, *, c_exp, mask=None):
    rows = st_ref.shape[1]
    for c in range(rows // LANES):
        cs = slice(c * LANES, (c + 1) * LANES)
        s = st_ref[:, cs]
        if mask is not None:
            s = mask(s, c)
        m_old = m_ref[:, cs]
        m_new = jnp.maximum(m_old, jnp.max(s, axis=0, keepdims=True))
        pt_ref[:, cs] = jnp.exp2((s - m_new[0:1]) * c_exp).astype(BF16)
        al_ref[:, cs] = jnp.exp2((m_old - m_new) * c_exp)
        m_ref[:, cs] = m_new
    acc_ref[...] = acc_ref[...] * al_ref[0:1, :] + _dot(vt_aug, pt_ref[...])


def _flash_loop(n_below, scores, process):
    scores(0, 0)

    def pair(jj, carry):
        j = 2 * jj
        scores(j + 1, 1)
        process(j, 0, False)
        scores(j + 2, 0)
        process(j + 1, 1, False)
        return carry

    lax.fori_loop(0, n_below // 2, pair, 0)

    @pl.when(n_below % 2 == 0)
    def _():
        process(n_below, 0, True)

    @pl.when(n_below % 2 == 1)
    def _():
        scores(n_below, 1)
        process(n_below - 1, 0, False)
        process(n_below, 1, True)


def _transpose_values(v, tk):
    b, s, h, dv = v.shape
    vt = jnp.transpose(v.reshape(b, s // tk, tk, h, dv), (0, 3, 1, 4, 2))
    return jnp.concatenate([vt, jnp.ones((b, h, s // tk, ONES_ROWS, tk), v.dtype)], axis=3)


def _diff_kernel(lq1_ref, lk1_ref, lq2_ref, lk2_ref, g_ref, q_ref, k_ref, vt_ref, o_ref,
                 q2_ref, st0_ref, st1_ref, pt0_ref, pt1_ref, m_ref, al_ref, acc_ref, *, tq, lam_init):
    i = pl.program_id(2)
    st, pt = (st0_ref, st1_ref), (pt0_ref, pt1_ref)
    q = q_ref[...]
    lane = lax.broadcasted_iota(jnp.int32, (tq, LANES), 1)
    zero = jnp.zeros_like(q)
    q2 = jnp.concatenate([jnp.where(lane < DIFF_HEAD_DIM, q, zero),
                          jnp.where(lane >= DIFF_HEAD_DIM, q, zero)], axis=0)
    q2_ref[...] = q2 * jnp.asarray(DIFF_HEAD_DIM ** -0.5, BF16)
    m_ref[...] = jnp.full_like(m_ref, NEG)
    acc_ref[...] = jnp.zeros_like(acc_ref)

    def scores(j, slot):
        kt = k_ref[pl.ds(pl.multiple_of(j * tq, tq), tq), :]
        st[slot][...] = _dot_nt(kt, q2_ref[...])

    def diagonal_mask(s, c):
        key = lax.broadcasted_iota(jnp.int32, (tq, LANES), 0)
        qry = (c * LANES + lax.broadcasted_iota(jnp.int32, (tq, LANES), 1)) & (tq - 1)
        return jnp.where(key <= qry, s, NEG)

    def process(j, slot, masked):
        _softmax_pv(st[slot], pt[slot], vt_ref[j], m_ref, al_ref, acc_ref, c_exp=LOG2E,
                    mask=diagonal_mask if masked else None)

    _flash_loop(i, scores, process)
    acc = acc_ref[...]
    o = (acc[:LANES] / acc[LANES:LANES + 1]).T
    lam = (jnp.exp(jnp.sum(lq1_ref[...] * lk1_ref[...], -1, keepdims=True))
           - jnp.exp(jnp.sum(lq2_ref[...] * lk2_ref[...], -1, keepdims=True)) + lam_init)
    od = o[:tq] - lam * o[tq:]
    y = od * lax.rsqrt(jnp.mean(od * od, -1, keepdims=True) + 1e-5) * g_ref[...]
    o_ref[...] = (y * (1.0 - lam_init)).astype(o_ref.dtype)


def _diff_attention(qkv, lq1, lk1, lq2, lk2, subln_g, lam_init, tq):
    b, s, _ = qkv.shape
    rows = 2 * tq
    dv = 2 * DIFF_HEAD_DIM
    vt = _transpose_values(qkv[..., 2 * DIFF_WIDTH:].reshape(b, s, DIFF_HEADS, dv), tq)
    vec64 = pl.BlockSpec((1, DIFF_HEAD_DIM), lambda bb, h, i: (0, 0))
    return pl.pallas_call(
        functools.partial(_diff_kernel, tq=tq, lam_init=lam_init),
        grid=(b, DIFF_HEADS, s // tq),
        in_specs=[vec64, vec64, vec64, vec64,
                  pl.BlockSpec((1, LANES), lambda bb, h, i: (0, 0)),
                  pl.BlockSpec((None, tq, LANES), lambda bb, h, i: (bb, i, h)),
                  pl.BlockSpec((None, s, LANES), lambda bb, h, i: (bb, 0, DIFF_HEADS + h)),
                  pl.BlockSpec((None, None, s // tq, dv + ONES_ROWS, tq), lambda bb, h, i: (bb, h, 0, 0, 0))],
        out_specs=pl.BlockSpec((None, tq, LANES), lambda bb, h, i: (bb, i, h)),
        out_shape=jax.ShapeDtypeStruct((b, s, DIFF_WIDTH), BF16),
        scratch_shapes=[pltpu.VMEM((rows, LANES), BF16), pltpu.VMEM((tq, rows), F32), pltpu.VMEM((tq, rows), F32),
                        pltpu.VMEM((tq, rows), BF16), pltpu.VMEM((tq, rows), BF16),
                        pltpu.VMEM((8, rows), F32), pltpu.VMEM((8, rows), F32),
                        pltpu.VMEM((dv + ONES_ROWS, rows), F32)],
        compiler_params=_params(("parallel", "parallel", "arbitrary")),
        name="diff_attention",
    )(lq1.reshape(1, -1), lk1.reshape(1, -1), lq2.reshape(1, -1), lk2.reshape(1, -1),
      subln_g.reshape(1, -1), qkv, qkv, vt)


def _ssd_kernel(pf_ref, cw_ref, cb_ref, dtb_ref, alog_ref, drep_ref, ng_ref, tri_ref, o_ref,
                xe_ref, tail_ref, state_ref, y_ref):
    L = SSM_CHUNK
    halo = 8

    @pl.when(pl.program_id(1) == 0)
    def _():
        tail_ref[...] = jnp.zeros_like(tail_ref)
        state_ref[...] = jnp.zeros_like(state_ref)

    z = pf_ref[:, 0:SSM_INNER]
    xbc = pf_ref[:, SSM_INNER:SSM_INNER + SSM_CONV_CH]
    dt_raw = pf_ref[:, SSM_INNER + SSM_CONV_CH:SSM_INNER + SSM_CONV_CH + LANES]
    xe_ref[0:halo, :] = tail_ref[...]
    xe_ref[halo:halo + L, :] = xbc
    tail_ref[...] = xbc[L - halo:L, :]
    conv = cb_ref[...] + cw_ref[SSM_CONV - 1:SSM_CONV, :] * xbc
    for k in range(SSM_CONV - 1):
        conv = conv + cw_ref[k:k + 1, :] * xe_ref[pl.ds(halo - (SSM_CONV - 1) + k, L), :]
    xbc = conv * jax.nn.sigmoid(conv)
    xs = xbc[:, :SSM_INNER]
    dt = jax.nn.softplus(dt_raw + dtb_ref[...])
    adt = dt * (-jnp.exp(alog_ref[...]))
    acum = jnp.dot(tri_ref[...], adt, preferred_element_type=F32, precision=lax.Precision.HIGHEST)
    acum_t = acum.T
    left = lax.broadcasted_iota(jnp.int32, (1, LANES), 1) < SSM_HEAD_DIM
    tril = lax.broadcasted_iota(jnp.int32, (L, L), 0) >= lax.broadcasted_iota(jnp.int32, (L, L), 1)
    pairs_per_group = SSM_HEADS // SSM_GROUPS // 2
    for g in range(SSM_GROUPS):
        bm = xbc[:, SSM_INNER + g * SSM_STATE:SSM_INNER + (g + 1) * SSM_STATE]
        cm = xbc[:, SSM_INNER + (SSM_GROUPS + g) * SSM_STATE:SSM_INNER + (SSM_GROUPS + g + 1) * SSM_STATE]
        cb = _dot_nt(cm.astype(BF16), bm.astype(BF16))
        bm_t = bm.T
        for qq in range(pairs_per_group):
            q = g * pairs_per_group + qq
            ls = slice(q * LANES, (q + 1) * LANES)
            xs_p = xs[:, ls]
            h0, h1 = 2 * q, 2 * q + 1
            xd = xs_p * jnp.where(left, dt[:, h0:h0 + 1], dt[:, h1:h1 + 1])
            prev = state_ref[q]
            y = drep_ref[:, ls] * xs_p
            new = jnp.zeros((SSM_STATE, LANES), F32)
            for h, keep in ((h0, left), (h1, jnp.logical_not(left))):
                xd_h = jnp.where(keep, xd, 0.0).astype(BF16)
                prev_h = jnp.where(keep, prev, 0.0).astype(BF16)
                cs_col = acum[:, h:h + 1]
                cs_row = acum_t[h:h + 1, :]
                w = cb * jnp.exp(jnp.where(tril, cs_col - cs_row, -jnp.inf))
                y = y + _dot(w.astype(BF16), xd_h) + _dot((cm * jnp.exp(cs_col)).astype(BF16), prev_h)
                new = new + _dot((bm_t * jnp.exp(acum[L - 1:L, h:h + 1] - cs_row)).astype(BF16), xd_h)
            decay = jnp.where(left, jnp.exp(acum[L - 1:L, h0:h0 + 1]), jnp.exp(acum[L - 1:L, h1:h1 + 1]))
            state_ref[q] = prev * decay + new
            y_ref[:, ls] = y
    y = y_ref[...] * (z * jax.nn.sigmoid(z))
    gw = SSM_INNER // SSM_GROUPS
    for g in range(SSM_GROUPS):
        yg = y[:, g * gw:(g + 1) * gw]
        yg = yg * lax.rsqrt(jnp.mean(yg * yg, -1, keepdims=True) + 1e-5) * ng_ref[:, g * gw:(g + 1) * gw]
        o_ref[:, g * gw:(g + 1) * gw] = yg.astype(o_ref.dtype)


def _mamba2_mixer(pf, conv_w, conv_b, dt_bias, a_log, d_skip, norm_g):
    b, s, w = pf.shape
    L = SSM_CHUNK
    pad_heads = lambda v: jnp.pad(v, (0, LANES - SSM_HEADS)).reshape(1, LANES)
    tri = jnp.tril(jnp.ones((L, L), F32))
    full = lambda shape: pl.BlockSpec(shape, lambda bb, c: (0,) * len(shape))
    return pl.pallas_call(
        _ssd_kernel,
        grid=(b, s // L),
        in_specs=[pl.BlockSpec((None, L, w), lambda bb, c: (bb, c, 0)),
                  full((SSM_CONV, SSM_CONV_CH)), full((1, SSM_CONV_CH)), full((1, LANES)), full((1, LANES)),
                  full((1, SSM_INNER)), full((1, SSM_INNER)), full((L, L))],
        out_specs=pl.BlockSpec((None, L, SSM_INNER), lambda bb, c: (bb, c, 0)),
        out_shape=jax.ShapeDtypeStruct((b, s, SSM_INNER), BF16),
        scratch_shapes=[pltpu.VMEM((L + 8, SSM_CONV_CH), F32), pltpu.VMEM((8, SSM_CONV_CH), F32),
                        pltpu.VMEM((SSM_HEADS // 2, SSM_STATE, LANES), F32), pltpu.VMEM((L, SSM_INNER), F32)],
        compiler_params=_params(("parallel", "arbitrary")),
        name="ssd_mixer",
    )(pf, conv_w, conv_b.reshape(1, -1), pad_heads(dt_bias), pad_heads(a_log),
      jnp.repeat(d_skip, SSM_HEAD_DIM).reshape(1, -1), norm_g.reshape(1, -1), tri)


def _cmp_kernel(x_ref, pek_ref, pev_ref, w1k_ref, w1v_ref, w2k_ref, w2v_ref, ko_ref, vo_ref,
                top_ref, bot_ref, *, nb):
    l = pl.program_id(1)
    g4 = NSA_KV_GROUPS

    @pl.when(l == 0)
    def _():
        top_ref[...] = jnp.zeros_like(top_ref)
        bot_ref[...] = jnp.zeros_like(bot_ref)

    x = x_ref[...]
    for kv, (pe_ref, w1_ref) in enumerate(((pek_ref, w1k_ref), (pev_ref, w1v_ref))):
        x4 = jnp.concatenate([x[:, kv * NSA_KV_WIDTH + g * LANES: kv * NSA_KV_WIDTH + (g + 1) * LANES]
                              for g in range(g4)], axis=0)
        top_ref[kv] += _dot((x4 + pe_ref[pl.ds(l, 1), :]).astype(BF16), w1_ref[l])
        bot_ref[kv] += _dot((x4 + pe_ref[pl.ds(l + NSA_CMP_STRIDE, 1), :]).astype(BF16),
                            w1_ref[l + NSA_CMP_STRIDE])

    @pl.when(l == NSA_CMP_STRIDE - 1)
    def _():
        last = lax.broadcasted_iota(jnp.int32, (nb, LANES), 0) == nb - 1
        for kv, (w2_ref, o_ref) in enumerate(((w2k_ref, ko_ref), (w2v_ref, vo_ref))):
            for g in range(g4):
                top = top_ref[kv, g * nb:(g + 1) * nb, :]
                bot = bot_ref[kv, g * nb:(g + 1) * nb, :]
                hid = top + pltpu.roll(bot, nb - 1, 0)
                out = _dot(jax.nn.gelu(hid, approximate=True).astype(BF16), w2_ref[...])
                o_ref[g] = jnp.where(last, 0.0, out).astype(o_ref.dtype)


def _compress(bf, pe_k, pe_v, w1k, w1v, w2k, w2v):
    b, s, w = bf.shape
    nb = s // NSA_CMP_STRIDE
    xv = bf.reshape(b, nb, NSA_CMP_STRIDE * w)
    full = lambda shape: pl.BlockSpec(shape, lambda bb, l: (0,) * len(shape))
    out_spec = pl.BlockSpec((None, NSA_KV_GROUPS, nb, LANES), lambda bb, l: (bb, 0, 0, 0))
    out_sd = jax.ShapeDtypeStruct((b, NSA_KV_GROUPS, nb, LANES), BF16)
    w1 = lambda a: a.astype(BF16).reshape(NSA_CMP_LEN, NSA_HEAD_DIM, -1)
    return pl.pallas_call(
        functools.partial(_cmp_kernel, nb=nb),
        grid=(b, NSA_CMP_STRIDE),
        in_specs=[pl.BlockSpec((None, nb, w), lambda bb, l: (bb, 0, l)),
                  full((NSA_CMP_LEN, LANES)), full((NSA_CMP_LEN, LANES)),
                  full((NSA_CMP_LEN, NSA_HEAD_DIM, LANES)), full((NSA_CMP_LEN, NSA_HEAD_DIM, LANES)),
                  full((LANES, LANES)), full((LANES, LANES))],
        out_specs=[out_spec, out_spec],
        out_shape=[out_sd, out_sd],
        scratch_shapes=[pltpu.VMEM((2, NSA_KV_GROUPS * nb, LANES), F32),
                        pltpu.VMEM((2, NSA_KV_GROUPS * nb, LANES), F32)],
        compiler_params=_params(("parallel", "arbitrary")),
        name="nsa_compress",
    )(xv, pe_k, pe_v, w1(w1k), w1(w1v), w2k.astype(BF16), w2v.astype(BF16))


def _nsa_kernel(q_ref, gate_ref, kc_ref, vct_ref, ks_ref, vst_ref, kw_ref, vwt_ref, blk_ref, ovt_ref, o_ref,
                qa_ref, sc_ref, pc_ref, ps_ref, st0_ref, st1_ref, pt0_ref, pt1_ref, sw_ref, pw_ref,
                m_ref, al_ref, acc_ref, mw_ref, alw_ref, accw_ref, *, tk, seq, slab):
    i = pl.program_id(2)
    tq = LANES
    t0 = i * tq
    rep = NSA_REP
    ncp = seq // NSA_CMP_STRIDE
    scale = NSA_HEAD_DIM ** -0.5
    c_exp = scale * LOG2E
    q = q_ref[...]
    for r in range(rep):
        qa_ref[r * tq:(r + 1) * tq, 0:LANES] = q[:, r * LANES:(r + 1) * LANES]
    qs = qa_ref[:, 0:LANES]
    t_lane = t0 + lax.broadcasted_iota(jnp.int32, (1, LANES), 1)

    sc_ref[...] = _dot_nt(kc_ref[...], qs)
    n_end = lax.broadcasted_iota(jnp.int32, (ncp, LANES), 0) * NSA_CMP_STRIDE + (NSA_CMP_LEN - 1)
    cmask = n_end <= t_lane
    for r in range(rep):
        cs = slice(r * tq, (r + 1) * tq)
        s = jnp.where(cmask, sc_ref[:, cs] * scale, NEG)
        e = jnp.where(cmask, jnp.exp(s - jnp.max(s, axis=0, keepdims=True)), 0.0)
        den = jnp.sum(e, axis=0, keepdims=True)
        p = e / jnp.where(den > 0, den, 1.0)
        pc_ref[:, cs] = p.astype(BF16)
        if r == 0:
            ps_ref[...] = p
        else:
            ps_ref[...] += p
    o_cmp_t = _dot(vct_ref[...], pc_ref[...])

    ps = ps_ref[...]
    p_hi = ps.astype(BF16)
    p_lo = (ps - p_hi.astype(F32)).astype(BF16)
    imp_t = _dot(ovt_ref[...], p_hi) + _dot(ovt_ref[...], p_lo)
    cur = t_lane >> 6
    mi = lax.broadcasted_iota(jnp.int32, (LANES, LANES), 0)
    forced = (mi == 0) | (mi == cur) | (mi == cur - 1)
    vals = jnp.where(mi <= cur, jnp.where(forced, BIG, imp_t), NEG)
    pen = jnp.full((LANES, LANES), -1e9, F32)
    mf = mi.astype(F32)
    for _ in range(NSA_SEL_COUNT):
        mx = jnp.max(vals, axis=0, keepdims=True)
        first = jnp.min(jnp.where(vals == mx, mf, float(LANES)), axis=0, keepdims=True)
        hit = mf == first
        pen = jnp.where(hit, 0.0, pen)
        vals = jnp.where(hit, -jnp.inf, vals)
    pen_q = pen.T.astype(BF16)
    for r in range(rep):
        qa_ref[r * tq:(r + 1) * tq, LANES:2 * LANES] = pen_q

    m_ref[...] = jnp.full_like(m_ref, NEG)
    acc_ref[...] = jnp.zeros_like(acc_ref)
    st, pt = (st0_ref, st1_ref), (pt0_ref, pt1_ref)

    def scores(j, slot):
        k0 = pl.multiple_of(j * tk, tk)
        ka = jnp.concatenate([ks_ref[pl.ds(k0, tk), :], blk_ref[pl.ds(k0, tk), :]], axis=1)
        st[slot][...] = _dot_nt(ka, qa_ref[...])

    def process(j, slot, masked):
        def causal(s, c):
            kpos = j * tk + lax.broadcasted_iota(jnp.int32, (tk, LANES), 0)
            return jnp.where(kpos <= t_lane, s, NEG)
        _softmax_pv(st[slot], pt[slot], vst_ref[j], m_ref, al_ref, acc_ref, c_exp=c_exp,
                    mask=causal if masked else None)

    _flash_loop((t0 + tq + tk - 1) // tk - 1, scores, process)
    acc = acc_ref[...]
    o_sel_t = acc[:LANES] / acc[LANES:LANES + 1]

    ws = pl.multiple_of(jnp.clip(t0 - NSA_WINDOW, 0, seq - slab), LANES)
    sw_ref[...] = _dot_nt(kw_ref[pl.ds(ws, slab), :], qs)
    mw_ref[...] = jnp.full_like(mw_ref, NEG)
    accw_ref[...] = jnp.zeros_like(accw_ref)
    jb = ws // LANES
    v_slab = jnp.concatenate([vwt_ref[jb + u] for u in range(slab // LANES)], axis=1)

    def window(s, c):
        kpos = ws + lax.broadcasted_iota(jnp.int32, (slab, LANES), 0)
        return jnp.where(kpos <= t_lane, jnp.where(kpos > t_lane - NSA_WINDOW, s, NEG), NEG)

    _softmax_pv(sw_ref, pw_ref, v_slab, mw_ref, alw_ref, accw_ref, c_exp=c_exp, mask=window)
    accw = accw_ref[...]
    o_win_t = accw[:LANES] / accw[LANES:LANES + 1]

    g_t = jax.nn.sigmoid(gate_ref[...]).T
    for r in range(rep):
        cs = slice(r * tq, (r + 1) * tq)
        out_t = (g_t[3 * r:3 * r + 1] * o_cmp_t[:, cs] + g_t[3 * r + 1:3 * r + 2] * o_sel_t[:, cs]
                 + g_t[3 * r + 2:3 * r + 3] * o_win_t[:, cs])
        o_ref[:, r * LANES:(r + 1) * LANES] = out_t.T.astype(o_ref.dtype)


def _nsa_attention(a_bf, b_f32, k_cmp, v_cmp, tk):
    b, s, _ = a_bf.shape
    g4 = NSA_KV_GROUPS
    tq = LANES
    rows = NSA_REP * tq
    ncp = s // NSA_CMP_STRIDE
    slab = NSA_WINDOW + tq
    dva = NSA_HEAD_DIM + ONES_ROWS
    assert s >= slab and s % tk == 0 and tk % tq == 0
    blk = (jnp.arange(s)[:, None] // NSA_SEL_LEN == jnp.arange(LANES)[None, :]).astype(BF16)
    cs = np.arange(ncp)[:, None] * NSA_CMP_STRIDE
    ss = np.arange(LANES)[None, :] * NSA_SEL_LEN
    ov = np.maximum(np.minimum(cs + NSA_CMP_LEN, ss + NSA_SEL_LEN) - np.maximum(cs, ss), 0) / NSA_CMP_LEN
    ov[ncp - 1] = 0.0
    qoff = NSA_Q_WIDTH // LANES
    group_values = lambda n: a_bf[..., NSA_Q_WIDTH + n * NSA_KV_WIDTH:NSA_Q_WIDTH + (n + 1) * NSA_KV_WIDTH].reshape(
        b, s, g4, NSA_HEAD_DIM)
    vst = _transpose_values(group_values(1), tk)
    vwt = _transpose_values(group_values(3), LANES)
    vct = jnp.swapaxes(v_cmp, 2, 3)
    k_spec = lambda n: pl.BlockSpec((None, s, LANES), lambda bb, g, i: (bb, 0, qoff + n * g4 + g))
    vt_spec = lambda t: pl.BlockSpec((None, None, s // t, dva, t), lambda bb, g, i: (bb, g, 0, 0, 0))
    return pl.pallas_call(
        functools.partial(_nsa_kernel, tk=tk, seq=s, slab=slab),
        grid=(b, g4, s // tq),
        in_specs=[pl.BlockSpec((None, tq, NSA_REP * LANES), lambda bb, g, i: (bb, i, g)),
                  pl.BlockSpec((None, tq, LANES), lambda bb, g, i: (bb, i, 2 * g4 + g)),
                  pl.BlockSpec((None, None, ncp, LANES), lambda bb, g, i: (bb, g, 0, 0)),
                  pl.BlockSpec((None, None, LANES, ncp), lambda bb, g, i: (bb, g, 0, 0)),
                  k_spec(0), vt_spec(tk), k_spec(2), vt_spec(LANES),
                  pl.BlockSpec((s, LANES), lambda bb, g, i: (0, 0)),
                  pl.BlockSpec((LANES, ncp), lambda bb, g, i: (0, 0))],
        out_specs=pl.BlockSpec((None, tq, NSA_REP * LANES), lambda bb, g, i: (bb, i, g)),
        out_shape=jax.ShapeDtypeStruct((b, s, NSA_Q_WIDTH), BF16),
        scratch_shapes=[pltpu.VMEM((rows, 2 * LANES), BF16),
                        pltpu.VMEM((ncp, rows), F32), pltpu.VMEM((ncp, rows), BF16), pltpu.VMEM((ncp, LANES), F32),
                        pltpu.VMEM((tk, rows), F32), pltpu.VMEM((tk, rows), F32),
                        pltpu.VMEM((tk, rows), BF16), pltpu.VMEM((tk, rows), BF16),
                        pltpu.VMEM((slab, rows), F32), pltpu.VMEM((slab, rows), BF16),
                        pltpu.VMEM((8, rows), F32), pltpu.VMEM((8, rows), F32), pltpu.VMEM((dva, rows), F32),
                        pltpu.VMEM((8, rows), F32), pltpu.VMEM((8, rows), F32), pltpu.VMEM((dva, rows), F32)],
        compiler_params=_params(("parallel", "parallel", "arbitrary")),
        name="nsa_attention",
    )(a_bf, b_f32, k_cmp, vct, a_bf, vst, a_bf, vwt, blk, jnp.asarray(ov.T, BF16))


def _moe_kernel(be_ref, nu_ref, x_ref, wg_ref, wu_ref, wd_ref, o_ref):
    i = pl.program_id(0)

    @pl.when(i < nu_ref[0])
    def _():
        x = x_ref[...]
        hid = jax.nn.silu(_dot(x, wg_ref[...])) * _dot(x, wu_ref[...])
        o_ref[...] = _dot(hid.astype(BF16), wd_ref[...])

    @pl.when(i >= nu_ref[0])
    def _():
        o_ref[...] = jnp.zeros_like(o_ref)


def _moe_blocks(block_e, n_used, xg, wg, wu, wd, tm):
    rows, d = xg.shape
    ff = wg.shape[-1]
    grid_spec = pltpu.PrefetchScalarGridSpec(
        num_scalar_prefetch=2,
        grid=(rows // tm,),
        in_specs=[pl.BlockSpec((tm, d), lambda i, be, nu: (i, 0)),
                  pl.BlockSpec((None, d, ff), lambda i, be, nu: (be[i], 0, 0)),
                  pl.BlockSpec((None, d, ff), lambda i, be, nu: (be[i], 0, 0)),
                  pl.BlockSpec((None, ff, d), lambda i, be, nu: (be[i], 0, 0))],
        out_specs=pl.BlockSpec((tm, d), lambda i, be, nu: (i, 0)),
    )
    return pl.pallas_call(
        _moe_kernel,
        grid_spec=grid_spec,
        out_shape=jax.ShapeDtypeStruct((rows, d), F32),
        compiler_params=_params(("arbitrary",)),
        name="moe_experts",
    )(block_e, n_used, xg, wg, wu, wd)


def _router_kernel(h_ref, wt_ref, b_ref, tri_ref, e_ref, g_ref, r_ref, cnt_ref, sig_ref, bia_ref, carry_ref):
    @pl.when(pl.program_id(0) == 0)
    def _():
        carry_ref[...] = jnp.zeros_like(carry_ref)

    tm = h_ref.shape[0]
    logits = lax.dot_general(wt_ref[...], h_ref[...], (((1,), (1,)), ((), ())),
                             preferred_element_type=F32, precision=lax.Precision.HIGHEST)
    sig = jax.nn.sigmoid(logits)
    sig_ref[...] = sig
    biased = sig + b_ref[...]
    n_chunks = tm // LANES
    for c in range(n_chunks):
        bia_ref[c] = biased[:, c * LANES:(c + 1) * LANES]
    cand = [jnp.concatenate([bia_ref[c, pl.ds(k, MOE_GROUPS, stride=MOE_PER_GROUP), :] for c in range(n_chunks)],
                            axis=1) for k in range(MOE_PER_GROUP)]

    def top_of(vals):
        best = functools.reduce(jnp.maximum, vals)
        idx = jnp.full(best.shape, MOE_PER_GROUP - 1, jnp.int32)
        for k in range(MOE_PER_GROUP - 2, -1, -1):
            idx = jnp.where(vals[k] == best, k, idx)
        return best, idx

    top1, idx1 = top_of(cand)
    top2, idx2 = top_of([jnp.where(idx1 == k, -jnp.inf, cand[k]) for k in range(MOE_PER_GROUP)])
    gsum = top1 + top2
    gi = lax.broadcasted_iota(jnp.int32, gsum.shape, 0)
    grp = jnp.min(jnp.where(gsum == jnp.max(gsum, axis=0, keepdims=True), gi, MOE_GROUPS), axis=0, keepdims=True)
    chosen = gi == grp
    e1 = grp * MOE_PER_GROUP + jnp.sum(jnp.where(chosen, idx1, 0), axis=0, keepdims=True)
    e2 = grp * MOE_PER_GROUP + jnp.sum(jnp.where(chosen, idx2, 0), axis=0, keepdims=True)
    ei = lax.broadcasted_iota(jnp.int32, (MOE_EXPERTS, tm), 0)
    oh1, oh2 = ei == e1, ei == e2
    sig = sig_ref[...]
    w1 = jnp.sum(jnp.where(oh1, sig, 0.0), axis=0, keepdims=True)
    w2 = jnp.sum(jnp.where(oh2, sig, 0.0), axis=0, keepdims=True)
    e_ref[...] = jnp.concatenate([e1, e2], axis=0)
    g_ref[...] = jnp.concatenate([w1, w2], axis=0) / (w1 + w2)
    both = jnp.where(oh1, 1.0, jnp.where(oh2, 1.0, 0.0))
    base = carry_ref[:, 0:1] + _dot(both.astype(BF16), tri_ref[...])
    r1 = jnp.sum(jnp.where(oh1, base, 0.0), axis=0, keepdims=True)
    r2 = jnp.sum(jnp.where(oh2, base, 0.0), axis=0, keepdims=True)
    r_ref[...] = jnp.concatenate([r1, r2], axis=0).astype(jnp.int32)
    carry_ref[...] += jnp.sum(both, axis=1, keepdims=True)
    cnt_ref[...] = carry_ref[...]


def _route(h, router_w, router_b, tm):
    t, d = h.shape
    e = router_w.shape[1]
    tri = (jnp.arange(tm)[:, None] < jnp.arange(tm)[None, :]).astype(BF16)
    tok = lambda dt: (pl.BlockSpec((MOE_TOPK, tm), lambda i: (0, i)), jax.ShapeDtypeStruct((MOE_TOPK, t), dt))
    (es, esd), (gs, gsd), (rs, rsd) = tok(jnp.int32), tok(F32), tok(jnp.int32)
    return pl.pallas_call(
        _router_kernel,
        grid=(t // tm,),
        in_specs=[pl.BlockSpec((tm, d), lambda i: (i, 0)), pl.BlockSpec((e, d), lambda i: (0, 0)),
                  pl.BlockSpec((e, 1), lambda i: (0, 0)), pl.BlockSpec((tm, tm), lambda i: (0, 0))],
        out_specs=[es, gs, rs, pl.BlockSpec((e, LANES), lambda i: (0, 0))],
        out_shape=[esd, gsd, rsd, jax.ShapeDtypeStruct((e, LANES), F32)],
        scratch_shapes=[pltpu.VMEM((e, tm), F32), pltpu.VMEM((tm // LANES, e, LANES), F32),
                        pltpu.VMEM((e, LANES), F32)],
        compiler_params=_params(("arbitrary",)),
        name="router",
    )(h, router_w.T, router_b.reshape(e, 1), tri)


def _moe_ffn(h, h_bf, router_w, router_b, wg, wu, wd, tm):
    t, d = h.shape
    expert, gate, rank, counts = _route(h, router_w, router_b, min(512, t))
    n_blocks = t * MOE_TOPK // tm + MOE_EXPERTS
    rows = n_blocks * tm
    counts = counts[:, 0].astype(jnp.int32)
    padded = (counts + tm - 1) // tm * tm
    pad_end = jnp.cumsum(padded)
    onehot = expert[..., None] == jnp.arange(MOE_EXPERTS, dtype=jnp.int32)
    pos = rank + jnp.sum(jnp.where(onehot, pad_end - padded, 0), axis=-1)
    tok = jnp.broadcast_to(jnp.arange(t, dtype=jnp.int32), (MOE_TOPK, t))
    buf_tok = jnp.full((rows,), t, jnp.int32).at[pos.reshape(-1)].set(tok.reshape(-1), unique_indices=True)
    block_e = jnp.minimum(jnp.searchsorted(pad_end, jnp.arange(n_blocks, dtype=jnp.int32) * tm, side='right'),
                          MOE_EXPERTS - 1).astype(jnp.int32)
    n_used = (pad_end[-1:] // tm).astype(jnp.int32)
    xg = jnp.concatenate([h_bf, jnp.zeros((1, d), BF16)], axis=0)[buf_tok]
    yb = _moe_blocks(block_e, n_used, xg, wg, wu, wd, tm)
    return yb[pos], gate.T


def _pad_cols(w, n):
    return jnp.pad(w, ((0, 0), (0, n - w.shape[1])))


def kernel(x, router_w, router_b, even_w_in, even_w_out, diff_lam_q1, diff_lam_k1, diff_lam_q2, diff_lam_k2,
           diff_subln_g, ssm_conv_w, ssm_conv_b, ssm_dt_bias, ssm_a_log, ssm_d, ssm_norm_g, odd_w_in,
           odd_w_out, nsa_pe_k, nsa_pe_v, nsa_ck_w1, nsa_ck_w2, nsa_cv_w1, nsa_cv_w2, ln_mix_g, ln_mix_b,
           ln_ffn_g, ln_ffn_b, moe_w_gate, moe_w_up, moe_w_down):
    b, s, d = x.shape
    t = b * s
    xf = x.reshape(t, d)
    xb = xf.astype(BF16)
    for layer in range(DEPTH):
        i = layer // 2
        if layer % 2 == 0:
            lam_init = 0.8 - 0.6 * math.exp(-0.3 * layer)
            w_in = even_w_in[i]
            n_attn = 3 * DIFF_WIDTH
            qkv = _matmul(xb, w_in[:, :n_attn].astype(BF16), BF16, 1024, 1024).reshape(b, s, n_attn)
            n_ssm = -(-(w_in.shape[1] - n_attn) // LANES) * LANES
            pf = _matmul(xb, _pad_cols(w_in[:, n_attn:], n_ssm).astype(BF16), F32, 1024, n_ssm // 3)
            pf = pf.reshape(b, s, n_ssm)
            y_attn = _diff_attention(qkv, diff_lam_q1[i], diff_lam_k1[i], diff_lam_q2[i], diff_lam_k2[i],
                                     diff_subln_g[i], lam_init, 256)
            y_ssm = _mamba2_mixer(pf, ssm_conv_w[i], ssm_conv_b[i], ssm_dt_bias[i], ssm_a_log[i], ssm_d[i],
                                  ssm_norm_g[i])
            mix_in = jnp.concatenate([y_attn, y_ssm], axis=-1).reshape(t, -1)
            w_out = even_w_out[i]
        else:
            w_in = odd_w_in[i]
            n_a = NSA_Q_WIDTH + 4 * NSA_KV_WIDTH
            wq = w_in[:, :NSA_Q_WIDTH]
            wkv = w_in[:, NSA_Q_WIDTH:NSA_Q_WIDTH + 6 * NSA_KV_WIDTH]
            wgate = w_in[:, NSA_Q_WIDTH + 6 * NSA_KV_WIDTH:]
            w_a = jnp.concatenate([wq, wkv[:, 2 * NSA_KV_WIDTH:]], axis=1)
            wgate = jnp.pad(wgate.reshape(d, NSA_KV_GROUPS, 3 * NSA_REP),
                            ((0, 0), (0, 0), (0, LANES - 3 * NSA_REP))).reshape(d, NSA_KV_GROUPS * LANES)
            w_b = jnp.concatenate([wkv[:, :2 * NSA_KV_WIDTH], wgate], axis=1)
            a_bf = _matmul(xb, w_a.astype(BF16), BF16, 1024, 1024).reshape(b, s, n_a)
            b_f32 = _matmul(xb, w_b.astype(BF16), F32, 1024, 512).reshape(b, s, -1)
            k_cmp, v_cmp = _compress(b_f32, nsa_pe_k[i], nsa_pe_v[i], nsa_ck_w1[i], nsa_cv_w1[i],
                                     nsa_ck_w2[i], nsa_cv_w2[i])
            mix_in = _nsa_attention(a_bf, b_f32, k_cmp, v_cmp, 256).reshape(t, -1)
            w_out = odd_w_out[i]
        h, h_bf = _matmul_res_ln(mix_in, w_out.astype(BF16), xf, ln_mix_g[layer], ln_mix_b[layer], 256)
        y2, gates = _moe_ffn(h, h_bf, router_w, router_b, moe_w_gate[layer].astype(BF16),
                             moe_w_up[layer].astype(BF16), moe_w_down[layer].astype(BF16), 256)
        xf, xb = _combine_ln(h, y2, gates, ln_ffn_g[layer], ln_ffn_b[layer], 256)
    return xf.reshape(b, s, d)
```

```python
import functools
import math

import jax
import jax.numpy as jnp
import numpy as np
from jax import lax
from jax.experimental import pallas as pl
from jax.experimental.pallas import tpu as pltpu

F32 = jnp.float32
BF16 = jnp.bfloat16

D_MODEL = 2048
DEPTH = 2
DEEPNORM_ALPHA = (2.0 * DEPTH) ** 0.25
LN_EPS = 1e-5

DIFF_HEADS = 8
DIFF_HEAD_DIM = 64
DIFF_WIDTH = DIFF_HEADS * 2 * DIFF_HEAD_DIM

SSM_INNER = D_MODEL // 2
SSM_HEAD_DIM = 64
SSM_HEADS = SSM_INNER // SSM_HEAD_DIM
SSM_GROUPS = 2
SSM_STATE = 128
SSM_CONV = 4
SSM_CHUNK = 128
SSM_CONV_CH = SSM_INNER + 2 * SSM_GROUPS * SSM_STATE

NSA_HEADS = 16
NSA_KV_GROUPS = 4
NSA_REP = NSA_HEADS // NSA_KV_GROUPS
NSA_HEAD_DIM = 128
NSA_CMP_LEN = 32
NSA_CMP_STRIDE = 16
NSA_SEL_LEN = 64
NSA_SEL_COUNT = 16
NSA_WINDOW = 512
NSA_Q_WIDTH = NSA_HEADS * NSA_HEAD_DIM
NSA_KV_WIDTH = NSA_KV_GROUPS * NSA_HEAD_DIM

MOE_GROUPS = 8
MOE_PER_GROUP = 4
MOE_EXPERTS = MOE_GROUPS * MOE_PER_GROUP
MOE_TOPK = 2
MOE_FF = 1024

NEG = -1e30
BIG = 1e30
LANES = 128
ONES_ROWS = 16
LOG2E = math.log2(math.e)
VMEM_LIMIT = 56 * 1024 * 1024


def _params(sem):
    return pltpu.CompilerParams(dimension_semantics=sem, vmem_limit_bytes=VMEM_LIMIT)


def _dot(a, b):
    return jnp.dot(a, b, preferred_element_type=F32)


def _dot_nt(a, b):
    return lax.dot_general(a, b, (((1,), (1,)), ((), ())), preferred_element_type=F32)


def _mm_kernel(x_ref, w_ref, o_ref):
    o_ref[...] = _dot(x_ref[...], w_ref[...]).astype(o_ref.dtype)


def _matmul(x, w, out_dtype, tm, tn):
    m, k = x.shape
    n = w.shape[1]
    return pl.pallas_call(
        _mm_kernel,
        grid=(n // tn, m // tm),
        in_specs=[pl.BlockSpec((tm, k), lambda j, i: (i, 0)),
                  pl.BlockSpec((k, tn), lambda j, i: (0, j))],
        out_specs=pl.BlockSpec((tm, tn), lambda j, i: (i, j)),
        out_shape=jax.ShapeDtypeStruct((m, n), out_dtype),
        compiler_params=_params(("parallel", "parallel")),
        name="proj_matmul",
    )(x, w)


def _layer_norm_rows(y, g, b):
    mu = jnp.mean(y, -1, keepdims=True)
    yc = y - mu
    var = jnp.mean(yc * yc, -1, keepdims=True)
    return yc * lax.rsqrt(var + LN_EPS) * g + b


def _mm_ln_kernel(a_ref, w_ref, res_ref, g_ref, b_ref, o_ref, ob_ref):
    y = _dot(a_ref[...], w_ref[...]) + DEEPNORM_ALPHA * res_ref[...]
    h = _layer_norm_rows(y, g_ref[...], b_ref[...])
    o_ref[...] = h
    ob_ref[...] = h.astype(BF16)


def _matmul_res_ln(a, w, res, g, b, tm):
    m, k = a.shape
    n = w.shape[1]
    return pl.pallas_call(
        _mm_ln_kernel,
        grid=(m // tm,),
        in_specs=[pl.BlockSpec((tm, k), lambda i: (i, 0)),
                  pl.BlockSpec((k, n), lambda i: (0, 0)),
                  pl.BlockSpec((tm, n), lambda i: (i, 0)),
                  pl.BlockSpec((1, n), lambda i: (0, 0)),
                  pl.BlockSpec((1, n), lambda i: (0, 0))],
        out_specs=[pl.BlockSpec((tm, n), lambda i: (i, 0)),
                   pl.BlockSpec((tm, n), lambda i: (i, 0))],
        out_shape=[jax.ShapeDtypeStruct((m, n), F32), jax.ShapeDtypeStruct((m, n), BF16)],
        compiler_params=_params(("parallel",)),
        name="outproj_ln",
    )(a, w, res, g.reshape(1, n), b.reshape(1, n))


def _combine_ln_kernel(h_ref, y_ref, w_ref, g_ref, b_ref, o_ref, ob_ref):
    w = w_ref[...]
    y = DEEPNORM_ALPHA * h_ref[...] + w[:, 0:1] * y_ref[0] + w[:, 1:2] * y_ref[1]
    x = _layer_norm_rows(y, g_ref[...], b_ref[...])
    o_ref[...] = x
    ob_ref[...] = x.astype(BF16)


def _combine_ln(h, y2, w, g, b, tm):
    m, n = h.shape
    row = pl.BlockSpec((tm, n), lambda i: (i, 0))
    vec = pl.BlockSpec((1, n), lambda i: (0, 0))
    return pl.pallas_call(
        _combine_ln_kernel,
        grid=(m // tm,),
        in_specs=[row, pl.BlockSpec((MOE_TOPK, tm, n), lambda i: (0, i, 0)),
                  pl.BlockSpec((tm, MOE_TOPK), lambda i: (i, 0)), vec, vec],
        out_specs=[row, row],
        out_shape=[jax.ShapeDtypeStruct((m, n), F32), jax.ShapeDtypeStruct((m, n), BF16)],
        compiler_params=_params(("parallel",)),
        name="ffn_combine_ln",
    )(h, y2, w, g.reshape(1, n), b.reshape(1, n))


def _softmax_tile(st_ref, pt_ref, m_ref, al_ref, *, c_exp, mask=None):
    rows = st_ref.shape[1]
    for c in range(rows // LANES):
        cs = slice(c * LANES, (c + 1) * LANES)
        s = st_ref[:, cs]
        if mask is not None:
            s = mask(s, c)
        m_old = m_ref[:, cs]
        m_new = jnp.maximum(m_old, jnp.max(s, axis=0, keepdims=True))
        pt_ref[:, cs] = jnp.exp2((s - m_new[0:1]) * c_exp).astype(BF16)
        al_ref[:, cs] = jnp.exp2((m_old - m_new) * c_exp)
        m_ref[:, cs] = m_new


def _pv_update(vt_aug, pt_ref, al_ref, acc_ref):
    acc_ref[...] = acc_ref[...] * al_ref[0:1, :] + _dot(vt_aug, pt_ref[...])


def _flash_loop(n, scores, softmax, update):
    scores(0, 0)

    @pl.when(n == 1)
    def _():
        softmax(0, 0, True)
        update(0, 0)

    @pl.when(n >= 2)
    def _():
        scores(1, 1)
        softmax(0, 0, False)
        n_pairs = (n - 2) // 2

        def pair(u, carry):
            j = 2 * u
            scores(j + 2, 0)
            softmax(j + 1, 1, False)
            update(j, 0)
            scores(j + 3, 1)
            softmax(j + 2, 0, False)
            update(j + 1, 1)
            return carry

        lax.fori_loop(0, n_pairs, pair, 0)
        j = 2 * n_pairs

        @pl.when(n - 1 == j + 1)
        def _():
            softmax(j + 1, 1, True)
            update(j, 0)
            update(j + 1, 1)

        @pl.when(n - 1 == j + 2)
        def _():
            scores(j + 2, 0)
            softmax(j + 1, 1, False)
            update(j, 0)
            softmax(j + 2, 0, True)
            update(j + 1, 1)
            update(j + 2, 0)


def _transpose_values(v, tk):
    b, s, h, dv = v.shape
    vt = jnp.transpose(v.reshape(b, s // tk, tk, h, dv), (0, 3, 1, 4, 2))
    return jnp.concatenate([vt, jnp.ones((b, h, s // tk, ONES_ROWS, tk), v.dtype)], axis=3)


def _diff_kernel(lq1_ref, lk1_ref, lq2_ref, lk2_ref, g_ref, q_ref, k_ref, vt_ref, o_ref,
                 q2_ref, st0_ref, st1_ref, pt0_ref, pt1_ref, m_ref, al0_ref, al1_ref, acc_ref, *, tq, tk, lam_init):
    i = pl.program_id(2)
    st, pt, al = (st0_ref, st1_ref), (pt0_ref, pt1_ref), (al0_ref, al1_ref)
    q = q_ref[...]
    lane = lax.broadcasted_iota(jnp.int32, (tq, LANES), 1)
    zero = jnp.zeros_like(q)
    q2 = jnp.concatenate([jnp.where(lane < DIFF_HEAD_DIM, q, zero),
                          jnp.where(lane >= DIFF_HEAD_DIM, q, zero)], axis=0)
    q2_ref[...] = q2 * jnp.asarray(DIFF_HEAD_DIM ** -0.5, BF16)
    m_ref[...] = jnp.full_like(m_ref, NEG)
    acc_ref[...] = jnp.zeros_like(acc_ref)

    def scores(j, slot):
        kt = k_ref[pl.ds(pl.multiple_of(j * tk, tk), tk), :]
        st[slot][...] = _dot_nt(kt, q2_ref[...])

    def softmax(j, slot, masked):
        def causal(s, c):
            key = j * tk + lax.broadcasted_iota(jnp.int32, (tk, LANES), 0)
            qry = i * tq + ((c * LANES + lax.broadcasted_iota(jnp.int32, (tk, LANES), 1)) & (tq - 1))
            return jnp.where(key <= qry, s, NEG)
        _softmax_tile(st[slot], pt[slot], m_ref, al[slot], c_exp=LOG2E, mask=causal if masked else None)

    def update(j, slot):
        _pv_update(vt_ref[j], pt[slot], al[slot], acc_ref)

    _flash_loop((i * tq + tq + tk - 1) // tk, scores, softmax, update)
    acc = acc_ref[...]
    o = (acc[:LANES] / acc[LANES:LANES + 1]).T
    lam = (jnp.exp(jnp.sum(lq1_ref[...] * lk1_ref[...], -1, keepdims=True))
           - jnp.exp(jnp.sum(lq2_ref[...] * lk2_ref[...], -1, keepdims=True)) + lam_init)
    od = o[:tq] - lam * o[tq:]
    y = od * lax.rsqrt(jnp.mean(od * od, -1, keepdims=True) + 1e-5) * g_ref[...]
    o_ref[...] = (y * (1.0 - lam_init)).astype(o_ref.dtype)


def _diff_attention(qkv, lq1, lk1, lq2, lk2, subln_g, lam_init, tq, tk):
    b, s, _ = qkv.shape
    rows = 2 * tq
    dv = 2 * DIFF_HEAD_DIM
    assert tk % tq == 0 and s % tk == 0
    vt = _transpose_values(qkv[..., 2 * DIFF_WIDTH:].reshape(b, s, DIFF_HEADS, dv), tk)
    vec64 = pl.BlockSpec((1, DIFF_HEAD_DIM), lambda bb, h, i: (0, 0))
    return pl.pallas_call(
        functools.partial(_diff_kernel, tq=tq, tk=tk, lam_init=lam_init),
        grid=(b, DIFF_HEADS, s // tq),
        in_specs=[vec64, vec64, vec64, vec64,
                  pl.BlockSpec((1, LANES), lambda bb, h, i: (0, 0)),
                  pl.BlockSpec((None, tq, LANES), lambda bb, h, i: (bb, i, h)),
                  pl.BlockSpec((None, s, LANES), lambda bb, h, i: (bb, 0, DIFF_HEADS + h)),
                  pl.BlockSpec((None, None, s // tk, dv + ONES_ROWS, tk), lambda bb, h, i: (bb, h, 0, 0, 0))],
        out_specs=pl.BlockSpec((None, tq, LANES), lambda bb, h, i: (bb, i, h)),
        out_shape=jax.ShapeDtypeStruct((b, s, DIFF_WIDTH), BF16),
        scratch_shapes=[pltpu.VMEM((rows, LANES), BF16), pltpu.VMEM((tk, rows), F32), pltpu.VMEM((tk, rows), F32),
                        pltpu.VMEM((tk, rows), BF16), pltpu.VMEM((tk, rows), BF16),
                        pltpu.VMEM((8, rows), F32), pltpu.VMEM((8, rows), F32), pltpu.VMEM((8, rows), F32),
                        pltpu.VMEM((dv + ONES_ROWS, rows), F32)],
        compiler_params=_params(("parallel", "parallel", "arbitrary")),
        name="diff_attention",
    )(lq1.reshape(1, -1), lk1.reshape(1, -1), lq2.reshape(1, -1), lk2.reshape(1, -1),
      subln_g.reshape(1, -1), qkv, qkv, vt)


def _ssd_kernel(pf_ref, cw_ref, cb_ref, dtb_ref, alog_ref, drep_ref, ng_ref, tri_ref, o_ref,
                xe_ref, tail_ref, state_ref, y_ref):
    L = SSM_CHUNK
    halo = 8

    @pl.when(pl.program_id(1) == 0)
    def _():
        tail_ref[...] = jnp.zeros_like(tail_ref)
        state_ref[...] = jnp.zeros_like(state_ref)

    z = pf_ref[:, 0:SSM_INNER]
    xbc = pf_ref[:, SSM_INNER:SSM_INNER + SSM_CONV_CH]
    dt_raw = pf_ref[:, SSM_INNER + SSM_CONV_CH:SSM_INNER + SSM_CONV_CH + LANES]
    xe_ref[0:halo, :] = tail_ref[...]
    xe_ref[halo:halo + L, :] = xbc
    tail_ref[...] = xbc[L - halo:L, :]
    conv = cb_ref[...] + cw_ref[SSM_CONV - 1:SSM_CONV, :] * xbc
    for k in range(SSM_CONV - 1):
        conv = conv + cw_ref[k:k + 1, :] * xe_ref[pl.ds(halo - (SSM_CONV - 1) + k, L), :]
    xbc = conv * jax.nn.sigmoid(conv)
    xs = xbc[:, :SSM_INNER]
    dt = jax.nn.softplus(dt_raw + dtb_ref[...])
    adt = dt * (-jnp.exp(alog_ref[...]))
    acum = jnp.dot(tri_ref[...], adt, preferred_element_type=F32, precision=lax.Precision.HIGHEST)
    acum_t = acum.T
    left = lax.broadcasted_iota(jnp.int32, (1, LANES), 1) < SSM_HEAD_DIM
    tril = lax.broadcasted_iota(jnp.int32, (L, L), 0) >= lax.broadcasted_iota(jnp.int32, (L, L), 1)
    pairs_per_group = SSM_HEADS // SSM_GROUPS // 2
    for g in range(SSM_GROUPS):
        bm = xbc[:, SSM_INNER + g * SSM_STATE:SSM_INNER + (g + 1) * SSM_STATE]
        cm = xbc[:, SSM_INNER + (SSM_GROUPS + g) * SSM_STATE:SSM_INNER + (SSM_GROUPS + g + 1) * SSM_STATE]
        cb = _dot_nt(cm.astype(BF16), bm.astype(BF16))
        bm_t = bm.T
        for qq in range(pairs_per_group):
            q = g * pairs_per_group + qq
            ls = slice(q * LANES, (q + 1) * LANES)
            xs_p = xs[:, ls]
            h0, h1 = 2 * q, 2 * q + 1
            xd = xs_p * jnp.where(left, dt[:, h0:h0 + 1], dt[:, h1:h1 + 1])
            prev = state_ref[q]
            y = drep_ref[:, ls] * xs_p
            new = jnp.zeros((SSM_STATE, LANES), F32)
            for h, keep in ((h0, left), (h1, jnp.logical_not(left))):
                xd_h = jnp.where(keep, xd, 0.0).astype(BF16)
                prev_h = jnp.where(keep, prev, 0.0).astype(BF16)
                cs_col = acum[:, h:h + 1]
                cs_row = acum_t[h:h + 1, :]
                w = cb * jnp.exp(jnp.where(tril, cs_col - cs_row, -jnp.inf))
                y = y + _dot(w.astype(BF16), xd_h) + _dot((cm * jnp.exp(cs_col)).astype(BF16), prev_h)
                new = new + _dot((bm_t * jnp.exp(acum[L - 1:L, h:h + 1] - cs_row)).astype(BF16), xd_h)
            decay = jnp.where(left, jnp.exp(acum[L - 1:L, h0:h0 + 1]), jnp.exp(acum[L - 1:L, h1:h1 + 1]))
            state_ref[q] = prev * decay + new
            y_ref[:, ls] = y
    y = y_ref[...] * (z * jax.nn.sigmoid(z))
    gw = SSM_INNER // SSM_GROUPS
    for g in range(SSM_GROUPS):
        yg = y[:, g * gw:(g + 1) * gw]
        yg = yg * lax.rsqrt(jnp.mean(yg * yg, -1, keepdims=True) + 1e-5) * ng_ref[:, g * gw:(g + 1) * gw]
        o_ref[:, g * gw:(g + 1) * gw] = yg.astype(o_ref.dtype)


def _mamba2_mixer(pf, conv_w, conv_b, dt_bias, a_log, d_skip, norm_g):
    b, s, w = pf.shape
    L = SSM_CHUNK
    pad_heads = lambda v: jnp.pad(v, (0, LANES - SSM_HEADS)).reshape(1, LANES)
    tri = jnp.tril(jnp.ones((L, L), F32))
    full = lambda shape: pl.BlockSpec(shape, lambda bb, c: (0,) * len(shape))
    return pl.pallas_call(
        _ssd_kernel,
        grid=(b, s // L),
        in_specs=[pl.BlockSpec((None, L, w), lambda bb, c: (bb, c, 0)),
                  full((SSM_CONV, SSM_CONV_CH)), full((1, SSM_CONV_CH)), full((1, LANES)), full((1, LANES)),
                  full((1, SSM_INNER)), full((1, SSM_INNER)), full((L, L))],
        out_specs=pl.BlockSpec((None, L, SSM_INNER), lambda bb, c: (bb, c, 0)),
        out_shape=jax.ShapeDtypeStruct((b, s, SSM_INNER), BF16),
        scratch_shapes=[pltpu.VMEM((L + 8, SSM_CONV_CH), F32), pltpu.VMEM((8, SSM_CONV_CH), F32),
                        pltpu.VMEM((SSM_HEADS // 2, SSM_STATE, LANES), F32), pltpu.VMEM((L, SSM_INNER), F32)],
        compiler_params=_params(("parallel", "arbitrary")),
        name="ssd_mixer",
    )(pf, conv_w, conv_b.reshape(1, -1), pad_heads(dt_bias), pad_heads(a_log),
      jnp.repeat(d_skip, SSM_HEAD_DIM).reshape(1, -1), norm_g.reshape(1, -1), tri)


def _cmp_kernel(x_ref, pek_ref, pev_ref, w1k_ref, w1v_ref, w2k_ref, w2v_ref, ko_ref, vo_ref,
                top_ref, bot_ref, *, nb):
    l = pl.program_id(1)
    g4 = NSA_KV_GROUPS

    @pl.when(l == 0)
    def _():
        top_ref[...] = jnp.zeros_like(top_ref)
        bot_ref[...] = jnp.zeros_like(bot_ref)

    x = x_ref[...]
    for kv, (pe_ref, w1_ref) in enumerate(((pek_ref, w1k_ref), (pev_ref, w1v_ref))):
        x4 = jnp.concatenate([x[:, kv * NSA_KV_WIDTH + g * LANES: kv * NSA_KV_WIDTH + (g + 1) * LANES]
                              for g in range(g4)], axis=0)
        top_ref[kv] += _dot((x4 + pe_ref[pl.ds(l, 1), :]).astype(BF16), w1_ref[l])
        bot_ref[kv] += _dot((x4 + pe_ref[pl.ds(l + NSA_CMP_STRIDE, 1), :]).astype(BF16),
                            w1_ref[l + NSA_CMP_STRIDE])

    @pl.when(l == NSA_CMP_STRIDE - 1)
    def _():
        last = lax.broadcasted_iota(jnp.int32, (nb, LANES), 0) == nb - 1
        for kv, (w2_ref, o_ref) in enumerate(((w2k_ref, ko_ref), (w2v_ref, vo_ref))):
            for g in range(g4):
                top = top_ref[kv, g * nb:(g + 1) * nb, :]
                bot = bot_ref[kv, g * nb:(g + 1) * nb, :]
                hid = top + pltpu.roll(bot, nb - 1, 0)
                out = _dot(jax.nn.gelu(hid, approximate=True).astype(BF16), w2_ref[...])
                o_ref[g] = jnp.where(last, 0.0, out).astype(o_ref.dtype)


def _compress(bf, pe_k, pe_v, w1k, w1v, w2k, w2v):
    b, s, w = bf.shape
    nb = s // NSA_CMP_STRIDE
    xv = bf.reshape(b, nb, NSA_CMP_STRIDE * w)
    full = lambda shape: pl.BlockSpec(shape, lambda bb, l: (0,) * len(shape))
    out_spec = pl.BlockSpec((None, NSA_KV_GROUPS, nb, LANES), lambda bb, l: (bb, 0, 0, 0))
    out_sd = jax.ShapeDtypeStruct((b, NSA_KV_GROUPS, nb, LANES), BF16)
    w1 = lambda a: a.astype(BF16).reshape(NSA_CMP_LEN, NSA_HEAD_DIM, -1)
    return pl.pallas_call(
        functools.partial(_cmp_kernel, nb=nb),
        grid=(b, NSA_CMP_STRIDE),
        in_specs=[pl.BlockSpec((None, nb, w), lambda bb, l: (bb, 0, l)),
                  full((NSA_CMP_LEN, LANES)), full((NSA_CMP_LEN, LANES)),
                  full((NSA_CMP_LEN, NSA_HEAD_DIM, LANES)), full((NSA_CMP_LEN, NSA_HEAD_DIM, LANES)),
                  full((LANES, LANES)), full((LANES, LANES))],
        out_specs=[out_spec, out_spec],
        out_shape=[out_sd, out_sd],
        scratch_shapes=[pltpu.VMEM((2, NSA_KV_GROUPS * nb, LANES), F32),
                        pltpu.VMEM((2, NSA_KV_GROUPS * nb, LANES), F32)],
        compiler_params=_params(("parallel", "arbitrary")),
        name="nsa_compress",
    )(xv, pe_k, pe_v, w1(w1k), w1(w1v), w2k.astype(BF16), w2v.astype(BF16))


def _nsa_kernel(q_ref, gate_ref, kc_ref, vct_ref, ks_ref, vst_ref, kw_ref, vwt_ref, blk_ref, ovt_ref, o_ref,
                qa_ref, sc_ref, pc_ref, ps_ref, st0_ref, st1_ref, pt0_ref, pt1_ref, sw_ref, pw_ref,
                m_ref, al0_ref, al1_ref, acc_ref, mw_ref, alw_ref, accw_ref, *, tk, seq, slab):
    i = pl.program_id(2)
    tq = LANES
    t0 = i * tq
    rep = NSA_REP
    ncp = seq // NSA_CMP_STRIDE
    scale = NSA_HEAD_DIM ** -0.5
    c_exp = scale * LOG2E
    q = q_ref[...]
    for r in range(rep):
        qa_ref[r * tq:(r + 1) * tq, 0:LANES] = q[:, r * LANES:(r + 1) * LANES]
    qs = qa_ref[:, 0:LANES]
    t_lane = t0 + lax.broadcasted_iota(jnp.int32, (1, LANES), 1)

    sc_ref[...] = _dot_nt(kc_ref[...], qs)
    n_end = lax.broadcasted_iota(jnp.int32, (ncp, LANES), 0) * NSA_CMP_STRIDE + (NSA_CMP_LEN - 1)
    cmask = n_end <= t_lane
    for r in range(rep):
        cs = slice(r * tq, (r + 1) * tq)
        s = jnp.where(cmask, sc_ref[:, cs] * scale, NEG)
        e = jnp.where(cmask, jnp.exp(s - jnp.max(s, axis=0, keepdims=True)), 0.0)
        den = jnp.sum(e, axis=0, keepdims=True)
        p = e / jnp.where(den > 0, den, 1.0)
        pc_ref[:, cs] = p.astype(BF16)
        if r == 0:
            ps_ref[...] = p
        else:
            ps_ref[...] += p
    o_cmp_t = _dot(vct_ref[...], pc_ref[...])

    ps = ps_ref[...]
    p_hi = ps.astype(BF16)
    p_lo = (ps - p_hi.astype(F32)).astype(BF16)
    imp_t = _dot(ovt_ref[...], p_hi) + _dot(ovt_ref[...], p_lo)
    cur = t_lane >> 6
    mi = lax.broadcasted_iota(jnp.int32, (LANES, LANES), 0)
    forced = (mi == 0) | (mi == cur) | (mi == cur - 1)
    vals = jnp.where(forced, -jnp.inf, jnp.where(mi <= cur, imp_t, NEG))
    pen = jnp.where(forced, 0.0, -1e9)
    mf = mi.astype(F32)
    for _ in range(NSA_SEL_COUNT - 3):
        mx = jnp.max(vals, axis=0, keepdims=True)
        first = jnp.min(jnp.where(vals == mx, mf, float(LANES)), axis=0, keepdims=True)
        hit = mf == first
        pen = jnp.where(hit, 0.0, pen)
        vals = jnp.where(hit, -jnp.inf, vals)
    pen_q = pen.T.astype(BF16)
    for r in range(rep):
        qa_ref[r * tq:(r + 1) * tq, LANES:2 * LANES] = pen_q

    ws = pl.multiple_of(jnp.clip(t0 - NSA_WINDOW, 0, seq - slab), LANES)
    sw_ref[...] = _dot_nt(kw_ref[pl.ds(ws, slab), :], qs)
    mw_ref[...] = jnp.full_like(mw_ref, NEG)
    accw_ref[...] = jnp.zeros_like(accw_ref)
    jb = ws // LANES
    v_slab = jnp.concatenate([vwt_ref[jb + u] for u in range(slab // LANES)], axis=1)

    def window(s, c):
        kpos = ws + lax.broadcasted_iota(jnp.int32, (slab, LANES), 0)
        return jnp.where(kpos <= t_lane, jnp.where(kpos > t_lane - NSA_WINDOW, s, NEG), NEG)

    _softmax_tile(sw_ref, pw_ref, mw_ref, alw_ref, c_exp=c_exp, mask=window)
    _pv_update(v_slab, pw_ref, alw_ref, accw_ref)
    accw = accw_ref[...]
    o_win_t = accw[:LANES] / accw[LANES:LANES + 1]

    m_ref[...] = jnp.full_like(m_ref, NEG)
    acc_ref[...] = jnp.zeros_like(acc_ref)
    st, pt, al = (st0_ref, st1_ref), (pt0_ref, pt1_ref), (al0_ref, al1_ref)

    def scores(j, slot):
        k0 = pl.multiple_of(j * tk, tk)
        ka = jnp.concatenate([ks_ref[pl.ds(k0, tk), :], blk_ref[pl.ds(k0, tk), :]], axis=1)
        st[slot][...] = _dot_nt(ka, qa_ref[...])

    def softmax(j, slot, masked):
        def causal(s, c):
            kpos = j * tk + lax.broadcasted_iota(jnp.int32, (tk, LANES), 0)
            return jnp.where(kpos <= t_lane, s, NEG)
        _softmax_tile(st[slot], pt[slot], m_ref, al[slot], c_exp=c_exp, mask=causal if masked else None)

    def update(j, slot):
        _pv_update(vst_ref[j], pt[slot], al[slot], acc_ref)

    _flash_loop((t0 + tq + tk - 1) // tk, scores, softmax, update)
    acc = acc_ref[...]
    o_sel_t = acc[:LANES] / acc[LANES:LANES + 1]

    g_t = jax.nn.sigmoid(gate_ref[...]).T
    for r in range(rep):
        cs = slice(r * tq, (r + 1) * tq)
        out_t = (g_t[3 * r:3 * r + 1] * o_cmp_t[:, cs] + g_t[3 * r + 1:3 * r + 2] * o_sel_t[:, cs]
                 + g_t[3 * r + 2:3 * r + 3] * o_win_t[:, cs])
        o_ref[:, r * LANES:(r + 1) * LANES] = out_t.T.astype(o_ref.dtype)


def _nsa_attention(a_bf, b_f32, k_cmp, v_cmp, tk):
    b, s, _ = a_bf.shape
    g4 = NSA_KV_GROUPS
    tq = LANES
    rows = NSA_REP * tq
    ncp = s // NSA_CMP_STRIDE
    slab = NSA_WINDOW + tq
    dva = NSA_HEAD_DIM + ONES_ROWS
    assert s >= slab and s % tk == 0 and tk % tq == 0
    blk = (jnp.arange(s)[:, None] // NSA_SEL_LEN == jnp.arange(LANES)[None, :]).astype(BF16)
    cs = np.arange(ncp)[:, None] * NSA_CMP_STRIDE
    ss = np.arange(LANES)[None, :] * NSA_SEL_LEN
    ov = np.maximum(np.minimum(cs + NSA_CMP_LEN, ss + NSA_SEL_LEN) - np.maximum(cs, ss), 0) / NSA_CMP_LEN
    ov[ncp - 1] = 0.0
    qoff = NSA_Q_WIDTH // LANES
    group_values = lambda n: a_bf[..., NSA_Q_WIDTH + n * NSA_KV_WIDTH:NSA_Q_WIDTH + (n + 1) * NSA_KV_WIDTH].reshape(
        b, s, g4, NSA_HEAD_DIM)
    vst = _transpose_values(group_values(1), tk)
    vwt = _transpose_values(group_values(3), LANES)
    vct = jnp.swapaxes(v_cmp, 2, 3)
    k_spec = lambda n: pl.BlockSpec((None, s, LANES), lambda bb, g, i: (bb, 0, qoff + n * g4 + g))
    vt_spec = lambda t: pl.BlockSpec((None, None, s // t, dva, t), lambda bb, g, i: (bb, g, 0, 0, 0))
    return pl.pallas_call(
        functools.partial(_nsa_kernel, tk=tk, seq=s, slab=slab),
        grid=(b, g4, s // tq),
        in_specs=[pl.BlockSpec((None, tq, NSA_REP * LANES), lambda bb, g, i: (bb, i, g)),
                  pl.BlockSpec((None, tq, LANES), lambda bb, g, i: (bb, i, 2 * g4 + g)),
                  pl.BlockSpec((None, None, ncp, LANES), lambda bb, g, i: (bb, g, 0, 0)),
                  pl.BlockSpec((None, None, LANES, ncp), lambda bb, g, i: (bb, g, 0, 0)),
                  k_spec(0), vt_spec(tk), k_spec(2), vt_spec(LANES),
                  pl.BlockSpec((s, LANES), lambda bb, g, i: (0, 0)),
                  pl.BlockSpec((LANES, ncp), lambda bb, g, i: (0, 0))],
        out_specs=pl.BlockSpec((None, tq, NSA_REP * LANES), lambda bb, g, i: (bb, i, g)),
        out_shape=jax.ShapeDtypeStruct((b, s, NSA_Q_WIDTH), BF16),
        scratch_shapes=[pltpu.VMEM((rows, 2 * LANES), BF16),
                        pltpu.VMEM((ncp, rows), F32), pltpu.VMEM((ncp, rows), BF16), pltpu.VMEM((ncp, LANES), F32),
                        pltpu.VMEM((tk, rows), F32), pltpu.VMEM((tk, rows), F32),
                        pltpu.VMEM((tk, rows), BF16), pltpu.VMEM((tk, rows), BF16),
                        pltpu.VMEM((slab, rows), F32), pltpu.VMEM((slab, rows), BF16),
                        pltpu.VMEM((8, rows), F32), pltpu.VMEM((8, rows), F32), pltpu.VMEM((8, rows), F32),
                        pltpu.VMEM((dva, rows), F32),
                        pltpu.VMEM((8, rows), F32), pltpu.VMEM((8, rows), F32), pltpu.VMEM((dva, rows), F32)],
        compiler_params=_params(("parallel", "parallel", "arbitrary")),
        name="nsa_attention",
    )(a_bf, b_f32, k_cmp, vct, a_bf, vst, a_bf, vwt, blk, jnp.asarray(ov.T, BF16))


def _moe_kernel(be_ref, hs_ref, first_ref, nu_ref, x_ref, wg_ref, wu_ref, wd_ref, o_ref, wgb_ref, wub_ref, wdb_ref):
    i, j = pl.program_id(0), pl.program_id(1)

    @pl.when(i < nu_ref[0])
    def _():
        @pl.when(first_ref[i] == 1)
        def _():
            wgb_ref[j] = wg_ref[...].astype(BF16)
            wub_ref[j] = wu_ref[...].astype(BF16)
            wdb_ref[j] = wd_ref[...].astype(BF16)

        x = x_ref[...]
        hid = jax.nn.silu(_dot(x, wgb_ref[j])) * _dot(x, wub_ref[j])
        part = _dot(hid.astype(BF16), wdb_ref[j])

        @pl.when(j == 0)
        def _():
            o_ref[...] = part

        @pl.when(j == 1)
        def _():
            o_ref[...] += part

    @pl.when(i >= nu_ref[0])
    def _():
        o_ref[...] = jnp.zeros_like(o_ref)


def _moe_blocks(block_e, n_used, xg, wg, wu, wd, layer, tm):
    rows, d = xg.shape
    ff = wg.shape[-1]
    ffh = ff // 2
    n_blocks = rows // tm
    blk = jnp.arange(n_blocks, dtype=jnp.int32)
    prev_e = jnp.concatenate([jnp.full((1,), -1, jnp.int32), block_e[:-1]])
    first = ((block_e != prev_e) & (blk < n_used[0])).astype(jnp.int32)
    half = jnp.where(first[:, None] == 1, jnp.arange(2, dtype=jnp.int32)[None, :], 1).reshape(-1)
    grid_spec = pltpu.PrefetchScalarGridSpec(
        num_scalar_prefetch=4,
        grid=(n_blocks, 2),
        in_specs=[pl.BlockSpec((tm, d), lambda i, j, be, hs, fi, nu: (i, 0)),
                  pl.BlockSpec((None, None, d, ffh), lambda i, j, be, hs, fi, nu: (layer, be[i], 0, hs[2 * i + j])),
                  pl.BlockSpec((None, None, d, ffh), lambda i, j, be, hs, fi, nu: (layer, be[i], 0, hs[2 * i + j])),
                  pl.BlockSpec((None, None, ffh, d), lambda i, j, be, hs, fi, nu: (layer, be[i], hs[2 * i + j], 0))],
        out_specs=pl.BlockSpec((tm, d), lambda i, j, be, hs, fi, nu: (i, 0)),
        scratch_shapes=[pltpu.VMEM((2, d, ffh), BF16), pltpu.VMEM((2, d, ffh), BF16), pltpu.VMEM((2, ffh, d), BF16)],
    )
    return pl.pallas_call(
        _moe_kernel,
        grid_spec=grid_spec,
        out_shape=jax.ShapeDtypeStruct((rows, d), F32),
        compiler_params=_params(("arbitrary", "arbitrary")),
        name="moe_experts",
    )(block_e, half, first, n_used, xg, wg, wu, wd)


def _router_kernel(h_ref, wt_ref, b_ref, tri_ref, e_ref, g_ref, r_ref, cnt_ref, sig_ref, bia_ref, carry_ref):
    @pl.when(pl.program_id(0) == 0)
    def _():
        carry_ref[...] = jnp.zeros_like(carry_ref)

    tm = h_ref.shape[0]
    logits = lax.dot_general(wt_ref[...], h_ref[...], (((1,), (1,)), ((), ())),
                             preferred_element_type=F32, precision=lax.Precision.HIGHEST)
    sig = jax.nn.sigmoid(logits)
    sig_ref[...] = sig
    biased = sig + b_ref[...]
    n_chunks = tm // LANES
    for c in range(n_chunks):
        bia_ref[c] = biased[:, c * LANES:(c + 1) * LANES]
    cand = [jnp.concatenate([bia_ref[c, pl.ds(k, MOE_GROUPS, stride=MOE_PER_GROUP), :] for c in range(n_chunks)],
                            axis=1) for k in range(MOE_PER_GROUP)]

    def top_of(vals):
        best = functools.reduce(jnp.maximum, vals)
        idx = jnp.full(best.shape, MOE_PER_GROUP - 1, jnp.int32)
        for k in range(MOE_PER_GROUP - 2, -1, -1):
            idx = jnp.where(vals[k] == best, k, idx)
        return best, idx

    top1, idx1 = top_of(cand)
    top2, idx2 = top_of([jnp.where(idx1 == k, -jnp.inf, cand[k]) for k in range(MOE_PER_GROUP)])
    gsum = top1 + top2
    gi = lax.broadcasted_iota(jnp.int32, gsum.shape, 0)
    grp = jnp.min(jnp.where(gsum == jnp.max(gsum, axis=0, keepdims=True), gi, MOE_GROUPS), axis=0, keepdims=True)
    chosen = gi == grp
    e1 = grp * MOE_PER_GROUP + jnp.sum(jnp.where(chosen, idx1, 0), axis=0, keepdims=True)
    e2 = grp * MOE_PER_GROUP + jnp.sum(jnp.where(chosen, idx2, 0), axis=0, keepdims=True)
    ei = lax.broadcasted_iota(jnp.int32, (MOE_EXPERTS, tm), 0)
    oh1, oh2 = ei == e1, ei == e2
    sig = sig_ref[...]
    w1 = jnp.sum(jnp.where(oh1, sig, 0.0), axis=0, keepdims=True)
    w2 = jnp.sum(jnp.where(oh2, sig, 0.0), axis=0, keepdims=True)
    e_ref[...] = jnp.concatenate([e1, e2], axis=0)
    g_ref[...] = jnp.concatenate([w1, w2], axis=0) / (w1 + w2)
    both = jnp.where(oh1, 1.0, jnp.where(oh2, 1.0, 0.0))
    base = carry_ref[:, 0:1] + _dot(both.astype(BF16), tri_ref[...])
    r1 = jnp.sum(jnp.where(oh1, base, 0.0), axis=0, keepdims=True)
    r2 = jnp.sum(jnp.where(oh2, base, 0.0), axis=0, keepdims=True)
    r_ref[...] = jnp.concatenate([r1, r2], axis=0).astype(jnp.int32)
    carry_ref[...] += jnp.sum(both, axis=1, keepdims=True)
    cnt_ref[...] = carry_ref[...]


def _route(h, router_w, router_b, tm):
    t, d = h.shape
    e = router_w.shape[1]
    tri = (jnp.arange(tm)[:, None] < jnp.arange(tm)[None, :]).astype(BF16)
    tok = lambda dt: (pl.BlockSpec((MOE_TOPK, tm), lambda i: (0, i)), jax.ShapeDtypeStruct((MOE_TOPK, t), dt))
    (es, esd), (gs, gsd), (rs, rsd) = tok(jnp.int32), tok(F32), tok(jnp.int32)
    return pl.pallas_call(
        _router_kernel,
        grid=(t // tm,),
        in_specs=[pl.BlockSpec((tm, d), lambda i: (i, 0)), pl.BlockSpec((e, d), lambda i: (0, 0)),
                  pl.BlockSpec((e, 1), lambda i: (0, 0)), pl.BlockSpec((tm, tm), lambda i: (0, 0))],
        out_specs=[es, gs, rs, pl.BlockSpec((e, LANES), lambda i: (0, 0))],
        out_shape=[esd, gsd, rsd, jax.ShapeDtypeStruct((e, LANES), F32)],
        scratch_shapes=[pltpu.VMEM((e, tm), F32), pltpu.VMEM((tm // LANES, e, LANES), F32),
                        pltpu.VMEM((e, LANES), F32)],
        compiler_params=_params(("arbitrary",)),
        name="router",
    )(h, router_w.T, router_b.reshape(e, 1), tri)


def _moe_ffn(h, h_bf, router_w, router_b, wg, wu, wd, layer, tm):
    t, d = h.shape
    expert, gate, rank, counts = _route(h, router_w, router_b, min(512, t))
    n_blocks = t * MOE_TOPK // tm + MOE_EXPERTS
    rows = n_blocks * tm
    counts = counts[:, 0].astype(jnp.int32)
    padded = (counts + tm - 1) // tm * tm
    pad_end = jnp.cumsum(padded)
    onehot = expert[..., None] == jnp.arange(MOE_EXPERTS, dtype=jnp.int32)
    pos = rank + jnp.sum(jnp.where(onehot, pad_end - padded, 0), axis=-1)
    tok = jnp.broadcast_to(jnp.arange(t, dtype=jnp.int32), (MOE_TOPK, t))
    buf_tok = jnp.full((rows,), t, jnp.int32).at[pos.reshape(-1)].set(tok.reshape(-1), unique_indices=True)
    block_e = jnp.minimum(jnp.searchsorted(pad_end, jnp.arange(n_blocks, dtype=jnp.int32) * tm, side='right'),
                          MOE_EXPERTS - 1).astype(jnp.int32)
    n_used = (pad_end[-1:] // tm).astype(jnp.int32)
    xg = jnp.concatenate([h_bf, jnp.zeros((1, d), BF16)], axis=0)[buf_tok]
    yb = _moe_blocks(block_e, n_used, xg, wg, wu, wd, layer, tm)
    return yb[pos], gate.T


def _pad_cols(w, n):
    return jnp.pad(w, ((0, 0), (0, n - w.shape[1])))


def kernel(x, router_w, router_b, even_w_in, even_w_out, diff_lam_q1, diff_lam_k1, diff_lam_q2, diff_lam_k2,
           diff_subln_g, ssm_conv_w, ssm_conv_b, ssm_dt_bias, ssm_a_log, ssm_d, ssm_norm_g, odd_w_in,
           odd_w_out, nsa_pe_k, nsa_pe_v, nsa_ck_w1, nsa_ck_w2, nsa_cv_w1, nsa_cv_w2, ln_mix_g, ln_mix_b,
           ln_ffn_g, ln_ffn_b, moe_w_gate, moe_w_up, moe_w_down):
    b, s, d = x.shape
    t = b * s
    xf = x.reshape(t, d)
    xb = xf.astype(BF16)
    for layer in range(DEPTH):
        i = layer // 2
        if layer % 2 == 0:
            lam_init = 0.8 - 0.6 * math.exp(-0.3 * layer)
            w_in = even_w_in[i]
            n_attn = 3 * DIFF_WIDTH
            qkv = _matmul(xb, w_in[:, :n_attn].astype(BF16), BF16, 1024, 1024).reshape(b, s, n_attn)
            n_ssm = -(-(w_in.shape[1] - n_attn) // LANES) * LANES
            pf = _matmul(xb, _pad_cols(w_in[:, n_attn:], n_ssm).astype(BF16), F32, 1024, n_ssm // 3)
            pf = pf.reshape(b, s, n_ssm)
            y_attn = _diff_attention(qkv, diff_lam_q1[i], diff_lam_k1[i], diff_lam_q2[i], diff_lam_k2[i],
                                     diff_subln_g[i], lam_init, 256, 256)
            y_ssm = _mamba2_mixer(pf, ssm_conv_w[i], ssm_conv_b[i], ssm_dt_bias[i], ssm_a_log[i], ssm_d[i],
                                  ssm_norm_g[i])
            mix_in = jnp.concatenate([y_attn, y_ssm], axis=-1).reshape(t, -1)
            w_out = even_w_out[i]
        else:
            w_in = odd_w_in[i]
            n_a = NSA_Q_WIDTH + 4 * NSA_KV_WIDTH
            wq = w_in[:, :NSA_Q_WIDTH]
            wkv = w_in[:, NSA_Q_WIDTH:NSA_Q_WIDTH + 6 * NSA_KV_WIDTH]
            wgate = w_in[:, NSA_Q_WIDTH + 6 * NSA_KV_WIDTH:]
            w_a = jnp.concatenate([wq, wkv[:, 2 * NSA_KV_WIDTH:]], axis=1)
            wgate = jnp.pad(wgate.reshape(d, NSA_KV_GROUPS, 3 * NSA_REP),
                            ((0, 0), (0, 0), (0, LANES - 3 * NSA_REP))).reshape(d, NSA_KV_GROUPS * LANES)
            w_b = jnp.concatenate([wkv[:, :2 * NSA_KV_WIDTH], wgate], axis=1)
            a_bf = _matmul(xb, w_a.astype(BF16), BF16, 1024, 1024).reshape(b, s, n_a)
            b_f32 = _matmul(xb, w_b.astype(BF16), F32, 1024, 512).reshape(b, s, -1)
            k_cmp, v_cmp = _compress(b_f32, nsa_pe_k[i], nsa_pe_v[i], nsa_ck_w1[i], nsa_cv_w1[i],
                                     nsa_ck_w2[i], nsa_cv_w2[i])
            mix_in = _nsa_attention(a_bf, b_f32, k_cmp, v_cmp, 512).reshape(t, -1)
            w_out = odd_w_out[i]
        h, h_bf = _matmul_res_ln(mix_in, w_out.astype(BF16), xf, ln_mix_g[layer], ln_mix_b[layer], 256)
        y2, gates = _moe_ffn(h, h_bf, router_w, router_b, moe_w_gate, moe_w_up, moe_w_down, layer, 256)
        xf, xb = _combine_ln(h, y2, gates, ln_ffn_g[layer], ln_ffn_b[layer], 256)
    return xf.reshape(b, s, d)
```

```python
import functools
import math

import jax
import jax.numpy as jnp
import numpy as np
from jax import lax
from jax.experimental import pallas as pl
from jax.experimental.pallas import tpu as pltpu

F32 = jnp.float32
BF16 = jnp.bfloat16

D_MODEL = 2048
DEPTH = 2
DEEPNORM_ALPHA = (2.0 * DEPTH) ** 0.25
LN_EPS = 1e-5

DIFF_HEADS = 8
DIFF_HEAD_DIM = 64
DIFF_WIDTH = DIFF_HEADS * 2 * DIFF_HEAD_DIM

SSM_INNER = D_MODEL // 2
SSM_HEAD_DIM = 64
SSM_HEADS = SSM_INNER // SSM_HEAD_DIM
SSM_GROUPS = 2
SSM_STATE = 128
SSM_CONV = 4
SSM_CHUNK = 128
SSM_CONV_CH = SSM_INNER + 2 * SSM_GROUPS * SSM_STATE

NSA_HEADS = 16
NSA_KV_GROUPS = 4
NSA_REP = NSA_HEADS // NSA_KV_GROUPS
NSA_HEAD_DIM = 128
NSA_CMP_LEN = 32
NSA_CMP_STRIDE = 16
NSA_SEL_LEN = 64
NSA_SEL_COUNT = 16
NSA_WINDOW = 512
NSA_Q_WIDTH = NSA_HEADS * NSA_HEAD_DIM
NSA_KV_WIDTH = NSA_KV_GROUPS * NSA_HEAD_DIM

MOE_GROUPS = 8
MOE_PER_GROUP = 4
MOE_EXPERTS = MOE_GROUPS * MOE_PER_GROUP
MOE_TOPK = 2
MOE_FF = 1024

NEG = -1e30
BIG = 1e30
LANES = 128
ONES_ROWS = 16
LOG2E = math.log2(math.e)
VMEM_LIMIT = 56 * 1024 * 1024


def _params(sem):
    return pltpu.CompilerParams(dimension_semantics=sem, vmem_limit_bytes=VMEM_LIMIT)


def _dot(a, b):
    return jnp.dot(a, b, preferred_element_type=F32)


def _dot_nt(a, b):
    return lax.dot_general(a, b, (((1,), (1,)), ((), ())), preferred_element_type=F32)


def _mm_kernel(x_ref, w_ref, o_ref):
    o_ref[...] = _dot(x_ref[...], w_ref[...]).astype(o_ref.dtype)


def _matmul(x, w, out_dtype, tm, tn):
    m, k = x.shape
    n = w.shape[1]
    return pl.pallas_call(
        _mm_kernel,
        grid=(n // tn, m // tm),
        in_specs=[pl.BlockSpec((tm, k), lambda j, i: (i, 0)),
                  pl.BlockSpec((k, tn), lambda j, i: (0, j))],
        out_specs=pl.BlockSpec((tm, tn), lambda j, i: (i, j)),
        out_shape=jax.ShapeDtypeStruct((m, n), out_dtype),
        compiler_params=_params(("parallel", "parallel")),
        name="proj_matmul",
    )(x, w)


def _layer_norm_rows(y, g, b):
    mu = jnp.mean(y, -1, keepdims=True)
    yc = y - mu
    var = jnp.mean(yc * yc, -1, keepdims=True)
    return yc * lax.rsqrt(var + LN_EPS) * g + b


def _mm_ln_kernel(a_ref, w_ref, res_ref, g_ref, b_ref, o_ref, ob_ref):
    y = _dot(a_ref[...], w_ref[...]) + DEEPNORM_ALPHA * res_ref[...]
    h = _layer_norm_rows(y, g_ref[...], b_ref[...])
    o_ref[...] = h
    ob_ref[...] = h.astype(BF16)


def _matmul_res_ln(a, w, res, g, b, tm):
    m, k = a.shape
    n = w.shape[1]
    return pl.pallas_call(
        _mm_ln_kernel,
        grid=(m // tm,),
        in_specs=[pl.BlockSpec((tm, k), lambda i: (i, 0)),
                  pl.BlockSpec((k, n), lambda i: (0, 0)),
                  pl.BlockSpec((tm, n), lambda i: (i, 0)),
                  pl.BlockSpec((1, n), lambda i: (0, 0)),
                  pl.BlockSpec((1, n), lambda i: (0, 0))],
        out_specs=[pl.BlockSpec((tm, n), lambda i: (i, 0)),
                   pl.BlockSpec((tm, n), lambda i: (i, 0))],
        out_shape=[jax.ShapeDtypeStruct((m, n), F32), jax.ShapeDtypeStruct((m, n), BF16)],
        compiler_params=_params(("parallel",)),
        name="outproj_ln",
    )(a, w, res, g.reshape(1, n), b.reshape(1, n))


def _combine_ln_kernel(h_ref, y_ref, w_ref, g_ref, b_ref, o_ref, ob_ref):
    w = w_ref[...]
    y = DEEPNORM_ALPHA * h_ref[...] + w[:, 0:1] * y_ref[0] + w[:, 1:2] * y_ref[1]
    x = _layer_norm_rows(y, g_ref[...], b_ref[...])
    o_ref[...] = x
    ob_ref[...] = x.astype(BF16)


def _combine_ln(h, y2, w, g, b, tm):
    m, n = h.shape
    row = pl.BlockSpec((tm, n), lambda i: (i, 0))
    vec = pl.BlockSpec((1, n), lambda i: (0, 0))
    return pl.pallas_call(
        _combine_ln_kernel,
        grid=(m // tm,),
        in_specs=[row, pl.BlockSpec((MOE_TOPK, tm, n), lambda i: (0, i, 0)),
                  pl.BlockSpec((tm, MOE_TOPK), lambda i: (i, 0)), vec, vec],
        out_specs=[row, row],
        out_shape=[jax.ShapeDtypeStruct((m, n), F32), jax.ShapeDtypeStruct((m, n), BF16)],
        compiler_params=_params(("parallel",)),
        name="ffn_combine_ln",
    )(h, y2, w, g.reshape(1, n), b.reshape(1, n))


def _softmax_tile(st_ref, pt_ref, m_ref, al_ref, *, c_exp, mask=None):
    rows = st_ref.shape[1]
    for c in range(rows // LANES):
        cs = slice(c * LANES, (c + 1) * LANES)
        s = st_ref[:, cs]
        if mask is not None:
            s = mask(s, c)
        m_old = m_ref[:, cs]
        m_new = jnp.maximum(m_old, jnp.max(s, axis=0, keepdims=True))
        pt_ref[:, cs] = jnp.exp2((s - m_new[0:1]) * c_exp).astype(BF16)
        al_ref[:, cs] = jnp.exp2((m_old - m_new) * c_exp)
        m_ref[:, cs] = m_new


def _pv_update(vt_aug, pt_ref, al_ref, acc_ref):
    acc_ref[...] = acc_ref[...] * al_ref[0:1, :] + _dot(vt_aug, pt_ref[...])


def _flash_loop(n, scores, softmax, update):
    scores(0, 0)

    @pl.when(n == 1)
    def _():
        softmax(0, 0, True)
        update(0, 0)

    @pl.when(n >= 2)
    def _():
        scores(1, 1)
        softmax(0, 0, False)
        n_pairs = (n - 2) // 2

        def pair(u, carry):
            j = 2 * u
            scores(j + 2, 0)
            softmax(j + 1, 1, False)
            update(j, 0)
            scores(j + 3, 1)
            softmax(j + 2, 0, False)
            update(j + 1, 1)
            return carry

        lax.fori_loop(0, n_pairs, pair, 0)
        j = 2 * n_pairs

        @pl.when(n - 1 == j + 1)
        def _():
            softmax(j + 1, 1, True)
            update(j, 0)
            update(j + 1, 1)

        @pl.when(n - 1 == j + 2)
        def _():
            scores(j + 2, 0)
            softmax(j + 1, 1, False)
            update(j, 0)
            softmax(j + 2, 0, True)
            update(j + 1, 1)
            update(j + 2, 0)


def _transpose_values(v, tk):
    b, s, h, dv = v.shape
    vt = jnp.transpose(v.reshape(b, s // tk, tk, h, dv), (0, 3, 1, 4, 2))
    return jnp.concatenate([vt, jnp.ones((b, h, s // tk, ONES_ROWS, tk), v.dtype)], axis=3)


def _diff_kernel(lq1_ref, lk1_ref, lq2_ref, lk2_ref, g_ref, q_ref, k_ref, vt_ref, o_ref,
                 q2_ref, st0_ref, st1_ref, pt0_ref, pt1_ref, m_ref, al0_ref, al1_ref, acc_ref, *, tq, tk, lam_init):
    i = pl.program_id(2)
    st, pt, al = (st0_ref, st1_ref), (pt0_ref, pt1_ref), (al0_ref, al1_ref)
    qt = q_ref[...].astype(F32).T * DIFF_HEAD_DIM ** -0.5
    half = lax.broadcasted_iota(jnp.int32, (LANES, tq), 0) < DIFF_HEAD_DIM
    q2_ref[...] = jnp.concatenate([jnp.where(half, qt, 0.0), jnp.where(half, 0.0, qt)], axis=1).astype(BF16)
    m_ref[...] = jnp.full_like(m_ref, NEG)
    acc_ref[...] = jnp.zeros_like(acc_ref)

    def scores(j, slot):
        kt = k_ref[pl.ds(pl.multiple_of(j * tk, tk), tk), :]
        st[slot][...] = _dot(kt, q2_ref[...])

    def softmax(j, slot, masked):
        def causal(s, c):
            key = j * tk + lax.broadcasted_iota(jnp.int32, (tk, LANES), 0)
            qry = i * tq + ((c * LANES + lax.broadcasted_iota(jnp.int32, (tk, LANES), 1)) & (tq - 1))
            return jnp.where(key <= qry, s, NEG)
        _softmax_tile(st[slot], pt[slot], m_ref, al[slot], c_exp=LOG2E, mask=causal if masked else None)

    def update(j, slot):
        _pv_update(vt_ref[j], pt[slot], al[slot], acc_ref)

    _flash_loop((i * tq + tq + tk - 1) // tk, scores, softmax, update)
    acc = acc_ref[...]
    o = (acc[:LANES] / acc[LANES:LANES + 1]).T
    lam = (jnp.exp(jnp.sum(lq1_ref[...] * lk1_ref[...], -1, keepdims=True))
           - jnp.exp(jnp.sum(lq2_ref[...] * lk2_ref[...], -1, keepdims=True)) + lam_init)
    od = o[:tq] - lam * o[tq:]
    y = od * lax.rsqrt(jnp.mean(od * od, -1, keepdims=True) + 1e-5) * g_ref[...]
    o_ref[...] = (y * (1.0 - lam_init)).astype(o_ref.dtype)


def _diff_attention(qkv, lq1, lk1, lq2, lk2, subln_g, lam_init, tq, tk):
    b, s, _ = qkv.shape
    rows = 2 * tq
    dv = 2 * DIFF_HEAD_DIM
    assert tk % tq == 0 and s % tk == 0
    vt = _transpose_values(qkv[..., 2 * DIFF_WIDTH:].reshape(b, s, DIFF_HEADS, dv), tk)
    vec64 = pl.BlockSpec((1, DIFF_HEAD_DIM), lambda bb, h, i: (0, 0))
    return pl.pallas_call(
        functools.partial(_diff_kernel, tq=tq, tk=tk, lam_init=lam_init),
        grid=(b, DIFF_HEADS, s // tq),
        in_specs=[vec64, vec64, vec64, vec64,
                  pl.BlockSpec((1, LANES), lambda bb, h, i: (0, 0)),
                  pl.BlockSpec((None, tq, LANES), lambda bb, h, i: (bb, i, h)),
                  pl.BlockSpec((None, s, LANES), lambda bb, h, i: (bb, 0, DIFF_HEADS + h)),
                  pl.BlockSpec((None, None, s // tk, dv + ONES_ROWS, tk), lambda bb, h, i: (bb, h, 0, 0, 0))],
        out_specs=pl.BlockSpec((None, tq, LANES), lambda bb, h, i: (bb, i, h)),
        out_shape=jax.ShapeDtypeStruct((b, s, DIFF_WIDTH), BF16),
        scratch_shapes=[pltpu.VMEM((LANES, rows), BF16), pltpu.VMEM((tk, rows), F32), pltpu.VMEM((tk, rows), F32),
                        pltpu.VMEM((tk, rows), BF16), pltpu.VMEM((tk, rows), BF16),
                        pltpu.VMEM((8, rows), F32), pltpu.VMEM((8, rows), F32), pltpu.VMEM((8, rows), F32),
                        pltpu.VMEM((dv + ONES_ROWS, rows), F32)],
        compiler_params=_params(("parallel", "parallel", "arbitrary")),
        name="diff_attention",
    )(lq1.reshape(1, -1), lk1.reshape(1, -1), lq2.reshape(1, -1), lk2.reshape(1, -1),
      subln_g.reshape(1, -1), qkv, qkv, vt)


def _ssd_kernel(pf_ref, cw_ref, cb_ref, dtb_ref, alog_ref, drep_ref, ng_ref, tri_ref, o_ref,
                xe_ref, tail_ref, state_ref, y_ref):
    L = SSM_CHUNK
    halo = 8

    @pl.when(pl.program_id(1) == 0)
    def _():
        tail_ref[...] = jnp.zeros_like(tail_ref)
        state_ref[...] = jnp.zeros_like(state_ref)

    z = pf_ref[:, 0:SSM_INNER]
    xbc = pf_ref[:, SSM_INNER:SSM_INNER + SSM_CONV_CH]
    dt_raw = pf_ref[:, SSM_INNER + SSM_CONV_CH:SSM_INNER + SSM_CONV_CH + LANES]
    xe_ref[0:halo, :] = tail_ref[...]
    xe_ref[halo:halo + L, :] = xbc
    tail_ref[...] = xbc[L - halo:L, :]
    conv = cb_ref[...] + cw_ref[SSM_CONV - 1:SSM_CONV, :] * xbc
    for k in range(SSM_CONV - 1):
        conv = conv + cw_ref[k:k + 1, :] * xe_ref[pl.ds(halo - (SSM_CONV - 1) + k, L), :]
    xbc = conv * jax.nn.sigmoid(conv)
    xs = xbc[:, :SSM_INNER]
    dt = jax.nn.softplus(dt_raw + dtb_ref[...])
    adt = dt * (-jnp.exp(alog_ref[...]))
    acum = jnp.dot(tri_ref[...], adt, preferred_element_type=F32, precision=lax.Precision.HIGHEST)
    acum_t = acum.T
    left = lax.broadcasted_iota(jnp.int32, (1, LANES), 1) < SSM_HEAD_DIM
    tril = lax.broadcasted_iota(jnp.int32, (L, L), 0) >= lax.broadcasted_iota(jnp.int32, (L, L), 1)
    pairs_per_group = SSM_HEADS // SSM_GROUPS // 2
    for g in range(SSM_GROUPS):
        bm = xbc[:, SSM_INNER + g * SSM_STATE:SSM_INNER + (g + 1) * SSM_STATE]
        cm = xbc[:, SSM_INNER + (SSM_GROUPS + g) * SSM_STATE:SSM_INNER + (SSM_GROUPS + g + 1) * SSM_STATE]
        cb = _dot_nt(cm.astype(BF16), bm.astype(BF16))
        bm_t = bm.T
        for qq in range(pairs_per_group):
            q = g * pairs_per_group + qq
            ls = slice(q * LANES, (q + 1) * LANES)
            xs_p = xs[:, ls]
            h0, h1 = 2 * q, 2 * q + 1
            xd = xs_p * jnp.where(left, dt[:, h0:h0 + 1], dt[:, h1:h1 + 1])
            prev = state_ref[q]
            y = drep_ref[:, ls] * xs_p
            new = jnp.zeros((SSM_STATE, LANES), F32)
            for h, keep in ((h0, left), (h1, jnp.logical_not(left))):
                xd_h = jnp.where(keep, xd, 0.0).astype(BF16)
                prev_h = jnp.where(keep, prev, 0.0).astype(BF16)
                cs_col = acum[:, h:h + 1]
                cs_row = acum_t[h:h + 1, :]
                w = cb * jnp.exp(jnp.where(tril, cs_col - cs_row, -jnp.inf))
                y = y + _dot(w.astype(BF16), xd_h) + _dot((cm * jnp.exp(cs_col)).astype(BF16), prev_h)
                new = new + _dot((bm_t * jnp.exp(acum[L - 1:L, h:h + 1] - cs_row)).astype(BF16), xd_h)
            decay = jnp.where(left, jnp.exp(acum[L - 1:L, h0:h0 + 1]), jnp.exp(acum[L - 1:L, h1:h1 + 1]))
            state_ref[q] = prev * decay + new
            y_ref[:, ls] = y
    y = y_ref[...] * (z * jax.nn.sigmoid(z))
    gw = SSM_INNER // SSM_GROUPS
    for g in range(SSM_GROUPS):
        yg = y[:, g * gw:(g + 1) * gw]
        yg = yg * lax.rsqrt(jnp.mean(yg * yg, -1, keepdims=True) + 1e-5) * ng_ref[:, g * gw:(g + 1) * gw]
        o_ref[:, g * gw:(g + 1) * gw] = yg.astype(o_ref.dtype)


def _mamba2_mixer(pf, conv_w, conv_b, dt_bias, a_log, d_skip, norm_g):
    b, s, w = pf.shape
    L = SSM_CHUNK
    pad_heads = lambda v: jnp.pad(v, (0, LANES - SSM_HEADS)).reshape(1, LANES)
    tri = jnp.tril(jnp.ones((L, L), F32))
    full = lambda shape: pl.BlockSpec(shape, lambda bb, c: (0,) * len(shape))
    return pl.pallas_call(
        _ssd_kernel,
        grid=(b, s // L),
        in_specs=[pl.BlockSpec((None, L, w), lambda bb, c: (bb, c, 0)),
                  full((SSM_CONV, SSM_CONV_CH)), full((1, SSM_CONV_CH)), full((1, LANES)), full((1, LANES)),
                  full((1, SSM_INNER)), full((1, SSM_INNER)), full((L, L))],
        out_specs=pl.BlockSpec((None, L, SSM_INNER), lambda bb, c: (bb, c, 0)),
        out_shape=jax.ShapeDtypeStruct((b, s, SSM_INNER), BF16),
        scratch_shapes=[pltpu.VMEM((L + 8, SSM_CONV_CH), F32), pltpu.VMEM((8, SSM_CONV_CH), F32),
                        pltpu.VMEM((SSM_HEADS // 2, SSM_STATE, LANES), F32), pltpu.VMEM((L, SSM_INNER), F32)],
        compiler_params=_params(("parallel", "arbitrary")),
        name="ssd_mixer",
    )(pf, conv_w, conv_b.reshape(1, -1), pad_heads(dt_bias), pad_heads(a_log),
      jnp.repeat(d_skip, SSM_HEAD_DIM).reshape(1, -1), norm_g.reshape(1, -1), tri)


def _cmp_kernel(x_ref, pek_ref, pev_ref, w1k_ref, w1v_ref, w2k_ref, w2v_ref, ko_ref, vo_ref,
                top_ref, bot_ref, *, nb):
    l = pl.program_id(1)
    g4 = NSA_KV_GROUPS

    @pl.when(l == 0)
    def _():
        top_ref[...] = jnp.zeros_like(top_ref)
        bot_ref[...] = jnp.zeros_like(bot_ref)

    x = x_ref[...]
    for kv, (pe_ref, w1_ref) in enumerate(((pek_ref, w1k_ref), (pev_ref, w1v_ref))):
        x4 = jnp.concatenate([x[:, kv * NSA_KV_WIDTH + g * LANES: kv * NSA_KV_WIDTH + (g + 1) * LANES]
                              for g in range(g4)], axis=0)
        top_ref[kv] += _dot((x4 + pe_ref[pl.ds(l, 1), :]).astype(BF16), w1_ref[l])
        bot_ref[kv] += _dot((x4 + pe_ref[pl.ds(l + NSA_CMP_STRIDE, 1), :]).astype(BF16),
                            w1_ref[l + NSA_CMP_STRIDE])

    @pl.when(l == NSA_CMP_STRIDE - 1)
    def _():
        last = lax.broadcasted_iota(jnp.int32, (nb, LANES), 0) == nb - 1
        for kv, (w2_ref, o_ref) in enumerate(((w2k_ref, ko_ref), (w2v_ref, vo_ref))):
            for g in range(g4):
                top = top_ref[kv, g * nb:(g + 1) * nb, :]
                bot = bot_ref[kv, g * nb:(g + 1) * nb, :]
                hid = top + pltpu.roll(bot, nb - 1, 0)
                out = _dot(jax.nn.gelu(hid, approximate=True).astype(BF16), w2_ref[...])
                o_ref[g] = jnp.where(last, 0.0, out).astype(o_ref.dtype)


def _compress(bf, pe_k, pe_v, w1k, w1v, w2k, w2v):
    b, s, w = bf.shape
    nb = s // NSA_CMP_STRIDE
    xv = bf.reshape(b, nb, NSA_CMP_STRIDE * w)
    full = lambda shape: pl.BlockSpec(shape, lambda bb, l: (0,) * len(shape))
    out_spec = pl.BlockSpec((None, NSA_KV_GROUPS, nb, LANES), lambda bb, l: (bb, 0, 0, 0))
    out_sd = jax.ShapeDtypeStruct((b, NSA_KV_GROUPS, nb, LANES), BF16)
    w1 = lambda a: a.astype(BF16).reshape(NSA_CMP_LEN, NSA_HEAD_DIM, -1)
    return pl.pallas_call(
        functools.partial(_cmp_kernel, nb=nb),
        grid=(b, NSA_CMP_STRIDE),
        in_specs=[pl.BlockSpec((None, nb, w), lambda bb, l: (bb, 0, l)),
                  full((NSA_CMP_LEN, LANES)), full((NSA_CMP_LEN, LANES)),
                  full((NSA_CMP_LEN, NSA_HEAD_DIM, LANES)), full((NSA_CMP_LEN, NSA_HEAD_DIM, LANES)),
                  full((LANES, LANES)), full((LANES, LANES))],
        out_specs=[out_spec, out_spec],
        out_shape=[out_sd, out_sd],
        scratch_shapes=[pltpu.VMEM((2, NSA_KV_GROUPS * nb, LANES), F32),
                        pltpu.VMEM((2, NSA_KV_GROUPS * nb, LANES), F32)],
        compiler_params=_params(("parallel", "arbitrary")),
        name="nsa_compress",
    )(xv, pe_k, pe_v, w1(w1k), w1(w1v), w2k.astype(BF16), w2v.astype(BF16))


def _nsa_kernel(q_ref, gate_ref, kc_ref, vct_ref, ks_ref, vst_ref, kw_ref, vwt_ref, blk_ref, ovt_ref, o_ref,
                qa_ref, sc_ref, pc_ref, ps_ref, st0_ref, st1_ref, pt0_ref, pt1_ref, sw_ref, pw_ref,
                m_ref, al0_ref, al1_ref, acc_ref, mw_ref, alw_ref, accw_ref, *, tk, seq, slab):
    i = pl.program_id(2)
    tq = LANES
    t0 = i * tq
    rep = NSA_REP
    ncp = seq // NSA_CMP_STRIDE
    scale = NSA_HEAD_DIM ** -0.5
    c_exp = scale * LOG2E
    q = q_ref[...].astype(F32)
    for r in range(rep):
        qa_ref[0:LANES, r * tq:(r + 1) * tq] = q[:, r * LANES:(r + 1) * LANES].T.astype(BF16)
    qs = qa_ref[0:LANES, :]
    t_lane = t0 + lax.broadcasted_iota(jnp.int32, (1, LANES), 1)

    sc_ref[...] = _dot(kc_ref[...], qs)
    n_end = lax.broadcasted_iota(jnp.int32, (ncp, LANES), 0) * NSA_CMP_STRIDE + (NSA_CMP_LEN - 1)
    cmask = n_end <= t_lane
    for r in range(rep):
        cs = slice(r * tq, (r + 1) * tq)
        s = jnp.where(cmask, sc_ref[:, cs] * scale, NEG)
        e = jnp.where(cmask, jnp.exp(s - jnp.max(s, axis=0, keepdims=True)), 0.0)
        den = jnp.sum(e, axis=0, keepdims=True)
        p = e / jnp.where(den > 0, den, 1.0)
        pc_ref[:, cs] = p.astype(BF16)
        if r == 0:
            ps_ref[...] = p
        else:
            ps_ref[...] += p
    o_cmp_t = _dot(vct_ref[...], pc_ref[...])

    ps = ps_ref[...]
    p_hi = ps.astype(BF16)
    p_lo = (ps - p_hi.astype(F32)).astype(BF16)
    imp_t = _dot(ovt_ref[...], p_hi) + _dot(ovt_ref[...], p_lo)
    cur = t_lane >> 6
    mi = lax.broadcasted_iota(jnp.int32, (LANES, LANES), 0)
    forced = (mi == 0) | (mi == cur) | (mi == cur - 1)
    vals = jnp.where(forced, -jnp.inf, jnp.where(mi <= cur, imp_t, NEG))
    pen = jnp.where(forced, 0.0, -1e9)
    mf = mi.astype(F32)
    for _ in range(NSA_SEL_COUNT - 3):
        mx = jnp.max(vals, axis=0, keepdims=True)
        first = jnp.min(jnp.where(vals == mx, mf, float(LANES)), axis=0, keepdims=True)
        hit = mf == first
        pen = jnp.where(hit, 0.0, pen)
        vals = jnp.where(hit, -jnp.inf, vals)
    pen_b = pen.astype(BF16)
    for r in range(rep):
        qa_ref[LANES:2 * LANES, r * tq:(r + 1) * tq] = pen_b

    ws = pl.multiple_of(jnp.clip(t0 - NSA_WINDOW, 0, seq - slab), LANES)
    sw_ref[...] = _dot(kw_ref[pl.ds(ws, slab), :], qs)
    mw_ref[...] = jnp.full_like(mw_ref, NEG)
    accw_ref[...] = jnp.zeros_like(accw_ref)
    jb = ws // LANES
    v_slab = jnp.concatenate([vwt_ref[jb + u] for u in range(slab // LANES)], axis=1)

    def window(s, c):
        kpos = ws + lax.broadcasted_iota(jnp.int32, (slab, LANES), 0)
        return jnp.where(kpos <= t_lane, jnp.where(kpos > t_lane - NSA_WINDOW, s, NEG), NEG)

    _softmax_tile(sw_ref, pw_ref, mw_ref, alw_ref, c_exp=c_exp, mask=window)
    _pv_update(v_slab, pw_ref, alw_ref, accw_ref)
    accw = accw_ref[...]
    o_win_t = accw[:LANES] / accw[LANES:LANES + 1]

    m_ref[...] = jnp.full_like(m_ref, NEG)
    acc_ref[...] = jnp.zeros_like(acc_ref)
    st, pt, al = (st0_ref, st1_ref), (pt0_ref, pt1_ref), (al0_ref, al1_ref)

    def scores(j, slot):
        k0 = pl.multiple_of(j * tk, tk)
        ka = jnp.concatenate([ks_ref[pl.ds(k0, tk), :], blk_ref[pl.ds(k0, tk), :]], axis=1)
        st[slot][...] = _dot(ka, qa_ref[...])

    def softmax(j, slot, masked):
        def causal(s, c):
            kpos = j * tk + lax.broadcasted_iota(jnp.int32, (tk, LANES), 0)
            return jnp.where(kpos <= t_lane, s, NEG)
        _softmax_tile(st[slot], pt[slot], m_ref, al[slot], c_exp=c_exp, mask=causal if masked else None)

    def update(j, slot):
        _pv_update(vst_ref[j], pt[slot], al[slot], acc_ref)

    _flash_loop((t0 + tq + tk - 1) // tk, scores, softmax, update)
    acc = acc_ref[...]
    o_sel_t = acc[:LANES] / acc[LANES:LANES + 1]

    g_t = jax.nn.sigmoid(gate_ref[...]).T
    for r in range(rep):
        cs = slice(r * tq, (r + 1) * tq)
        out_t = (g_t[3 * r:3 * r + 1] * o_cmp_t[:, cs] + g_t[3 * r + 1:3 * r + 2] * o_sel_t[:, cs]
                 + g_t[3 * r + 2:3 * r + 3] * o_win_t[:, cs])
        o_ref[:, r * LANES:(r + 1) * LANES] = out_t.T.astype(o_ref.dtype)


def _nsa_attention(a_bf, b_f32, k_cmp, v_cmp, tk):
    b, s, _ = a_bf.shape
    g4 = NSA_KV_GROUPS
    tq = LANES
    rows = NSA_REP * tq
    ncp = s // NSA_CMP_STRIDE
    slab = NSA_WINDOW + tq
    dva = NSA_HEAD_DIM + ONES_ROWS
    assert s >= slab and s % tk == 0 and tk % tq == 0
    blk = (jnp.arange(s)[:, None] // NSA_SEL_LEN == jnp.arange(LANES)[None, :]).astype(BF16)
    cs = np.arange(ncp)[:, None] * NSA_CMP_STRIDE
    ss = np.arange(LANES)[None, :] * NSA_SEL_LEN
    ov = np.maximum(np.minimum(cs + NSA_CMP_LEN, ss + NSA_SEL_LEN) - np.maximum(cs, ss), 0) / NSA_CMP_LEN
    ov[ncp - 1] = 0.0
    qoff = NSA_Q_WIDTH // LANES
    group_values = lambda n: a_bf[..., NSA_Q_WIDTH + n * NSA_KV_WIDTH:NSA_Q_WIDTH + (n + 1) * NSA_KV_WIDTH].reshape(
        b, s, g4, NSA_HEAD_DIM)
    vst = _transpose_values(group_values(1), tk)
    vwt = _transpose_values(group_values(3), LANES)
    vct = jnp.swapaxes(v_cmp, 2, 3)
    k_spec = lambda n: pl.BlockSpec((None, s, LANES), lambda bb, g, i: (bb, 0, qoff + n * g4 + g))
    vt_spec = lambda t: pl.BlockSpec((None, None, s // t, dva, t), lambda bb, g, i: (bb, g, 0, 0, 0))
    return pl.pallas_call(
        functools.partial(_nsa_kernel, tk=tk, seq=s, slab=slab),
        grid=(b, g4, s // tq),
        in_specs=[pl.BlockSpec((None, tq, NSA_REP * LANES), lambda bb, g, i: (bb, i, g)),
                  pl.BlockSpec((None, tq, LANES), lambda bb, g, i: (bb, i, 2 * g4 + g)),
                  pl.BlockSpec((None, None, ncp, LANES), lambda bb, g, i: (bb, g, 0, 0)),
                  pl.BlockSpec((None, None, LANES, ncp), lambda bb, g, i: (bb, g, 0, 0)),
                  k_spec(0), vt_spec(tk), k_spec(2), vt_spec(LANES),
                  pl.BlockSpec((s, LANES), lambda bb, g, i: (0, 0)),
                  pl.BlockSpec((LANES, ncp), lambda bb, g, i: (0, 0))],
        out_specs=pl.BlockSpec((None, tq, NSA_REP * LANES), lambda bb, g, i: (bb, i, g)),
        out_shape=jax.ShapeDtypeStruct((b, s, NSA_Q_WIDTH), BF16),
        scratch_shapes=[pltpu.VMEM((2 * LANES, rows), BF16),
                        pltpu.VMEM((ncp, rows), F32), pltpu.VMEM((ncp, rows), BF16), pltpu.VMEM((ncp, LANES), F32),
                        pltpu.VMEM((tk, rows), F32), pltpu.VMEM((tk, rows), F32),
                        pltpu.VMEM((tk, rows), BF16), pltpu.VMEM((tk, rows), BF16),
                        pltpu.VMEM((slab, rows), F32), pltpu.VMEM((slab, rows), BF16),
                        pltpu.VMEM((8, rows), F32), pltpu.VMEM((8, rows), F32), pltpu.VMEM((8, rows), F32),
                        pltpu.VMEM((dva, rows), F32),
                        pltpu.VMEM((8, rows), F32), pltpu.VMEM((8, rows), F32), pltpu.VMEM((dva, rows), F32)],
        compiler_params=_params(("parallel", "parallel", "arbitrary")),
        name="nsa_attention",
    )(a_bf, b_f32, k_cmp, vct, a_bf, vst, a_bf, vwt, blk, jnp.asarray(ov.T, BF16))


def _moe_kernel(be_ref, hs_ref, first_ref, nu_ref, x_ref, wg_ref, wu_ref, wd_ref, o_ref, wgb_ref, wub_ref, wdb_ref):
    i, j = pl.program_id(0), pl.program_id(1)

    @pl.when(i < nu_ref[0])
    def _():
        @pl.when(first_ref[i] == 1)
        def _():
            wgb_ref[j] = wg_ref[...].astype(BF16)
            wub_ref[j] = wu_ref[...].astype(BF16)
            wdb_ref[j] = wd_ref[...].astype(BF16)

        x = x_ref[...]
        hid = jax.nn.silu(_dot(x, wgb_ref[j])) * _dot(x, wub_ref[j])
        part = _dot(hid.astype(BF16), wdb_ref[j])

        @pl.when(j == 0)
        def _():
            o_ref[...] = part

        @pl.when(j == 1)
        def _():
            o_ref[...] += part

    @pl.when(i >= nu_ref[0])
    def _():
        o_ref[...] = jnp.zeros_like(o_ref)


def _moe_blocks(block_e, n_used, xg, wg, wu, wd, layer, tm):
    rows, d = xg.shape
    ff = wg.shape[-1]
    ffh = ff // 2
    n_blocks = rows // tm
    blk = jnp.arange(n_blocks, dtype=jnp.int32)
    prev_e = jnp.concatenate([jnp.full((1,), -1, jnp.int32), block_e[:-1]])
    first = ((block_e != prev_e) & (blk < n_used[0])).astype(jnp.int32)
    half = jnp.where(first[:, None] == 1, jnp.arange(2, dtype=jnp.int32)[None, :], 1).reshape(-1)
    grid_spec = pltpu.PrefetchScalarGridSpec(
        num_scalar_prefetch=4,
        grid=(n_blocks, 2),
        in_specs=[pl.BlockSpec((tm, d), lambda i, j, be, hs, fi, nu: (i, 0)),
                  pl.BlockSpec((None, None, d, ffh), lambda i, j, be, hs, fi, nu: (layer, be[i], 0, hs[2 * i + j])),
                  pl.BlockSpec((None, None, d, ffh), lambda i, j, be, hs, fi, nu: (layer, be[i], 0, hs[2 * i + j])),
                  pl.BlockSpec((None, None, ffh, d), lambda i, j, be, hs, fi, nu: (layer, be[i], hs[2 * i + j], 0))],
        out_specs=pl.BlockSpec((tm, d), lambda i, j, be, hs, fi, nu: (i, 0)),
        scratch_shapes=[pltpu.VMEM((2, d, ffh), BF16), pltpu.VMEM((2, d, ffh), BF16), pltpu.VMEM((2, ffh, d), BF16)],
    )
    return pl.pallas_call(
        _moe_kernel,
        grid_spec=grid_spec,
        out_shape=jax.ShapeDtypeStruct((rows, d), F32),
        compiler_params=_params(("arbitrary", "arbitrary")),
        name="moe_experts",
    )(block_e, half, first, n_used, xg, wg, wu, wd)


def _router_kernel(h_ref, wt_ref, b_ref, tri_ref, e_ref, g_ref, r_ref, cnt_ref, sig_ref, bia_ref, carry_ref):
    @pl.when(pl.program_id(0) == 0)
    def _():
        carry_ref[...] = jnp.zeros_like(carry_ref)

    tm = h_ref.shape[0]
    logits = lax.dot_general(wt_ref[...], h_ref[...], (((1,), (1,)), ((), ())),
                             preferred_element_type=F32, precision=lax.Precision.HIGHEST)
    sig = jax.nn.sigmoid(logits)
    sig_ref[...] = sig
    biased = sig + b_ref[...]
    n_chunks = tm // LANES
    for c in range(n_chunks):
        bia_ref[c] = biased[:, c * LANES:(c + 1) * LANES]
    cand = [jnp.concatenate([bia_ref[c, pl.ds(k, MOE_GROUPS, stride=MOE_PER_GROUP), :] for c in range(n_chunks)],
                            axis=1) for k in range(MOE_PER_GROUP)]

    def top_of(vals):
        best = functools.reduce(jnp.maximum, vals)
        idx = jnp.full(best.shape, MOE_PER_GROUP - 1, jnp.int32)
        for k in range(MOE_PER_GROUP - 2, -1, -1):
            idx = jnp.where(vals[k] == best, k, idx)
        return best, idx

    top1, idx1 = top_of(cand)
    top2, idx2 = top_of([jnp.where(idx1 == k, -jnp.inf, cand[k]) for k in range(MOE_PER_GROUP)])
    gsum = top1 + top2
    gi = lax.broadcasted_iota(jnp.int32, gsum.shape, 0)
    grp = jnp.min(jnp.where(gsum == jnp.max(gsum, axis=0, keepdims=True), gi, MOE_GROUPS), axis=0, keepdims=True)
    chosen = gi == grp
    e1 = grp * MOE_PER_GROUP + jnp.sum(jnp.where(chosen, idx1, 0), axis=0, keepdims=True)
    e2 = grp * MOE_PER_GROUP + jnp.sum(jnp.where(chosen, idx2, 0), axis=0, keepdims=True)
    ei = lax.broadcasted_iota(jnp.int32, (MOE_EXPERTS, tm), 0)
    oh1, oh2 = ei == e1, ei == e2
    sig = sig_ref[...]
    w1 = jnp.sum(jnp.where(oh1, sig, 0.0), axis=0, keepdims=True)
    w2 = jnp.sum(jnp.where(oh2, sig, 0.0), axis=0, keepdims=True)
    e_ref[...] = jnp.concatenate([e1, e2], axis=0)
    g_ref[...] = jnp.concatenate([w1, w2], axis=0) / (w1 + w2)
    both = jnp.where(oh1, 1.0, jnp.where(oh2, 1.0, 0.0))
    base = carry_ref[:, 0:1] + _dot(both.astype(BF16), tri_ref[...])
    r1 = jnp.sum(jnp.where(oh1, base, 0.0), axis=0, keepdims=True)
    r2 = jnp.sum(jnp.where(oh2, base, 0.0), axis=0, keepdims=True)
    r_ref[...] = jnp.concatenate([r1, r2], axis=0).astype(jnp.int32)
    carry_ref[...] += jnp.sum(both, axis=1, keepdims=True)
    cnt_ref[...] = carry_ref[...]


def _route(h, router_w, router_b, tm):
    t, d = h.shape
    e = router_w.shape[1]
    tri = (jnp.arange(tm)[:, None] < jnp.arange(tm)[None, :]).astype(BF16)
    tok = lambda dt: (pl.BlockSpec((MOE_TOPK, tm), lambda i: (0, i)), jax.ShapeDtypeStruct((MOE_TOPK, t), dt))
    (es, esd), (gs, gsd), (rs, rsd) = tok(jnp.int32), tok(F32), tok(jnp.int32)
    return pl.pallas_call(
        _router_kernel,
        grid=(t // tm,),
        in_specs=[pl.BlockSpec((tm, d), lambda i: (i, 0)), pl.BlockSpec((e, d), lambda i: (0, 0)),
                  pl.BlockSpec((e, 1), lambda i: (0, 0)), pl.BlockSpec((tm, tm), lambda i: (0, 0))],
        out_specs=[es, gs, rs, pl.BlockSpec((e, LANES), lambda i: (0, 0))],
        out_shape=[esd, gsd, rsd, jax.ShapeDtypeStruct((e, LANES), F32)],
        scratch_shapes=[pltpu.VMEM((e, tm), F32), pltpu.VMEM((tm // LANES, e, LANES), F32),
                        pltpu.VMEM((e, LANES), F32)],
        compiler_params=_params(("arbitrary",)),
        name="router",
    )(h, router_w.T, router_b.reshape(e, 1), tri)


def _moe_ffn(h, h_bf, router_w, router_b, wg, wu, wd, layer, tm):
    t, d = h.shape
    expert, gate, rank, counts = _route(h, router_w, router_b, min(512, t))
    n_blocks = t * MOE_TOPK // tm + MOE_EXPERTS
    rows = n_blocks * tm
    counts = counts[:, 0].astype(jnp.int32)
    padded = (counts + tm - 1) // tm * tm
    pad_end = jnp.cumsum(padded)
    onehot = expert[..., None] == jnp.arange(MOE_EXPERTS, dtype=jnp.int32)
    pos = rank + jnp.sum(jnp.where(onehot, pad_end - padded, 0), axis=-1)
    tok = jnp.broadcast_to(jnp.arange(t, dtype=jnp.int32), (MOE_TOPK, t))
    buf_tok = jnp.full((rows,), t, jnp.int32).at[pos.reshape(-1)].set(tok.reshape(-1), unique_indices=True)
    block_start = jnp.arange(n_blocks, dtype=jnp.int32) * tm
    block_e = jnp.minimum(jnp.sum((pad_end[None, :] <= block_start[:, None]).astype(jnp.int32), axis=1),
                          MOE_EXPERTS - 1)
    n_used = (pad_end[-1:] // tm).astype(jnp.int32)
    xg = jnp.concatenate([h_bf, jnp.zeros((1, d), BF16)], axis=0)[buf_tok]
    yb = _moe_blocks(block_e, n_used, xg, wg, wu, wd, layer, tm)
    return yb[pos], gate.T


def _pad_cols(w, n):
    return jnp.pad(w, ((0, 0), (0, n - w.shape[1])))


def kernel(x, router_w, router_b, even_w_in, even_w_out, diff_lam_q1, diff_lam_k1, diff_lam_q2, diff_lam_k2,
           diff_subln_g, ssm_conv_w, ssm_conv_b, ssm_dt_bias, ssm_a_log, ssm_d, ssm_norm_g, odd_w_in,
           odd_w_out, nsa_pe_k, nsa_pe_v, nsa_ck_w1, nsa_ck_w2, nsa_cv_w1, nsa_cv_w2, ln_mix_g, ln_mix_b,
           ln_ffn_g, ln_ffn_b, moe_w_gate, moe_w_up, moe_w_down):
    b, s, d = x.shape
    t = b * s
    xf = x.reshape(t, d)
    xb = xf.astype(BF16)
    for layer in range(DEPTH):
        i = layer // 2
        if layer % 2 == 0:
            lam_init = 0.8 - 0.6 * math.exp(-0.3 * layer)
            w_in = even_w_in[i]
            n_attn = 3 * DIFF_WIDTH
            qkv = _matmul(xb, w_in[:, :n_attn].astype(BF16), BF16, 1024, 1024).reshape(b, s, n_attn)
            n_ssm = -(-(w_in.shape[1] - n_attn) // LANES) * LANES
            pf = _matmul(xb, _pad_cols(w_in[:, n_attn:], n_ssm).astype(BF16), F32, 1024, n_ssm // 3)
            pf = pf.reshape(b, s, n_ssm)
            y_attn = _diff_attention(qkv, diff_lam_q1[i], diff_lam_k1[i], diff_lam_q2[i], diff_lam_k2[i],
                                     diff_subln_g[i], lam_init, 256, 256)
            y_ssm = _mamba2_mixer(pf, ssm_conv_w[i], ssm_conv_b[i], ssm_dt_bias[i], ssm_a_log[i], ssm_d[i],
                                  ssm_norm_g[i])
            mix_in = jnp.concatenate([y_attn, y_ssm], axis=-1).reshape(t, -1)
            w_out = even_w_out[i]
        else:
            w_in = odd_w_in[i]
            n_a = NSA_Q_WIDTH + 4 * NSA_KV_WIDTH
            wq = w_in[:, :NSA_Q_WIDTH]
            wkv = w_in[:, NSA_Q_WIDTH:NSA_Q_WIDTH + 6 * NSA_KV_WIDTH]
            wgate = w_in[:, NSA_Q_WIDTH + 6 * NSA_KV_WIDTH:]
            w_a = jnp.concatenate([wq, wkv[:, 2 * NSA_KV_WIDTH:]], axis=1)
            wgate = jnp.pad(wgate.reshape(d, NSA_KV_GROUPS, 3 * NSA_REP),
                            ((0, 0), (0, 0), (0, LANES - 3 * NSA_REP))).reshape(d, NSA_KV_GROUPS * LANES)
            w_b = jnp.concatenate([wkv[:, :2 * NSA_KV_WIDTH], wgate], axis=1)
            a_bf = _matmul(xb, w_a.astype(BF16), BF16, 1024, 1024).reshape(b, s, n_a)
            b_f32 = _matmul(xb, w_b.astype(BF16), F32, 1024, 512).reshape(b, s, -1)
            k_cmp, v_cmp = _compress(b_f32, nsa_pe_k[i], nsa_pe_v[i], nsa_ck_w1[i], nsa_cv_w1[i],
                                     nsa_ck_w2[i], nsa_cv_w2[i])
            mix_in = _nsa_attention(a_bf, b_f32, k_cmp, v_cmp, 512).reshape(t, -1)
            w_out = odd_w_out[i]
        h, h_bf = _matmul_res_ln(mix_in, w_out.astype(BF16), xf, ln_mix_g[layer], ln_mix_b[layer], 256)
        y2, gates = _moe_ffn(h, h_bf, router_w, router_b, moe_w_gate, moe_w_up, moe_w_down, layer, 256)
        xf, xb = _combine_ln(h, y2, gates, ln_ffn_g[layer], ln_ffn_b[layer], 256)
    return xf.reshape(b, s, d)
```

```python
import functools
import math

import jax
import jax.numpy as jnp
import numpy as np
from jax import lax
from jax.experimental import pallas as pl
from jax.experimental.pallas import tpu as pltpu

F32 = jnp.float32
BF16 = jnp.bfloat16

D_MODEL = 2048
DEPTH = 2
DEEPNORM_ALPHA = (2.0 * DEPTH) ** 0.25
LN_EPS = 1e-5

DIFF_HEADS = 8
DIFF_HEAD_DIM = 64
DIFF_WIDTH = DIFF_HEADS * 2 * DIFF_HEAD_DIM

SSM_INNER = D_MODEL // 2
SSM_HEAD_DIM = 64
SSM_HEADS = SSM_INNER // SSM_HEAD_DIM
SSM_GROUPS = 2
SSM_STATE = 128
SSM_CONV = 4
SSM_CHUNK = 128
SSM_CONV_CH = SSM_INNER + 2 * SSM_GROUPS * SSM_STATE

NSA_HEADS = 16
NSA_KV_GROUPS = 4
NSA_REP = NSA_HEADS // NSA_KV_GROUPS
NSA_HEAD_DIM = 128
NSA_CMP_LEN = 32
NSA_CMP_STRIDE = 16
NSA_SEL_LEN = 64
NSA_SEL_COUNT = 16
NSA_WINDOW = 512
NSA_Q_WIDTH = NSA_HEADS * NSA_HEAD_DIM
NSA_KV_WIDTH = NSA_KV_GROUPS * NSA_HEAD_DIM

MOE_GROUPS = 8
MOE_PER_GROUP = 4
MOE_EXPERTS = MOE_GROUPS * MOE_PER_GROUP
MOE_TOPK = 2
MOE_FF = 1024

NEG = -1e30
BIG = 1e30
LANES = 128
ONES_ROWS = 16
LOG2E = math.log2(math.e)
VMEM_LIMIT = 56 * 1024 * 1024


def _params(sem):
    return pltpu.CompilerParams(dimension_semantics=sem, vmem_limit_bytes=VMEM_LIMIT)


def _dot(a, b):
    return jnp.dot(a, b, preferred_element_type=F32)


def _dot_nt(a, b):
    return lax.dot_general(a, b, (((1,), (1,)), ((), ())), preferred_element_type=F32)


def _mm_kernel(x_ref, w_ref, o_ref):
    o_ref[...] = _dot(x_ref[...], w_ref[...]).astype(o_ref.dtype)


def _matmul(x, w, out_dtype, tm, tn):
    m, k = x.shape
    n = w.shape[1]
    return pl.pallas_call(
        _mm_kernel,
        grid=(n // tn, m // tm),
        in_specs=[pl.BlockSpec((tm, k), lambda j, i: (i, 0)),
                  pl.BlockSpec((k, tn), lambda j, i: (0, j))],
        out_specs=pl.BlockSpec((tm, tn), lambda j, i: (i, j)),
        out_shape=jax.ShapeDtypeStruct((m, n), out_dtype),
        compiler_params=_params(("parallel", "parallel")),
        name="proj_matmul",
    )(x, w)


def _layer_norm_rows(y, g, b):
    mu = jnp.mean(y, -1, keepdims=True)
    yc = y - mu
    var = jnp.mean(yc * yc, -1, keepdims=True)
    return yc * lax.rsqrt(var + LN_EPS) * g + b


def _mm_ln_kernel(a_ref, w_ref, res_ref, g_ref, b_ref, o_ref, ob_ref):
    y = _dot(a_ref[...], w_ref[...]) + DEEPNORM_ALPHA * res_ref[...]
    h = _layer_norm_rows(y, g_ref[...], b_ref[...])
    o_ref[...] = h
    ob_ref[...] = h.astype(BF16)


def _matmul_res_ln(a, w, res, g, b, tm):
    m, k = a.shape
    n = w.shape[1]
    return pl.pallas_call(
        _mm_ln_kernel,
        grid=(m // tm,),
        in_specs=[pl.BlockSpec((tm, k), lambda i: (i, 0)),
                  pl.BlockSpec((k, n), lambda i: (0, 0)),
                  pl.BlockSpec((tm, n), lambda i: (i, 0)),
                  pl.BlockSpec((1, n), lambda i: (0, 0)),
                  pl.BlockSpec((1, n), lambda i: (0, 0))],
        out_specs=[pl.BlockSpec((tm, n), lambda i: (i, 0)),
                   pl.BlockSpec((tm, n), lambda i: (i, 0))],
        out_shape=[jax.ShapeDtypeStruct((m, n), F32), jax.ShapeDtypeStruct((m, n), BF16)],
        compiler_params=_params(("parallel",)),
        name="outproj_ln",
    )(a, w, res, g.reshape(1, n), b.reshape(1, n))


def _combine_ln_kernel(h_ref, y_ref, w_ref, g_ref, b_ref, o_ref, ob_ref):
    w = w_ref[...]
    y = DEEPNORM_ALPHA * h_ref[...] + w[:, 0:1] * y_ref[0].astype(F32) + w[:, 1:2] * y_ref[1].astype(F32)
    x = _layer_norm_rows(y, g_ref[...], b_ref[...])
    o_ref[...] = x
    ob_ref[...] = x.astype(BF16)


def _combine_ln(h, y2, w, g, b, tm):
    m, n = h.shape
    row = pl.BlockSpec((tm, n), lambda i: (i, 0))
    vec = pl.BlockSpec((1, n), lambda i: (0, 0))
    return pl.pallas_call(
        _combine_ln_kernel,
        grid=(m // tm,),
        in_specs=[row, pl.BlockSpec((MOE_TOPK, tm, n), lambda i: (0, i, 0)),
                  pl.BlockSpec((tm, MOE_TOPK), lambda i: (i, 0)), vec, vec],
        out_specs=[row, row],
        out_shape=[jax.ShapeDtypeStruct((m, n), F32), jax.ShapeDtypeStruct((m, n), BF16)],
        compiler_params=_params(("parallel",)),
        name="ffn_combine_ln",
    )(h, y2, w, g.reshape(1, n), b.reshape(1, n))


def _softmax_tile(st_ref, pt_ref, m_ref, al_ref, *, mask=None):
    rows = st_ref.shape[1]
    for c in range(rows // LANES):
        cs = slice(c * LANES, (c + 1) * LANES)
        s = st_ref[:, cs]
        if mask is not None:
            s = mask(s, c)
        m_old = m_ref[:, cs]
        m_new = jnp.maximum(m_old, jnp.max(s, axis=0, keepdims=True))
        pt_ref[:, cs] = jnp.exp2(s - m_new[0:1]).astype(BF16)
        al_ref[:, cs] = jnp.exp2(m_old - m_new)
        m_ref[:, cs] = m_new


def _pv_update(vt_aug, pt_ref, al_ref, acc_ref):
    acc_ref[...] = acc_ref[...] * al_ref[0:1, :] + _dot(vt_aug, pt_ref[...])


def _flash_loop(n, scores, softmax, update):
    scores(0, 0)

    @pl.when(n == 1)
    def _():
        softmax(0, 0, True)
        update(0, 0)

    @pl.when(n >= 2)
    def _():
        scores(1, 1)
        softmax(0, 0, False)
        n_pairs = (n - 2) // 2

        def pair(u, carry):
            j = 2 * u
            scores(j + 2, 0)
            softmax(j + 1, 1, False)
            update(j, 0)
            scores(j + 3, 1)
            softmax(j + 2, 0, False)
            update(j + 1, 1)
            return carry

        lax.fori_loop(0, n_pairs, pair, 0)
        j = 2 * n_pairs

        @pl.when(n - 1 == j + 1)
        def _():
            softmax(j + 1, 1, True)
            update(j, 0)
            update(j + 1, 1)

        @pl.when(n - 1 == j + 2)
        def _():
            scores(j + 2, 0)
            softmax(j + 1, 1, False)
            update(j, 0)
            softmax(j + 2, 0, True)
            update(j + 1, 1)
            update(j + 2, 0)


def _transpose_values(v, tk):
    b, s, h, dv = v.shape
    vt = jnp.transpose(v.reshape(b, s // tk, tk, h, dv), (0, 3, 1, 4, 2))
    return jnp.concatenate([vt, jnp.ones((b, h, s // tk, ONES_ROWS, tk), v.dtype)], axis=3)


def _diff_kernel(lq1_ref, lk1_ref, lq2_ref, lk2_ref, g_ref, q_ref, k_ref, vt_ref, o_ref,
                 q2_ref, st0_ref, st1_ref, pt0_ref, pt1_ref, m_ref, al0_ref, al1_ref, acc_ref, *, tq, tk, lam_init):
    i = pl.program_id(2)
    st, pt, al = (st0_ref, st1_ref), (pt0_ref, pt1_ref), (al0_ref, al1_ref)
    qt = q_ref[...].astype(F32).T
    half = lax.broadcasted_iota(jnp.int32, (LANES, tq), 0) < DIFF_HEAD_DIM
    q2_ref[...] = jnp.concatenate([jnp.where(half, qt, 0.0), jnp.where(half, 0.0, qt)], axis=1).astype(BF16)
    m_ref[...] = jnp.full_like(m_ref, NEG)
    acc_ref[...] = jnp.zeros_like(acc_ref)

    def scores(j, slot):
        kt = k_ref[pl.ds(pl.multiple_of(j * tk, tk), tk), :]
        st[slot][...] = _dot(kt, q2_ref[...])

    def softmax(j, slot, masked):
        def causal(s, c):
            key = j * tk + lax.broadcasted_iota(jnp.int32, (tk, LANES), 0)
            qry = i * tq + ((c * LANES + lax.broadcasted_iota(jnp.int32, (tk, LANES), 1)) & (tq - 1))
            return jnp.where(key <= qry, s, NEG)
        _softmax_tile(st[slot], pt[slot], m_ref, al[slot], mask=causal if masked else None)

    def update(j, slot):
        _pv_update(vt_ref[j], pt[slot], al[slot], acc_ref)

    _flash_loop((i * tq + tq + tk - 1) // tk, scores, softmax, update)
    acc = acc_ref[...]
    o = (acc[:LANES] / acc[LANES:LANES + 1]).T
    lam = (jnp.exp(jnp.sum(lq1_ref[...] * lk1_ref[...], -1, keepdims=True))
           - jnp.exp(jnp.sum(lq2_ref[...] * lk2_ref[...], -1, keepdims=True)) + lam_init)
    od = o[:tq] - lam * o[tq:]
    y = od * lax.rsqrt(jnp.mean(od * od, -1, keepdims=True) + 1e-5) * g_ref[...]
    o_ref[...] = (y * (1.0 - lam_init)).astype(o_ref.dtype)


def _diff_attention(qkv, lq1, lk1, lq2, lk2, subln_g, lam_init, tq, tk):
    b, s, _ = qkv.shape
    rows = 2 * tq
    dv = 2 * DIFF_HEAD_DIM
    assert tk % tq == 0 and s % tk == 0
    vt = _transpose_values(qkv[..., 2 * DIFF_WIDTH:].reshape(b, s, DIFF_HEADS, dv), tk)
    vec64 = pl.BlockSpec((1, DIFF_HEAD_DIM), lambda bb, h, i: (0, 0))
    return pl.pallas_call(
        functools.partial(_diff_kernel, tq=tq, tk=tk, lam_init=lam_init),
        grid=(b, DIFF_HEADS, s // tq),
        in_specs=[vec64, vec64, vec64, vec64,
                  pl.BlockSpec((1, LANES), lambda bb, h, i: (0, 0)),
                  pl.BlockSpec((None, tq, LANES), lambda bb, h, i: (bb, i, h)),
                  pl.BlockSpec((None, s, LANES), lambda bb, h, i: (bb, 0, DIFF_HEADS + h)),
                  pl.BlockSpec((None, None, s // tk, dv + ONES_ROWS, tk), lambda bb, h, i: (bb, h, 0, 0, 0))],
        out_specs=pl.BlockSpec((None, tq, LANES), lambda bb, h, i: (bb, i, h)),
        out_shape=jax.ShapeDtypeStruct((b, s, DIFF_WIDTH), BF16),
        scratch_shapes=[pltpu.VMEM((LANES, rows), BF16), pltpu.VMEM((tk, rows), F32), pltpu.VMEM((tk, rows), F32),
                        pltpu.VMEM((tk, rows), BF16), pltpu.VMEM((tk, rows), BF16),
                        pltpu.VMEM((8, rows), F32), pltpu.VMEM((8, rows), F32), pltpu.VMEM((8, rows), F32),
                        pltpu.VMEM((dv + ONES_ROWS, rows), F32)],
        compiler_params=_params(("parallel", "parallel", "arbitrary")),
        name="diff_attention",
    )(lq1.reshape(1, -1), lk1.reshape(1, -1), lq2.reshape(1, -1), lk2.reshape(1, -1),
      subln_g.reshape(1, -1), qkv, qkv, vt)


def _ssd_kernel(pf_ref, cw_ref, cb_ref, dtb_ref, alog_ref, drep_ref, ng_ref, tri_ref, o_ref,
                xe_ref, tail_ref, state_ref, y_ref):
    L = SSM_CHUNK
    halo = 8

    @pl.when(pl.program_id(1) == 0)
    def _():
        tail_ref[...] = jnp.zeros_like(tail_ref)
        state_ref[...] = jnp.zeros_like(state_ref)

    z = pf_ref[:, 0:SSM_INNER]
    xbc = pf_ref[:, SSM_INNER:SSM_INNER + SSM_CONV_CH]
    dt_raw = pf_ref[:, SSM_INNER + SSM_CONV_CH:SSM_INNER + SSM_CONV_CH + LANES]
    xe_ref[0:halo, :] = tail_ref[...]
    xe_ref[halo:halo + L, :] = xbc
    tail_ref[...] = xbc[L - halo:L, :]
    conv = cb_ref[...] + cw_ref[SSM_CONV - 1:SSM_CONV, :] * xbc
    for k in range(SSM_CONV - 1):
        conv = conv + cw_ref[k:k + 1, :] * xe_ref[pl.ds(halo - (SSM_CONV - 1) + k, L), :]
    xbc = conv * jax.nn.sigmoid(conv)
    xs = xbc[:, :SSM_INNER]
    dt = jax.nn.softplus(dt_raw + dtb_ref[...])
    adt = dt * (-jnp.exp(alog_ref[...]))
    acum = jnp.dot(tri_ref[...], adt, preferred_element_type=F32, precision=lax.Precision.HIGHEST)
    acum_t = acum.T
    left = lax.broadcasted_iota(jnp.int32, (1, LANES), 1) < SSM_HEAD_DIM
    tril = lax.broadcasted_iota(jnp.int32, (L, L), 0) >= lax.broadcasted_iota(jnp.int32, (L, L), 1)
    pairs_per_group = SSM_HEADS // SSM_GROUPS // 2
    for g in range(SSM_GROUPS):
        bm = xbc[:, SSM_INNER + g * SSM_STATE:SSM_INNER + (g + 1) * SSM_STATE]
        cm = xbc[:, SSM_INNER + (SSM_GROUPS + g) * SSM_STATE:SSM_INNER + (SSM_GROUPS + g + 1) * SSM_STATE]
        cb = _dot_nt(cm.astype(BF16), bm.astype(BF16))
        bm_t = bm.T
        for qq in range(pairs_per_group):
            q = g * pairs_per_group + qq
            ls = slice(q * LANES, (q + 1) * LANES)
            xs_p = xs[:, ls]
            h0, h1 = 2 * q, 2 * q + 1
            xd = xs_p * jnp.where(left, dt[:, h0:h0 + 1], dt[:, h1:h1 + 1])
            prev = state_ref[q]
            y = drep_ref[:, ls] * xs_p
            new = jnp.zeros((SSM_STATE, LANES), F32)
            for h, keep in ((h0, left), (h1, jnp.logical_not(left))):
                xd_h = jnp.where(keep, xd, 0.0).astype(BF16)
                prev_h = jnp.where(keep, prev, 0.0).astype(BF16)
                cs_col = acum[:, h:h + 1]
                cs_row = acum_t[h:h + 1, :]
                w = cb * jnp.exp(jnp.where(tril, cs_col - cs_row, -jnp.inf))
                y = y + _dot(w.astype(BF16), xd_h) + _dot((cm * jnp.exp(cs_col)).astype(BF16), prev_h)
                new = new + _dot((bm_t * jnp.exp(acum[L - 1:L, h:h + 1] - cs_row)).astype(BF16), xd_h)
            decay = jnp.where(left, jnp.exp(acum[L - 1:L, h0:h0 + 1]), jnp.exp(acum[L - 1:L, h1:h1 + 1]))
            state_ref[q] = prev * decay + new
            y_ref[:, ls] = y
    y = y_ref[...] * (z * jax.nn.sigmoid(z))
    gw = SSM_INNER // SSM_GROUPS
    for g in range(SSM_GROUPS):
        yg = y[:, g * gw:(g + 1) * gw]
        yg = yg * lax.rsqrt(jnp.mean(yg * yg, -1, keepdims=True) + 1e-5) * ng_ref[:, g * gw:(g + 1) * gw]
        o_ref[:, g * gw:(g + 1) * gw] = yg.astype(o_ref.dtype)


def _mamba2_mixer(pf, conv_w, conv_b, dt_bias, a_log, d_skip, norm_g):
    b, s, w = pf.shape
    L = SSM_CHUNK
    pad_heads = lambda v: jnp.pad(v, (0, LANES - SSM_HEADS)).reshape(1, LANES)
    tri = jnp.tril(jnp.ones((L, L), F32))
    full = lambda shape: pl.BlockSpec(shape, lambda bb, c: (0,) * len(shape))
    return pl.pallas_call(
        _ssd_kernel,
        grid=(b, s // L),
        in_specs=[pl.BlockSpec((None, L, w), lambda bb, c: (bb, c, 0)),
                  full((SSM_CONV, SSM_CONV_CH)), full((1, SSM_CONV_CH)), full((1, LANES)), full((1, LANES)),
                  full((1, SSM_INNER)), full((1, SSM_INNER)), full((L, L))],
        out_specs=pl.BlockSpec((None, L, SSM_INNER), lambda bb, c: (bb, c, 0)),
        out_shape=jax.ShapeDtypeStruct((b, s, SSM_INNER), BF16),
        scratch_shapes=[pltpu.VMEM((L + 8, SSM_CONV_CH), F32), pltpu.VMEM((8, SSM_CONV_CH), F32),
                        pltpu.VMEM((SSM_HEADS // 2, SSM_STATE, LANES), F32), pltpu.VMEM((L, SSM_INNER), F32)],
        compiler_params=_params(("parallel", "arbitrary")),
        name="ssd_mixer",
    )(pf, conv_w, conv_b.reshape(1, -1), pad_heads(dt_bias), pad_heads(a_log),
      jnp.repeat(d_skip, SSM_HEAD_DIM).reshape(1, -1), norm_g.reshape(1, -1), tri)


def _cmp_kernel(x_ref, pek_ref, pev_ref, w1k_ref, w1v_ref, w2k_ref, w2v_ref, ko_ref, vo_ref,
                top_ref, bot_ref, *, nb):
    l = pl.program_id(1)
    g4 = NSA_KV_GROUPS

    @pl.when(l == 0)
    def _():
        top_ref[...] = jnp.zeros_like(top_ref)
        bot_ref[...] = jnp.zeros_like(bot_ref)

    x = x_ref[...]
    for kv, (pe_ref, w1_ref) in enumerate(((pek_ref, w1k_ref), (pev_ref, w1v_ref))):
        x4 = jnp.concatenate([x[:, kv * NSA_KV_WIDTH + g * LANES: kv * NSA_KV_WIDTH + (g + 1) * LANES]
                              for g in range(g4)], axis=0)
        top_ref[kv] += _dot((x4 + pe_ref[pl.ds(l, 1), :]).astype(BF16), w1_ref[l])
        bot_ref[kv] += _dot((x4 + pe_ref[pl.ds(l + NSA_CMP_STRIDE, 1), :]).astype(BF16),
                            w1_ref[l + NSA_CMP_STRIDE])

    @pl.when(l == NSA_CMP_STRIDE - 1)
    def _():
        last = lax.broadcasted_iota(jnp.int32, (nb, LANES), 0) == nb - 1
        for kv, (w2_ref, o_ref) in enumerate(((w2k_ref, ko_ref), (w2v_ref, vo_ref))):
            for g in range(g4):
                top = top_ref[kv, g * nb:(g + 1) * nb, :]
                bot = bot_ref[kv, g * nb:(g + 1) * nb, :]
                hid = top + pltpu.roll(bot, nb - 1, 0)
                out = _dot(jax.nn.gelu(hid, approximate=True).astype(BF16), w2_ref[...])
                o_ref[g] = jnp.where(last, 0.0, out).astype(o_ref.dtype)


def _compress(bf, pe_k, pe_v, w1k, w1v, w2k, w2v):
    b, s, w = bf.shape
    nb = s // NSA_CMP_STRIDE
    xv = bf.reshape(b, nb, NSA_CMP_STRIDE * w)
    full = lambda shape: pl.BlockSpec(shape, lambda bb, l: (0,) * len(shape))
    out_spec = pl.BlockSpec((None, NSA_KV_GROUPS, nb, LANES), lambda bb, l: (bb, 0, 0, 0))
    out_sd = jax.ShapeDtypeStruct((b, NSA_KV_GROUPS, nb, LANES), BF16)
    w1 = lambda a: a.astype(BF16).reshape(NSA_CMP_LEN, NSA_HEAD_DIM, -1)
    return pl.pallas_call(
        functools.partial(_cmp_kernel, nb=nb),
        grid=(b, NSA_CMP_STRIDE),
        in_specs=[pl.BlockSpec((None, nb, w), lambda bb, l: (bb, 0, l)),
                  full((NSA_CMP_LEN, LANES)), full((NSA_CMP_LEN, LANES)),
                  full((NSA_CMP_LEN, NSA_HEAD_DIM, LANES)), full((NSA_CMP_LEN, NSA_HEAD_DIM, LANES)),
                  full((LANES, LANES)), full((LANES, LANES))],
        out_specs=[out_spec, out_spec],
        out_shape=[out_sd, out_sd],
        scratch_shapes=[pltpu.VMEM((2, NSA_KV_GROUPS * nb, LANES), F32),
                        pltpu.VMEM((2, NSA_KV_GROUPS * nb, LANES), F32)],
        compiler_params=_params(("parallel", "arbitrary")),
        name="nsa_compress",
    )(xv, pe_k, pe_v, w1(w1k), w1(w1v), w2k.astype(BF16), w2v.astype(BF16))


def _nsa_kernel(q_ref, gate_ref, kc_ref, vct_ref, ks_ref, vst_ref, kw_ref, vwt_ref, blk_ref, ovt_ref, o_ref,
                qa_ref, sc_ref, pc_ref, ps_ref, oc_ref, imp_ref, st0_ref, st1_ref, pt0_ref, pt1_ref, sw_ref, pw_ref,
                m_ref, al0_ref, al1_ref, acc_ref, mw_ref, alw_ref, accw_ref, *, tk, seq, slab):
    i = pl.program_id(2)
    tq = LANES
    t0 = i * tq
    rep = NSA_REP
    ncp = seq // NSA_CMP_STRIDE
    q = q_ref[...].astype(F32)
    for r in range(rep):
        qa_ref[0:LANES, r * tq:(r + 1) * tq] = q[:, r * LANES:(r + 1) * LANES].T.astype(BF16)
    qs = qa_ref[0:LANES, :]
    t_lane = t0 + lax.broadcasted_iota(jnp.int32, (1, LANES), 1)

    def compressed(nk):
        sc_ref[0:nk, :] = _dot(kc_ref[0:nk, :], qs)
        n_end = lax.broadcasted_iota(jnp.int32, (nk, LANES), 0) * NSA_CMP_STRIDE + (NSA_CMP_LEN - 1)
        cmask = n_end <= t_lane
        any_visible = t_lane >= NSA_CMP_LEN - 1
        for r in range(rep):
            cs = slice(r * tq, (r + 1) * tq)
            s = jnp.where(cmask, sc_ref[0:nk, cs], NEG)
            e = jnp.exp2(s - jnp.max(s, axis=0, keepdims=True))
            p = e * jnp.where(any_visible, 1.0 / jnp.sum(e, axis=0, keepdims=True), 0.0)
            pc_ref[0:nk, cs] = p.astype(BF16)
            ps_ref[0:nk, :] = p if r == 0 else ps_ref[0:nk, :] + p
        oc_ref[...] = _dot(vct_ref[:, 0:nk], pc_ref[0:nk, :])
        ps = ps_ref[0:nk, :]
        p_hi = ps.astype(BF16)
        p_lo = (ps - p_hi.astype(F32)).astype(BF16)
        imp_ref[...] = _dot(ovt_ref[:, 0:nk], p_hi) + _dot(ovt_ref[:, 0:nk], p_lo)

    n_visible = t0 // NSA_CMP_STRIDE + (tq - NSA_CMP_LEN) // NSA_CMP_STRIDE + 1
    for v in range(ncp // LANES):
        @pl.when((n_visible - 1) // LANES == v)
        def _():
            compressed((v + 1) * LANES)
    o_cmp_t = oc_ref[...]
    imp_t = imp_ref[...]
    cur = t_lane >> 6
    mi = lax.broadcasted_iota(jnp.int32, (LANES, LANES), 0)
    forced = (mi == 0) | (mi == cur) | (mi == cur - 1)
    vals = jnp.where(forced, -jnp.inf, jnp.where(mi <= cur, imp_t, NEG))
    pen = jnp.where(forced, 0.0, -1e9)
    mf = mi.astype(F32)
    for _ in range(NSA_SEL_COUNT - 3):
        mx = jnp.max(vals, axis=0, keepdims=True)
        first = jnp.min(jnp.where(vals == mx, mf, float(LANES)), axis=0, keepdims=True)
        hit = mf == first
        pen = jnp.where(hit, 0.0, pen)
        vals = jnp.where(hit, -jnp.inf, vals)
    pen_b = pen.astype(BF16)
    for r in range(rep):
        qa_ref[LANES:2 * LANES, r * tq:(r + 1) * tq] = pen_b

    ws = pl.multiple_of(jnp.clip(t0 - NSA_WINDOW, 0, seq - slab), LANES)
    sw_ref[...] = _dot(kw_ref[pl.ds(ws, slab), :], qs)
    mw_ref[...] = jnp.full_like(mw_ref, NEG)
    accw_ref[...] = jnp.zeros_like(accw_ref)
    jb = ws // LANES
    v_slab = jnp.concatenate([vwt_ref[jb + u] for u in range(slab // LANES)], axis=1)

    def window(s, c):
        kpos = ws + lax.broadcasted_iota(jnp.int32, (slab, LANES), 0)
        return jnp.where(kpos <= t_lane, jnp.where(kpos > t_lane - NSA_WINDOW, s, NEG), NEG)

    _softmax_tile(sw_ref, pw_ref, mw_ref, alw_ref, mask=window)
    _pv_update(v_slab, pw_ref, alw_ref, accw_ref)
    accw = accw_ref[...]
    o_win_t = accw[:LANES] / accw[LANES:LANES + 1]

    m_ref[...] = jnp.full_like(m_ref, NEG)
    acc_ref[...] = jnp.zeros_like(acc_ref)
    st, pt, al = (st0_ref, st1_ref), (pt0_ref, pt1_ref), (al0_ref, al1_ref)

    def scores(j, slot):
        k0 = pl.multiple_of(j * tk, tk)
        ka = jnp.concatenate([ks_ref[pl.ds(k0, tk), :], blk_ref[pl.ds(k0, tk), :]], axis=1)
        st[slot][...] = _dot(ka, qa_ref[...])

    def softmax(j, slot, masked):
        def causal(s, c):
            kpos = j * tk + lax.broadcasted_iota(jnp.int32, (tk, LANES), 0)
            return jnp.where(kpos <= t_lane, s, NEG)
        _softmax_tile(st[slot], pt[slot], m_ref, al[slot], mask=causal if masked else None)

    def update(j, slot):
        _pv_update(vst_ref[j], pt[slot], al[slot], acc_ref)

    _flash_loop((t0 + tq + tk - 1) // tk, scores, softmax, update)
    acc = acc_ref[...]
    o_sel_t = acc[:LANES] / acc[LANES:LANES + 1]

    g_t = jax.nn.sigmoid(gate_ref[...]).T
    for r in range(rep):
        cs = slice(r * tq, (r + 1) * tq)
        out_t = (g_t[3 * r:3 * r + 1] * o_cmp_t[:, cs] + g_t[3 * r + 1:3 * r + 2] * o_sel_t[:, cs]
                 + g_t[3 * r + 2:3 * r + 3] * o_win_t[:, cs])
        o_ref[:, r * LANES:(r + 1) * LANES] = out_t.T.astype(o_ref.dtype)


def _nsa_attention(a_bf, b_f32, k_cmp, v_cmp, tk):
    b, s, _ = a_bf.shape
    g4 = NSA_KV_GROUPS
    tq = LANES
    rows = NSA_REP * tq
    ncp = s // NSA_CMP_STRIDE
    slab = NSA_WINDOW + tq
    dva = NSA_HEAD_DIM + ONES_ROWS
    assert s >= slab and s % tk == 0 and tk % tq == 0
    blk = (jnp.arange(s)[:, None] // NSA_SEL_LEN == jnp.arange(LANES)[None, :]).astype(BF16)
    cs = np.arange(ncp)[:, None] * NSA_CMP_STRIDE
    ss = np.arange(LANES)[None, :] * NSA_SEL_LEN
    ov = np.maximum(np.minimum(cs + NSA_CMP_LEN, ss + NSA_SEL_LEN) - np.maximum(cs, ss), 0) / NSA_CMP_LEN
    ov[ncp - 1] = 0.0
    qoff = NSA_Q_WIDTH // LANES
    group_values = lambda n: a_bf[..., NSA_Q_WIDTH + n * NSA_KV_WIDTH:NSA_Q_WIDTH + (n + 1) * NSA_KV_WIDTH].reshape(
        b, s, g4, NSA_HEAD_DIM)
    vst = _transpose_values(group_values(1), tk)
    vwt = _transpose_values(group_values(3), LANES)
    vct = jnp.swapaxes(v_cmp, 2, 3)
    k_spec = lambda n: pl.BlockSpec((None, s, LANES), lambda bb, g, i: (bb, 0, qoff + n * g4 + g))
    vt_spec = lambda t: pl.BlockSpec((None, None, s // t, dva, t), lambda bb, g, i: (bb, g, 0, 0, 0))
    return pl.pallas_call(
        functools.partial(_nsa_kernel, tk=tk, seq=s, slab=slab),
        grid=(b, g4, s // tq),
        in_specs=[pl.BlockSpec((None, tq, NSA_REP * LANES), lambda bb, g, i: (bb, i, g)),
                  pl.BlockSpec((None, tq, LANES), lambda bb, g, i: (bb, i, 2 * g4 + g)),
                  pl.BlockSpec((None, None, ncp, LANES), lambda bb, g, i: (bb, g, 0, 0)),
                  pl.BlockSpec((None, None, LANES, ncp), lambda bb, g, i: (bb, g, 0, 0)),
                  k_spec(0), vt_spec(tk), k_spec(2), vt_spec(LANES),
                  pl.BlockSpec((s, LANES), lambda bb, g, i: (0, 0)),
                  pl.BlockSpec((LANES, ncp), lambda bb, g, i: (0, 0))],
        out_specs=pl.BlockSpec((None, tq, NSA_REP * LANES), lambda bb, g, i: (bb, i, g)),
        out_shape=jax.ShapeDtypeStruct((b, s, NSA_Q_WIDTH), BF16),
        scratch_shapes=[pltpu.VMEM((2 * LANES, rows), BF16),
                        pltpu.VMEM((ncp, rows), F32), pltpu.VMEM((ncp, rows), BF16), pltpu.VMEM((ncp, LANES), F32),
                        pltpu.VMEM((NSA_HEAD_DIM, rows), F32), pltpu.VMEM((LANES, LANES), F32),
                        pltpu.VMEM((tk, rows), F32), pltpu.VMEM((tk, rows), F32),
                        pltpu.VMEM((tk, rows), BF16), pltpu.VMEM((tk, rows), BF16),
                        pltpu.VMEM((slab, rows), F32), pltpu.VMEM((slab, rows), BF16),
                        pltpu.VMEM((8, rows), F32), pltpu.VMEM((8, rows), F32), pltpu.VMEM((8, rows), F32),
                        pltpu.VMEM((dva, rows), F32),
                        pltpu.VMEM((8, rows), F32), pltpu.VMEM((8, rows), F32), pltpu.VMEM((dva, rows), F32)],
        compiler_params=_params(("parallel", "parallel", "arbitrary")),
        name="nsa_attention",
    )(a_bf, b_f32, k_cmp, vct, a_bf, vst, a_bf, vwt, blk, jnp.asarray(ov.T, BF16))


def _moe_kernel(be_ref, hs_ref, first_ref, nu_ref, x_ref, wg_ref, wu_ref, wd_ref, o_ref,
                wgb_ref, wub_ref, wdb_ref, part_ref):
    i, j = pl.program_id(0), pl.program_id(1)

    @pl.when(i < nu_ref[0])
    def _():
        @pl.when(first_ref[i] == 1)
        def _():
            wgb_ref[j] = wg_ref[...].astype(BF16)
            wub_ref[j] = wu_ref[...].astype(BF16)
            wdb_ref[j] = wd_ref[...].astype(BF16)

        x = x_ref[...]
        hid = jax.nn.silu(_dot(x, wgb_ref[j])) * _dot(x, wub_ref[j])
        part = _dot(hid.astype(BF16), wdb_ref[j])

        @pl.when(j == 0)
        def _():
            part_ref[...] = part

        @pl.when(j == 1)
        def _():
            o_ref[...] = (part_ref[...] + part).astype(o_ref.dtype)

    @pl.when(i >= nu_ref[0])
    def _():
        o_ref[...] = jnp.zeros_like(o_ref)


def _moe_blocks(block_e, n_used, xg, wg, wu, wd, layer, tm):
    rows, d = xg.shape
    ff = wg.shape[-1]
    ffh = ff // 2
    n_blocks = rows // tm
    blk = jnp.arange(n_blocks, dtype=jnp.int32)
    prev_e = jnp.concatenate([jnp.full((1,), -1, jnp.int32), block_e[:-1]])
    first = ((block_e != prev_e) & (blk < n_used[0])).astype(jnp.int32)
    half = jnp.where(first[:, None] == 1, jnp.arange(2, dtype=jnp.int32)[None, :], 1).reshape(-1)
    grid_spec = pltpu.PrefetchScalarGridSpec(
        num_scalar_prefetch=4,
        grid=(n_blocks, 2),
        in_specs=[pl.BlockSpec((tm, d), lambda i, j, be, hs, fi, nu: (i, 0)),
                  pl.BlockSpec((None, None, d, ffh), lambda i, j, be, hs, fi, nu: (layer, be[i], 0, hs[2 * i + j])),
                  pl.BlockSpec((None, None, d, ffh), lambda i, j, be, hs, fi, nu: (layer, be[i], 0, hs[2 * i + j])),
                  pl.BlockSpec((None, None, ffh, d), lambda i, j, be, hs, fi, nu: (layer, be[i], hs[2 * i + j], 0))],
        out_specs=pl.BlockSpec((tm, d), lambda i, j, be, hs, fi, nu: (i, 0)),
        scratch_shapes=[pltpu.VMEM((2, d, ffh), BF16), pltpu.VMEM((2, d, ffh), BF16), pltpu.VMEM((2, ffh, d), BF16),
                        pltpu.VMEM((tm, d), F32)],
    )
    return pl.pallas_call(
        _moe_kernel,
        grid_spec=grid_spec,
        out_shape=jax.ShapeDtypeStruct((rows, d), BF16),
        compiler_params=_params(("arbitrary", "arbitrary")),
        name="moe_experts",
    )(block_e, half, first, n_used, xg, wg, wu, wd)


def _router_kernel(h_ref, wt_ref, b_ref, tri_ref, e_ref, g_ref, r_ref, cnt_ref, sig_ref, bia_ref, carry_ref):
    @pl.when(pl.program_id(0) == 0)
    def _():
        carry_ref[...] = jnp.zeros_like(carry_ref)

    tm = h_ref.shape[0]
    logits = lax.dot_general(wt_ref[...], h_ref[...], (((1,), (1,)), ((), ())),
                             preferred_element_type=F32, precision=lax.Precision.HIGHEST)
    sig = jax.nn.sigmoid(logits)
    sig_ref[...] = sig
    biased = sig + b_ref[...]
    n_chunks = tm // LANES
    for c in range(n_chunks):
        bia_ref[c] = biased[:, c * LANES:(c + 1) * LANES]
    cand = [jnp.concatenate([bia_ref[c, pl.ds(k, MOE_GROUPS, stride=MOE_PER_GROUP), :] for c in range(n_chunks)],
                            axis=1) for k in range(MOE_PER_GROUP)]

    def top_of(vals):
        best = functools.reduce(jnp.maximum, vals)
        idx = jnp.full(best.shape, MOE_PER_GROUP - 1, jnp.int32)
        for k in range(MOE_PER_GROUP - 2, -1, -1):
            idx = jnp.where(vals[k] == best, k, idx)
        return best, idx

    top1, idx1 = top_of(cand)
    top2, idx2 = top_of([jnp.where(idx1 == k, -jnp.inf, cand[k]) for k in range(MOE_PER_GROUP)])
    gsum = top1 + top2
    gi = lax.broadcasted_iota(jnp.int32, gsum.shape, 0)
    grp = jnp.min(jnp.where(gsum == jnp.max(gsum, axis=0, keepdims=True), gi, MOE_GROUPS), axis=0, keepdims=True)
    chosen = gi == grp
    e1 = grp * MOE_PER_GROUP + jnp.sum(jnp.where(chosen, idx1, 0), axis=0, keepdims=True)
    e2 = grp * MOE_PER_GROUP + jnp.sum(jnp.where(chosen, idx2, 0), axis=0, keepdims=True)
    ei = lax.broadcasted_iota(jnp.int32, (MOE_EXPERTS, tm), 0)
    oh1, oh2 = ei == e1, ei == e2
    sig = sig_ref[...]
    w1 = jnp.sum(jnp.where(oh1, sig, 0.0), axis=0, keepdims=True)
    w2 = jnp.sum(jnp.where(oh2, sig, 0.0), axis=0, keepdims=True)
    e_ref[...] = jnp.concatenate([e1, e2], axis=0)
    g_ref[...] = jnp.concatenate([w1, w2], axis=0) / (w1 + w2)
    both = jnp.where(oh1, 1.0, jnp.where(oh2, 1.0, 0.0))
    base = carry_ref[:, 0:1] + _dot(both.astype(BF16), tri_ref[...])
    r1 = jnp.sum(jnp.where(oh1, base, 0.0), axis=0, keepdims=True)
    r2 = jnp.sum(jnp.where(oh2, base, 0.0), axis=0, keepdims=True)
    r_ref[...] = jnp.concatenate([r1, r2], axis=0).astype(jnp.int32)
    carry_ref[...] += jnp.sum(both, axis=1, keepdims=True)
    cnt_ref[...] = carry_ref[...]


def _route(h, router_w, router_b, tm):
    t, d = h.shape
    e = router_w.shape[1]
    tri = (jnp.arange(tm)[:, None] < jnp.arange(tm)[None, :]).astype(BF16)
    tok = lambda dt: (pl.BlockSpec((MOE_TOPK, tm), lambda i: (0, i)), jax.ShapeDtypeStruct((MOE_TOPK, t), dt))
    (es, esd), (gs, gsd), (rs, rsd) = tok(jnp.int32), tok(F32), tok(jnp.int32)
    return pl.pallas_call(
        _router_kernel,
        grid=(t // tm,),
        in_specs=[pl.BlockSpec((tm, d), lambda i: (i, 0)), pl.BlockSpec((e, d), lambda i: (0, 0)),
                  pl.BlockSpec((e, 1), lambda i: (0, 0)), pl.BlockSpec((tm, tm), lambda i: (0, 0))],
        out_specs=[es, gs, rs, pl.BlockSpec((e, LANES), lambda i: (0, 0))],
        out_shape=[esd, gsd, rsd, jax.ShapeDtypeStruct((e, LANES), F32)],
        scratch_shapes=[pltpu.VMEM((e, tm), F32), pltpu.VMEM((tm // LANES, e, LANES), F32),
                        pltpu.VMEM((e, LANES), F32)],
        compiler_params=_params(("arbitrary",)),
        name="router",
    )(h, router_w.T, router_b.reshape(e, 1), tri)


def _moe_ffn(h, h_bf, router_w, router_b, wg, wu, wd, layer, tm):
    t, d = h.shape
    expert, gate, rank, counts = _route(h, router_w, router_b, min(512, t))
    n_blocks = t * MOE_TOPK // tm + MOE_EXPERTS
    rows = n_blocks * tm
    counts = counts[:, 0].astype(jnp.int32)
    padded = (counts + tm - 1) // tm * tm
    pad_end = jnp.cumsum(padded)
    onehot = expert[..., None] == jnp.arange(MOE_EXPERTS, dtype=jnp.int32)
    pos = rank + jnp.sum(jnp.where(onehot, pad_end - padded, 0), axis=-1)
    tok = jnp.broadcast_to(jnp.arange(t, dtype=jnp.int32), (MOE_TOPK, t))
    buf_tok = jnp.full((rows,), t, jnp.int32).at[pos.reshape(-1)].set(tok.reshape(-1), unique_indices=True)
    block_start = jnp.arange(n_blocks, dtype=jnp.int32) * tm
    block_e = jnp.minimum(jnp.sum((pad_end[None, :] <= block_start[:, None]).astype(jnp.int32), axis=1),
                          MOE_EXPERTS - 1)
    n_used = (pad_end[-1:] // tm).astype(jnp.int32)
    xg = jnp.concatenate([h_bf, jnp.zeros((1, d), BF16)], axis=0)[buf_tok]
    yb = _moe_blocks(block_e, n_used, xg, wg, wu, wd, layer, tm)
    return yb[pos], gate.T


def _pad_cols(w, n):
    return jnp.pad(w, ((0, 0), (0, n - w.shape[1])))


def kernel(x, router_w, router_b, even_w_in, even_w_out, diff_lam_q1, diff_lam_k1, diff_lam_q2, diff_lam_k2,
           diff_subln_g, ssm_conv_w, ssm_conv_b, ssm_dt_bias, ssm_a_log, ssm_d, ssm_norm_g, odd_w_in,
           odd_w_out, nsa_pe_k, nsa_pe_v, nsa_ck_w1, nsa_ck_w2, nsa_cv_w1, nsa_cv_w2, ln_mix_g, ln_mix_b,
           ln_ffn_g, ln_ffn_b, moe_w_gate, moe_w_up, moe_w_down):
    b, s, d = x.shape
    t = b * s
    xf = x.reshape(t, d)
    xb = xf.astype(BF16)
    for layer in range(DEPTH):
        i = layer // 2
        if layer % 2 == 0:
            lam_init = 0.8 - 0.6 * math.exp(-0.3 * layer)
            w_in = even_w_in[i]
            n_attn = 3 * DIFF_WIDTH
            col_scale = jnp.where(jnp.arange(n_attn) < DIFF_WIDTH, DIFF_HEAD_DIM ** -0.5 * LOG2E, 1.0)
            qkv = _matmul(xb, (w_in[:, :n_attn] * col_scale).astype(BF16), BF16, 1024, 1024).reshape(b, s, n_attn)
            n_ssm = -(-(w_in.shape[1] - n_attn) // LANES) * LANES
            pf = _matmul(xb, _pad_cols(w_in[:, n_attn:], n_ssm).astype(BF16), F32, 1024, n_ssm // 3)
            pf = pf.reshape(b, s, n_ssm)
            y_attn = _diff_attention(qkv, diff_lam_q1[i], diff_lam_k1[i], diff_lam_q2[i], diff_lam_k2[i],
                                     diff_subln_g[i], lam_init, 256, 256)
            y_ssm = _mamba2_mixer(pf, ssm_conv_w[i], ssm_conv_b[i], ssm_dt_bias[i], ssm_a_log[i], ssm_d[i],
                                  ssm_norm_g[i])
            mix_in = jnp.concatenate([y_attn, y_ssm], axis=-1).reshape(t, -1)
            w_out = even_w_out[i]
        else:
            w_in = odd_w_in[i]
            n_a = NSA_Q_WIDTH + 4 * NSA_KV_WIDTH
            wq = w_in[:, :NSA_Q_WIDTH] * (NSA_HEAD_DIM ** -0.5 * LOG2E)
            wkv =w_in[:, NSA_Q_WIDTH:NSA_Q_WIDTH + 6 * NSA_KV_WIDTH]
            wgate = w_in[:, NSA_Q_WIDTH + 6 * NSA_KV_WIDTH:]
            w_a = jnp.concatenate([wq, wkv[:, 2 * NSA_KV_WIDTH:]], axis=1)
            wgate = jnp.pad(wgate.reshape(d, NSA_KV_GROUPS, 3 * NSA_REP),
                            ((0, 0), (0, 0), (0, LANES - 3 * NSA_REP))).reshape(d, NSA_KV_GROUPS * LANES)
            w_b = jnp.concatenate([wkv[:, :2 * NSA_KV_WIDTH], wgate], axis=1)
            a_bf = _matmul(xb, w_a.astype(BF16), BF16, 1024, 1024).reshape(b, s, n_a)
            b_f32 = _matmul(xb, w_b.astype(BF16), F32, 1024, 512).reshape(b, s, -1)
            k_cmp, v_cmp = _compress(b_f32, nsa_pe_k[i], nsa_pe_v[i], nsa_ck_w1[i], nsa_cv_w1[i],
                                     nsa_ck_w2[i], nsa_cv_w2[i])
            mix_in = _nsa_attention(a_bf, b_f32, k_cmp, v_cmp, 512).reshape(t, -1)
            w_out = odd_w_out[i]
        h, h_bf = _matmul_res_ln(mix_in, w_out.astype(BF16), xf, ln_mix_g[layer], ln_mix_b[layer], 256)
        y2, gates = _moe_ffn(h, h_bf, router_w, router_b, moe_w_gate, moe_w_up, moe_w_down, layer, 256)
        xf, xb = _combine_ln(h, y2, gates, ln_ffn_g[layer], ln_ffn_b[layer], 256)
    return xf.reshape(b, s, d)
```

```python
import functools
import math

import jax
import jax.numpy as jnp
import numpy as np
from jax import lax
from jax.experimental import pallas as pl
from jax.experimental.pallas import tpu as pltpu

F32 = jnp.float32
BF16 = jnp.bfloat16

D_MODEL = 2048
DEPTH = 2
DEEPNORM_ALPHA = (2.0 * DEPTH) ** 0.25
LN_EPS = 1e-5

DIFF_HEADS = 8
DIFF_HEAD_DIM = 64
DIFF_WIDTH = DIFF_HEADS * 2 * DIFF_HEAD_DIM

SSM_INNER = D_MODEL // 2
SSM_HEAD_DIM = 64
SSM_HEADS = SSM_INNER // SSM_HEAD_DIM
SSM_GROUPS = 2
SSM_STATE = 128
SSM_CONV = 4
SSM_CHUNK = 128
SSM_CONV_CH = SSM_INNER + 2 * SSM_GROUPS * SSM_STATE

NSA_HEADS = 16
NSA_KV_GROUPS = 4
NSA_REP = NSA_HEADS // NSA_KV_GROUPS
NSA_HEAD_DIM = 128
NSA_CMP_LEN = 32
NSA_CMP_STRIDE = 16
NSA_SEL_LEN = 64
NSA_SEL_COUNT = 16
NSA_WINDOW = 512
NSA_Q_WIDTH = NSA_HEADS * NSA_HEAD_DIM
NSA_KV_WIDTH = NSA_KV_GROUPS * NSA_HEAD_DIM

MOE_GROUPS = 8
MOE_PER_GROUP = 4
MOE_EXPERTS = MOE_GROUPS * MOE_PER_GROUP
MOE_TOPK = 2
MOE_FF = 1024

NEG = -1e30
BIG = 1e30
LANES = 128
ONES_ROWS = 16
LOG2E = math.log2(math.e)
VMEM_LIMIT = 56 * 1024 * 1024


def _params(sem):
    return pltpu.CompilerParams(dimension_semantics=sem, vmem_limit_bytes=VMEM_LIMIT)


def _dot(a, b):
    return jnp.dot(a, b, preferred_element_type=F32)


def _dot_nt(a, b):
    return lax.dot_general(a, b, (((1,), (1,)), ((), ())), preferred_element_type=F32)


def _mm_kernel(x_ref, w_ref, o_ref):
    o_ref[...] = _dot(x_ref[...], w_ref[...]).astype(o_ref.dtype)


def _matmul(x, w, out_dtype, tm, tn):
    m, k = x.shape
    n = w.shape[1]
    return pl.pallas_call(
        _mm_kernel,
        grid=(n // tn, m // tm),
        in_specs=[pl.BlockSpec((tm, k), lambda j, i: (i, 0)),
                  pl.BlockSpec((k, tn), lambda j, i: (0, j))],
        out_specs=pl.BlockSpec((tm, tn), lambda j, i: (i, j)),
        out_shape=jax.ShapeDtypeStruct((m, n), out_dtype),
        compiler_params=_params(("parallel", "parallel")),
        name="proj_matmul",
    )(x, w)


def _layer_norm_rows(y, g, b):
    mu = jnp.mean(y, -1, keepdims=True)
    yc = y - mu
    var = jnp.mean(yc * yc, -1, keepdims=True)
    return yc * lax.rsqrt(var + LN_EPS) * g + b


def _mm_ln_kernel(a_ref, w_ref, res_ref, g_ref, b_ref, o_ref, ob_ref):
    y = _dot(a_ref[...], w_ref[...]) + DEEPNORM_ALPHA * res_ref[...]
    h = _layer_norm_rows(y, g_ref[...], b_ref[...])
    o_ref[...] = h
    ob_ref[...] = h.astype(BF16)


def _matmul_res_ln(a, w, res, g, b, tm):
    m, k = a.shape
    n = w.shape[1]
    return pl.pallas_call(
        _mm_ln_kernel,
        grid=(m // tm,),
        in_specs=[pl.BlockSpec((tm, k), lambda i: (i, 0)),
                  pl.BlockSpec((k, n), lambda i: (0, 0)),
                  pl.BlockSpec((tm, n), lambda i: (i, 0)),
                  pl.BlockSpec((1, n), lambda i: (0, 0)),
                  pl.BlockSpec((1, n), lambda i: (0, 0))],
        out_specs=[pl.BlockSpec((tm, n), lambda i: (i, 0)),
                   pl.BlockSpec((tm, n), lambda i: (i, 0))],
        out_shape=[jax.ShapeDtypeStruct((m, n), F32), jax.ShapeDtypeStruct((m, n), BF16)],
        compiler_params=_params(("parallel",)),
        name="outproj_ln",
    )(a, w, res, g.reshape(1, n), b.reshape(1, n))


def _combine_ln_kernel(h_ref, y_ref, w_ref, g_ref, b_ref, o_ref, ob_ref):
    w = w_ref[...]
    y = DEEPNORM_ALPHA * h_ref[...] + w[:, 0:1] * y_ref[0].astype(F32) + w[:, 1:2] * y_ref[1].astype(F32)
    x = _layer_norm_rows(y, g_ref[...], b_ref[...])
    o_ref[...] = x
    ob_ref[...] = x.astype(BF16)


def _combine_ln(h, y2, w, g, b, tm):
    m, n = h.shape
    row = pl.BlockSpec((tm, n), lambda i: (i, 0))
    vec = pl.BlockSpec((1, n), lambda i: (0, 0))
    return pl.pallas_call(
        _combine_ln_kernel,
        grid=(m // tm,),
        in_specs=[row, pl.BlockSpec((MOE_TOPK, tm, n), lambda i: (0, i, 0)),
                  pl.BlockSpec((tm, MOE_TOPK), lambda i: (i, 0)), vec, vec],
        out_specs=[row, row],
        out_shape=[jax.ShapeDtypeStruct((m, n), F32), jax.ShapeDtypeStruct((m, n), BF16)],
        compiler_params=_params(("parallel",)),
        name="ffn_combine_ln",
    )(h, y2, w, g.reshape(1, n), b.reshape(1, n))


def _softmax_tile(st_ref, pt_ref, m_ref, al_ref, *, mask=None):
    rows = st_ref.shape[1]
    for c in range(rows // LANES):
        cs = slice(c * LANES, (c + 1) * LANES)
        s = st_ref[:, cs]
        if mask is not None:
            s = mask(s, c)
        m_old = m_ref[:, cs]
        m_new = jnp.maximum(m_old, jnp.max(s, axis=0, keepdims=True))
        pt_ref[:, cs] = jnp.exp2(s - m_new[0:1]).astype(BF16)
        al_ref[:, cs] = jnp.exp2(m_old - m_new)
        m_ref[:, cs] = m_new


def _pv_update(vt_aug, pt_ref, al_ref, acc_ref):
    acc_ref[...] = acc_ref[...] * al_ref[0:1, :] + _dot(vt_aug, pt_ref[...])


def _flash_loop(n, scores, softmax, update):
    scores(0, 0)

    @pl.when(n == 1)
    def _():
        softmax(0, 0, True)
        update(0, 0)

    @pl.when(n >= 2)
    def _():
        scores(1, 1)
        softmax(0, 0, False)
        n_pairs = (n - 2) // 2

        def pair(u, carry):
            j = 2 * u
            scores(j + 2, 0)
            softmax(j + 1, 1, False)
            update(j, 0)
            scores(j + 3, 1)
            softmax(j + 2, 0, False)
            update(j + 1, 1)
            return carry

        lax.fori_loop(0, n_pairs, pair, 0)
        j = 2 * n_pairs

        @pl.when(n - 1 == j + 1)
        def _():
            softmax(j + 1, 1, True)
            update(j, 0)
            update(j + 1, 1)

        @pl.when(n - 1 == j + 2)
        def _():
            scores(j + 2, 0)
            softmax(j + 1, 1, False)
            update(j, 0)
            softmax(j + 2, 0, True)
            update(j + 1, 1)
            update(j + 2, 0)


def _transpose_values(v, tk):
    b, s, h, dv = v.shape
    vt = jnp.transpose(v.reshape(b, s // tk, tk, h, dv), (0, 3, 1, 4, 2))
    return jnp.concatenate([vt, jnp.ones((b, h, s // tk, ONES_ROWS, tk), v.dtype)], axis=3)


def _diff_kernel(lq1_ref, lk1_ref, lq2_ref, lk2_ref, g_ref, q_ref, k_ref, vt_ref, o_ref,
                 q2_ref, st0_ref, st1_ref, pt0_ref, pt1_ref, m_ref, al0_ref, al1_ref, acc_ref, *, tq, tk, lam_init):
    i = pl.program_id(2)
    st, pt, al = (st0_ref, st1_ref), (pt0_ref, pt1_ref), (al0_ref, al1_ref)
    qt = q_ref[...].astype(F32).T
    half = lax.broadcasted_iota(jnp.int32, (LANES, tq), 0) < DIFF_HEAD_DIM
    q2_ref[...] = jnp.concatenate([jnp.where(half, qt, 0.0), jnp.where(half, 0.0, qt)], axis=1).astype(BF16)
    m_ref[...] = jnp.full_like(m_ref, NEG)
    acc_ref[...] = jnp.zeros_like(acc_ref)

    def scores(j, slot):
        kt = k_ref[pl.ds(pl.multiple_of(j * tk, tk), tk), :]
        st[slot][...] = _dot(kt, q2_ref[...])

    def softmax(j, slot, masked):
        def causal(s, c):
            key = j * tk + lax.broadcasted_iota(jnp.int32, (tk, LANES), 0)
            qry = i * tq + ((c * LANES + lax.broadcasted_iota(jnp.int32, (tk, LANES), 1)) & (tq - 1))
            return jnp.where(key <= qry, s, NEG)
        _softmax_tile(st[slot], pt[slot], m_ref, al[slot], mask=causal if masked else None)

    def update(j, slot):
        _pv_update(vt_ref[j], pt[slot], al[slot], acc_ref)

    _flash_loop((i * tq + tq + tk - 1) // tk, scores, softmax, update)
    acc = acc_ref[...]
    o = (acc[:LANES] / acc[LANES:LANES + 1]).T
    lam = (jnp.exp(jnp.sum(lq1_ref[...] * lk1_ref[...], -1, keepdims=True))
           - jnp.exp(jnp.sum(lq2_ref[...] * lk2_ref[...], -1, keepdims=True)) + lam_init)
    od = o[:tq] - lam * o[tq:]
    y = od * lax.rsqrt(jnp.mean(od * od, -1, keepdims=True) + 1e-5) * g_ref[...]
    o_ref[...] = (y * (1.0 - lam_init)).astype(o_ref.dtype)


def _diff_attention(qkv, lq1, lk1, lq2, lk2, subln_g, lam_init, tq, tk):
    b, s, _ = qkv.shape
    rows = 2 * tq
    dv = 2 * DIFF_HEAD_DIM
    assert tk % tq == 0 and s % tk == 0
    vt = _transpose_values(qkv[..., 2 * DIFF_WIDTH:].reshape(b, s, DIFF_HEADS, dv), tk)
    vec64 = pl.BlockSpec((1, DIFF_HEAD_DIM), lambda bb, h, i: (0, 0))
    return pl.pallas_call(
        functools.partial(_diff_kernel, tq=tq, tk=tk, lam_init=lam_init),
        grid=(b, DIFF_HEADS, s // tq),
        in_specs=[vec64, vec64, vec64, vec64,
                  pl.BlockSpec((1, LANES), lambda bb, h, i: (0, 0)),
                  pl.BlockSpec((None, tq, LANES), lambda bb, h, i: (bb, i, h)),
                  pl.BlockSpec((None, s, LANES), lambda bb, h, i: (bb, 0, DIFF_HEADS + h)),
                  pl.BlockSpec((None, None, s // tk, dv + ONES_ROWS, tk), lambda bb, h, i: (bb, h, 0, 0, 0))],
        out_specs=pl.BlockSpec((None, tq, LANES), lambda bb, h, i: (bb, i, h)),
        out_shape=jax.ShapeDtypeStruct((b, s, DIFF_WIDTH), BF16),
        scratch_shapes=[pltpu.VMEM((LANES, rows), BF16), pltpu.VMEM((tk, rows), F32), pltpu.VMEM((tk, rows), F32),
                        pltpu.VMEM((tk, rows), BF16), pltpu.VMEM((tk, rows), BF16),
                        pltpu.VMEM((8, rows), F32), pltpu.VMEM((8, rows), F32), pltpu.VMEM((8, rows), F32),
                        pltpu.VMEM((dv + ONES_ROWS, rows), F32)],
        compiler_params=_params(("parallel", "parallel", "arbitrary")),
        name="diff_attention",
    )(lq1.reshape(1, -1), lk1.reshape(1, -1), lq2.reshape(1, -1), lk2.reshape(1, -1),
      subln_g.reshape(1, -1), qkv, qkv, vt)


def _ssd_kernel(pf_ref, cw_ref, cb_ref, dtb_ref, alog_ref, drep_ref, ng_ref, tri_ref, o_ref,
                xe_ref, tail_ref, state_ref, y_ref):
    L = SSM_CHUNK
    halo = 8

    @pl.when(pl.program_id(1) == 0)
    def _():
        tail_ref[...] = jnp.zeros_like(tail_ref)
        state_ref[...] = jnp.zeros_like(state_ref)

    z = pf_ref[:, 0:SSM_INNER]
    xbc = pf_ref[:, SSM_INNER:SSM_INNER + SSM_CONV_CH]
    dt_raw = pf_ref[:, SSM_INNER + SSM_CONV_CH:SSM_INNER + SSM_CONV_CH + LANES]
    xe_ref[0:halo, :] = tail_ref[...]
    xe_ref[halo:halo + L, :] = xbc
    tail_ref[...] = xbc[L - halo:L, :]
    conv = cb_ref[...] + cw_ref[SSM_CONV - 1:SSM_CONV, :] * xbc
    for k in range(SSM_CONV - 1):
        conv = conv + cw_ref[k:k + 1, :] * xe_ref[pl.ds(halo - (SSM_CONV - 1) + k, L), :]
    xbc = conv * jax.nn.sigmoid(conv)
    xs = xbc[:, :SSM_INNER]
    dt = jax.nn.softplus(dt_raw + dtb_ref[...])
    adt = dt * (-jnp.exp(alog_ref[...]))
    acum = jnp.dot(tri_ref[...], adt, preferred_element_type=F32, precision=lax.Precision.HIGHEST)
    acum_t = acum.T
    left = lax.broadcasted_iota(jnp.int32, (1, LANES), 1) < SSM_HEAD_DIM
    tril = lax.broadcasted_iota(jnp.int32, (L, L), 0) >= lax.broadcasted_iota(jnp.int32, (L, L), 1)
    pairs_per_group = SSM_HEADS // SSM_GROUPS // 2
    for g in range(SSM_GROUPS):
        bm = xbc[:, SSM_INNER + g * SSM_STATE:SSM_INNER + (g + 1) * SSM_STATE]
        cm = xbc[:, SSM_INNER + (SSM_GROUPS + g) * SSM_STATE:SSM_INNER + (SSM_GROUPS + g + 1) * SSM_STATE]
        cb = _dot_nt(cm.astype(BF16), bm.astype(BF16))
        bm_t = bm.T
        for qq in range(pairs_per_group):
            q = g * pairs_per_group + qq
            ls = slice(q * LANES, (q + 1) * LANES)
            xs_p = xs[:, ls]
            h0, h1 = 2 * q, 2 * q + 1
            xd = xs_p * jnp.where(left, dt[:, h0:h0 + 1], dt[:, h1:h1 + 1])
            prev = state_ref[q]
            y = drep_ref[:, ls] * xs_p
            new = jnp.zeros((SSM_STATE, LANES), F32)
            for h, keep in ((h0, left), (h1, jnp.logical_not(left))):
                xd_h = jnp.where(keep, xd, 0.0).astype(BF16)
                prev_h = jnp.where(keep, prev, 0.0).astype(BF16)
                cs_col = acum[:, h:h + 1]
                cs_row = acum_t[h:h + 1, :]
                w = cb * jnp.exp(jnp.where(tril, cs_col - cs_row, -jnp.inf))
                y = y + _dot(w.astype(BF16), xd_h) + _dot((cm * jnp.exp(cs_col)).astype(BF16), prev_h)
                new = new + _dot((bm_t * jnp.exp(acum[L - 1:L, h:h + 1] - cs_row)).astype(BF16), xd_h)
            decay = jnp.where(left, jnp.exp(acum[L - 1:L, h0:h0 + 1]), jnp.exp(acum[L - 1:L, h1:h1 + 1]))
            state_ref[q] = prev * decay + new
            y_ref[:, ls] = y
    y = y_ref[...] * (z * jax.nn.sigmoid(z))
    gw = SSM_INNER // SSM_GROUPS
    for g in range(SSM_GROUPS):
        yg = y[:, g * gw:(g + 1) * gw]
        yg = yg * lax.rsqrt(jnp.mean(yg * yg, -1, keepdims=True) + 1e-5) * ng_ref[:, g * gw:(g + 1) * gw]
        o_ref[:, g * gw:(g + 1) * gw] = yg.astype(o_ref.dtype)


def _mamba2_mixer(pf, conv_w, conv_b, dt_bias, a_log, d_skip, norm_g):
    b, s, w = pf.shape
    L = SSM_CHUNK
    pad_heads = lambda v: jnp.pad(v, (0, LANES - SSM_HEADS)).reshape(1, LANES)
    tri = jnp.tril(jnp.ones((L, L), F32))
    full = lambda shape: pl.BlockSpec(shape, lambda bb, c: (0,) * len(shape))
    return pl.pallas_call(
        _ssd_kernel,
        grid=(b, s // L),
        in_specs=[pl.BlockSpec((None, L, w), lambda bb, c: (bb, c, 0)),
                  full((SSM_CONV, SSM_CONV_CH)), full((1, SSM_CONV_CH)), full((1, LANES)), full((1, LANES)),
                  full((1, SSM_INNER)), full((1, SSM_INNER)), full((L, L))],
        out_specs=pl.BlockSpec((None, L, SSM_INNER), lambda bb, c: (bb, c, 0)),
        out_shape=jax.ShapeDtypeStruct((b, s, SSM_INNER), BF16),
        scratch_shapes=[pltpu.VMEM((L + 8, SSM_CONV_CH), F32), pltpu.VMEM((8, SSM_CONV_CH), F32),
                        pltpu.VMEM((SSM_HEADS // 2, SSM_STATE, LANES), F32), pltpu.VMEM((L, SSM_INNER), F32)],
        compiler_params=_params(("parallel", "arbitrary")),
        name="ssd_mixer",
    )(pf, conv_w, conv_b.reshape(1, -1), pad_heads(dt_bias), pad_heads(a_log),
      jnp.repeat(d_skip, SSM_HEAD_DIM).reshape(1, -1), norm_g.reshape(1, -1), tri)


def _cmp_kernel(x_ref, pek_ref, pev_ref, w1k_ref, w1v_ref, w2k_ref, w2v_ref, ko_ref, vo_ref,
                top_ref, bot_ref, *, nb):
    l = pl.program_id(1)
    g4 = NSA_KV_GROUPS

    @pl.when(l == 0)
    def _():
        top_ref[...] = jnp.zeros_like(top_ref)
        bot_ref[...] = jnp.zeros_like(bot_ref)

    x = x_ref[...]
    for kv, (pe_ref, w1_ref) in enumerate(((pek_ref, w1k_ref), (pev_ref, w1v_ref))):
        x4 = jnp.concatenate([x[:, kv * NSA_KV_WIDTH + g * LANES: kv * NSA_KV_WIDTH + (g + 1) * LANES]
                              for g in range(g4)], axis=0)
        top_ref[kv] += _dot((x4 + pe_ref[pl.ds(l, 1), :]).astype(BF16), w1_ref[l])
        bot_ref[kv] += _dot((x4 + pe_ref[pl.ds(l + NSA_CMP_STRIDE, 1), :]).astype(BF16),
                            w1_ref[l + NSA_CMP_STRIDE])

    @pl.when(l == NSA_CMP_STRIDE - 1)
    def _():
        last = lax.broadcasted_iota(jnp.int32, (nb, LANES), 0) == nb - 1
        for kv, (w2_ref, o_ref) in enumerate(((w2k_ref, ko_ref), (w2v_ref, vo_ref))):
            for g in range(g4):
                top = top_ref[kv, g * nb:(g + 1) * nb, :]
                bot = bot_ref[kv, g * nb:(g + 1) * nb, :]
                hid = top + pltpu.roll(bot, nb - 1, 0)
                out = _dot(jax.nn.gelu(hid, approximate=True).astype(BF16), w2_ref[...])
                o_ref[g] = jnp.where(last, 0.0, out).astype(o_ref.dtype)


def _compress(bf, pe_k, pe_v, w1k, w1v, w2k, w2v):
    b, s, w = bf.shape
    nb = s // NSA_CMP_STRIDE
    xv = bf.reshape(b, nb, NSA_CMP_STRIDE * w)
    full = lambda shape: pl.BlockSpec(shape, lambda bb, l: (0,) * len(shape))
    out_spec = pl.BlockSpec((None, NSA_KV_GROUPS, nb, LANES), lambda bb, l: (bb, 0, 0, 0))
    out_sd = jax.ShapeDtypeStruct((b, NSA_KV_GROUPS, nb, LANES), BF16)
    w1 = lambda a: a.astype(BF16).reshape(NSA_CMP_LEN, NSA_HEAD_DIM, -1)
    return pl.pallas_call(
        functools.partial(_cmp_kernel, nb=nb),
        grid=(b, NSA_CMP_STRIDE),
        in_specs=[pl.BlockSpec((None, nb, w), lambda bb, l: (bb, 0, l)),
                  full((NSA_CMP_LEN, LANES)), full((NSA_CMP_LEN, LANES)),
                  full((NSA_CMP_LEN, NSA_HEAD_DIM, LANES)), full((NSA_CMP_LEN, NSA_HEAD_DIM, LANES)),
                  full((LANES, LANES)), full((LANES, LANES))],
        out_specs=[out_spec, out_spec],
        out_shape=[out_sd, out_sd],
        scratch_shapes=[pltpu.VMEM((2, NSA_KV_GROUPS * nb, LANES), F32),
                        pltpu.VMEM((2, NSA_KV_GROUPS * nb, LANES), F32)],
        compiler_params=_params(("parallel", "arbitrary")),
        name="nsa_compress",
    )(xv, pe_k, pe_v, w1(w1k), w1(w1v), w2k.astype(BF16), w2v.astype(BF16))


def _nsa_kernel(q_ref, gate_ref, kc_ref, vct_ref, ks_ref, vst_ref, kw_ref, vwt_ref, blk_ref, ovt_ref, o_ref,
                qa_ref, sc_ref, pc_ref, ps_ref, oc_ref, imp_ref, st0_ref, st1_ref, pt0_ref, pt1_ref, sw_ref, pw_ref,
                m_ref, al0_ref, al1_ref, acc_ref, mw_ref, alw_ref, accw_ref, *, tk, seq, slab):
    i = pl.program_id(2)
    tq = LANES
    t0 = i * tq
    rep = NSA_REP
    ncp = seq // NSA_CMP_STRIDE
    q = q_ref[...].astype(F32)
    for r in range(rep):
        qa_ref[0:LANES, r * tq:(r + 1) * tq] = q[:, r * LANES:(r + 1) * LANES].T.astype(BF16)
    qs = qa_ref[0:LANES, :]
    t_lane = t0 + lax.broadcasted_iota(jnp.int32, (1, LANES), 1)

    def compressed(nk):
        sc_ref[0:nk, :] = _dot(kc_ref[0:nk, :], qs)
        n_end = lax.broadcasted_iota(jnp.int32, (nk, LANES), 0) * NSA_CMP_STRIDE + (NSA_CMP_LEN - 1)
        cmask = n_end <= t_lane
        any_visible = t_lane >= NSA_CMP_LEN - 1
        for r in range(rep):
            cs = slice(r * tq, (r + 1) * tq)
            s = jnp.where(cmask, sc_ref[0:nk, cs], NEG)
            e = jnp.exp2(s - jnp.max(s, axis=0, keepdims=True))
            p = e * jnp.where(any_visible, 1.0 / jnp.sum(e, axis=0, keepdims=True), 0.0)
            pc_ref[0:nk, cs] = p.astype(BF16)
            ps_ref[0:nk, :] = p if r == 0 else ps_ref[0:nk, :] + p
        oc_ref[...] = _dot(vct_ref[:, 0:nk], pc_ref[0:nk, :])
        ps = ps_ref[0:nk, :]
        p_hi = ps.astype(BF16)
        p_lo = (ps - p_hi.astype(F32)).astype(BF16)
        imp_ref[...] = _dot(ovt_ref[:, 0:nk], p_hi) + _dot(ovt_ref[:, 0:nk], p_lo)

    n_visible = t0 // NSA_CMP_STRIDE + (tq - NSA_CMP_LEN) // NSA_CMP_STRIDE + 1
    for v in range(ncp // LANES):
        @pl.when((n_visible - 1) // LANES == v)
        def _():
            compressed((v + 1) * LANES)
    o_cmp_t = oc_ref[...]
    imp_t = imp_ref[...]
    cur = t_lane >> 6
    mi = lax.broadcasted_iota(jnp.int32, (LANES, LANES), 0)
    forced = (mi == 0) | (mi == cur) | (mi == cur - 1)
    vals = jnp.where(forced, -jnp.inf, jnp.where(mi <= cur, imp_t, NEG))
    pen = jnp.where(forced, 0.0, -1e9)
    mf = mi.astype(F32)
    for _ in range(NSA_SEL_COUNT - 3):
        mx = jnp.max(vals, axis=0, keepdims=True)
        first = jnp.min(jnp.where(vals == mx, mf, float(LANES)), axis=0, keepdims=True)
        hit = mf == first
        pen = jnp.where(hit, 0.0, pen)
        vals = jnp.where(hit, -jnp.inf, vals)
    pen_b = pen.astype(BF16)
    for r in range(rep):
        qa_ref[LANES:2 * LANES, r * tq:(r + 1) * tq] = pen_b

    ws = pl.multiple_of(jnp.clip(t0 - NSA_WINDOW, 0, seq - slab), LANES)
    sw_ref[...] = _dot(kw_ref[pl.ds(ws, slab), :], qs)
    mw_ref[...] = jnp.full_like(mw_ref, NEG)
    accw_ref[...] = jnp.zeros_like(accw_ref)
    jb = ws // LANES
    v_slab = jnp.concatenate([vwt_ref[jb + u] for u in range(slab // LANES)], axis=1)

    def window(s, c):
        kpos = ws + lax.broadcasted_iota(jnp.int32, (slab, LANES), 0)
        return jnp.where(kpos <= t_lane, jnp.where(kpos > t_lane - NSA_WINDOW, s, NEG), NEG)

    _softmax_tile(sw_ref, pw_ref, mw_ref, alw_ref, mask=window)
    _pv_update(v_slab, pw_ref, alw_ref, accw_ref)
    accw = accw_ref[...]
    o_win_t = accw[:LANES] / accw[LANES:LANES + 1]

    m_ref[...] = jnp.full_like(m_ref, NEG)
    acc_ref[...] = jnp.zeros_like(acc_ref)
    st, pt, al = (st0_ref, st1_ref), (pt0_ref, pt1_ref), (al0_ref, al1_ref)

    def scores(j, slot):
        k0 = pl.multiple_of(j * tk, tk)
        ka = jnp.concatenate([ks_ref[pl.ds(k0, tk), :], blk_ref[pl.ds(k0, tk), :]], axis=1)
        st[slot][...] = _dot(ka, qa_ref[...])

    def softmax(j, slot, masked):
        def causal(s, c):
            kpos = j * tk + lax.broadcasted_iota(jnp.int32, (tk, LANES), 0)
            return jnp.where(kpos <= t_lane, s, NEG)
        _softmax_tile(st[slot], pt[slot], m_ref, al[slot], mask=causal if masked else None)

    def update(j, slot):
        _pv_update(vst_ref[j], pt[slot], al[slot], acc_ref)

    _flash_loop((t0 + tq + tk - 1) // tk, scores, softmax, update)
    acc = acc_ref[...]
    o_sel_t = acc[:LANES] / acc[LANES:LANES + 1]

    g_t = jax.nn.sigmoid(gate_ref[...]).T
    for r in range(rep):
        cs = slice(r * tq, (r + 1) * tq)
        out_t = (g_t[3 * r:3 * r + 1] * o_cmp_t[:, cs] + g_t[3 * r + 1:3 * r + 2] * o_sel_t[:, cs]
                 + g_t[3 * r + 2:3 * r + 3] * o_win_t[:, cs])
        o_ref[:, r * LANES:(r + 1) * LANES] = out_t.T.astype(o_ref.dtype)


def _nsa_attention(a_bf, b_f32, k_cmp, v_cmp, tk):
    b, s, _ = a_bf.shape
    g4 = NSA_KV_GROUPS
    tq = LANES
    rows = NSA_REP * tq
    ncp = s // NSA_CMP_STRIDE
    slab = NSA_WINDOW + tq
    dva = NSA_HEAD_DIM + ONES_ROWS
    assert s >= slab and s % tk == 0 and tk % tq == 0
    blk = (jnp.arange(s)[:, None] // NSA_SEL_LEN == jnp.arange(LANES)[None, :]).astype(BF16)
    cs = np.arange(ncp)[:, None] * NSA_CMP_STRIDE
    ss = np.arange(LANES)[None, :] * NSA_SEL_LEN
    ov = np.maximum(np.minimum(cs + NSA_CMP_LEN, ss + NSA_SEL_LEN) - np.maximum(cs, ss), 0) / NSA_CMP_LEN
    ov[ncp - 1] = 0.0
    qoff = NSA_Q_WIDTH // LANES
    group_values = lambda n: a_bf[..., NSA_Q_WIDTH + n * NSA_KV_WIDTH:NSA_Q_WIDTH + (n + 1) * NSA_KV_WIDTH].reshape(
        b, s, g4, NSA_HEAD_DIM)
    vst = _transpose_values(group_values(1), tk)
    vwt = _transpose_values(group_values(3), LANES)
    vct = jnp.swapaxes(v_cmp, 2, 3)
    k_spec = lambda n: pl.BlockSpec((None, s, LANES), lambda bb, g, i: (bb, 0, qoff + n * g4 + g))
    vt_spec = lambda t: pl.BlockSpec((None, None, s // t, dva, t), lambda bb, g, i: (bb, g, 0, 0, 0))
    return pl.pallas_call(
        functools.partial(_nsa_kernel, tk=tk, seq=s, slab=slab),
        grid=(b, g4, s // tq),
        in_specs=[pl.BlockSpec((None, tq, NSA_REP * LANES), lambda bb, g, i: (bb, i, g)),
                  pl.BlockSpec((None, tq, LANES), lambda bb, g, i: (bb, i, 2 * g4 + g)),
                  pl.BlockSpec((None, None, ncp, LANES), lambda bb, g, i: (bb, g, 0, 0)),
                  pl.BlockSpec((None, None, LANES, ncp), lambda bb, g, i: (bb, g, 0, 0)),
                  k_spec(0), vt_spec(tk), k_spec(2), vt_spec(LANES),
                  pl.BlockSpec((s, LANES), lambda bb, g, i: (0, 0)),
                  pl.BlockSpec((LANES, ncp), lambda bb, g, i: (0, 0))],
        out_specs=pl.BlockSpec((None, tq, NSA_REP * LANES), lambda bb, g, i: (bb, i, g)),
        out_shape=jax.ShapeDtypeStruct((b, s, NSA_Q_WIDTH), BF16),
        scratch_shapes=[pltpu.VMEM((2 * LANES, rows), BF16),
                        pltpu.VMEM((ncp, rows), F32), pltpu.VMEM((ncp, rows), BF16), pltpu.VMEM((ncp, LANES), F32),
                        pltpu.VMEM((NSA_HEAD_DIM, rows), F32), pltpu.VMEM((LANES, LANES), F32),
                        pltpu.VMEM((tk, rows), F32), pltpu.VMEM((tk, rows), F32),
                        pltpu.VMEM((tk, rows), BF16), pltpu.VMEM((tk, rows), BF16),
                        pltpu.VMEM((slab, rows), F32), pltpu.VMEM((slab, rows), BF16),
                        pltpu.VMEM((8, rows), F32), pltpu.VMEM((8, rows), F32), pltpu.VMEM((8, rows), F32),
                        pltpu.VMEM((dva, rows), F32),
                        pltpu.VMEM((8, rows), F32), pltpu.VMEM((8, rows), F32), pltpu.VMEM((dva, rows), F32)],
        compiler_params=_params(("parallel", "parallel", "arbitrary")),
        name="nsa_attention",
    )(a_bf, b_f32, k_cmp, vct, a_bf, vst, a_bf, vwt, blk, jnp.asarray(ov.T, BF16))


def _moe_kernel(be_ref, hs_ref, first_ref, nu_ref, x_ref, wg_ref, wu_ref, wd_ref, o_ref,
                wgb_ref, wub_ref, wdb_ref, part_ref):
    i, j = pl.program_id(0), pl.program_id(1)

    @pl.when(i < nu_ref[0])
    def _():
        @pl.when(first_ref[i] == 1)
        def _():
            wgb_ref[j] = wg_ref[...].astype(BF16)
            wub_ref[j] = wu_ref[...].astype(BF16)
            wdb_ref[j] = wd_ref[...].astype(BF16)

        x = x_ref[...]
        hid = jax.nn.silu(_dot(x, wgb_ref[j])) * _dot(x, wub_ref[j])
        part = _dot(hid.astype(BF16), wdb_ref[j])

        @pl.when(j == 0)
        def _():
            part_ref[...] = part

        @pl.when(j == 1)
        def _():
            o_ref[...] = (part_ref[...] + part).astype(o_ref.dtype)

    @pl.when(i >= nu_ref[0])
    def _():
        o_ref[...] = jnp.zeros_like(o_ref)


def _moe_blocks(block_e, n_used, xg, wg, wu, wd, layer, tm):
    rows, d = xg.shape
    ff = wg.shape[-1]
    ffh = ff // 2
    n_blocks = rows // tm
    blk = jnp.arange(n_blocks, dtype=jnp.int32)
    prev_e = jnp.concatenate([jnp.full((1,), -1, jnp.int32), block_e[:-1]])
    first = ((block_e != prev_e) & (blk < n_used[0])).astype(jnp.int32)
    half = jnp.where(first[:, None] == 1, jnp.arange(2, dtype=jnp.int32)[None, :], 1).reshape(-1)
    grid_spec = pltpu.PrefetchScalarGridSpec(
        num_scalar_prefetch=4,
        grid=(n_blocks, 2),
        in_specs=[pl.BlockSpec((tm, d), lambda i, j, be, hs, fi, nu: (i, 0)),
                  pl.BlockSpec((None, None, d, ffh), lambda i, j, be, hs, fi, nu: (layer, be[i], 0, hs[2 * i + j])),
                  pl.BlockSpec((None, None, d, ffh), lambda i, j, be, hs, fi, nu: (layer, be[i], 0, hs[2 * i + j])),
                  pl.BlockSpec((None, None, ffh, d), lambda i, j, be, hs, fi, nu: (layer, be[i], hs[2 * i + j], 0))],
        out_specs=pl.BlockSpec((tm, d), lambda i, j, be, hs, fi, nu: (i, 0)),
        scratch_shapes=[pltpu.VMEM((2, d, ffh), BF16), pltpu.VMEM((2, d, ffh), BF16), pltpu.VMEM((2, ffh, d), BF16),
                        pltpu.VMEM((tm, d), F32)],
    )
    return pl.pallas_call(
        _moe_kernel,
        grid_spec=grid_spec,
        out_shape=jax.ShapeDtypeStruct((rows, d), BF16),
        compiler_params=_params(("arbitrary", "arbitrary")),
        name="moe_experts",
    )(block_e, half, first, n_used, xg, wg, wu, wd)


def _router_kernel(h_ref, wt_ref, b_ref, tri_ref, e_ref, g_ref, r_ref, cnt_ref, sig_ref, bia_ref, carry_ref):
    @pl.when(pl.program_id(0) == 0)
    def _():
        carry_ref[...] = jnp.zeros_like(carry_ref)

    tm = h_ref.shape[0]
    logits = lax.dot_general(wt_ref[...], h_ref[...], (((1,), (1,)), ((), ())),
                             preferred_element_type=F32, precision=lax.Precision.HIGHEST)
    sig = jax.nn.sigmoid(logits)
    sig_ref[...] = sig
    biased = sig + b_ref[...]
    n_chunks = tm // LANES
    for c in range(n_chunks):
        bia_ref[c] = biased[:, c * LANES:(c + 1) * LANES]
    cand = [jnp.concatenate([bia_ref[c, pl.ds(k, MOE_GROUPS, stride=MOE_PER_GROUP), :] for c in range(n_chunks)],
                            axis=1) for k in range(MOE_PER_GROUP)]

    def top_of(vals):
        best = functools.reduce(jnp.maximum, vals)
        idx = jnp.full(best.shape, MOE_PER_GROUP - 1, jnp.int32)
        for k in range(MOE_PER_GROUP - 2, -1, -1):
            idx = jnp.where(vals[k] == best, k, idx)
        return best, idx

    top1, idx1 = top_of(cand)
    top2, idx2 = top_of([jnp.where(idx1 == k, -jnp.inf, cand[k]) for k in range(MOE_PER_GROUP)])
    gsum = top1 + top2
    gi = lax.broadcasted_iota(jnp.int32, gsum.shape, 0)
    grp = jnp.min(jnp.where(gsum == jnp.max(gsum, axis=0, keepdims=True), gi, MOE_GROUPS), axis=0, keepdims=True)
    chosen = gi == grp
    e1 = grp * MOE_PER_GROUP + jnp.sum(jnp.where(chosen, idx1, 0), axis=0, keepdims=True)
    e2 = grp * MOE_PER_GROUP + jnp.sum(jnp.where(chosen, idx2, 0), axis=0, keepdims=True)
    ei = lax.broadcasted_iota(jnp.int32, (MOE_EXPERTS, tm), 0)
    oh1, oh2 = ei == e1, ei == e2
    sig = sig_ref[...]
    w1 = jnp.sum(jnp.where(oh1, sig, 0.0), axis=0, keepdims=True)
    w2 = jnp.sum(jnp.where(oh2, sig, 0.0), axis=0, keepdims=True)
    e_ref[...] = jnp.concatenate([e1, e2], axis=0)
    g_ref[...] = jnp.concatenate([w1, w2], axis=0) / (w1 + w2)
    both = jnp.where(oh1, 1.0, jnp.where(oh2, 1.0, 0.0))
    base = carry_ref[:, 0:1] + _dot(both.astype(BF16), tri_ref[...])
    r1 = jnp.sum(jnp.where(oh1, base, 0.0), axis=0, keepdims=True)
    r2 = jnp.sum(jnp.where(oh2, base, 0.0), axis=0, keepdims=True)
    r_ref[...] = jnp.concatenate([r1, r2], axis=0).astype(jnp.int32)
    carry_ref[...] += jnp.sum(both, axis=1, keepdims=True)
    cnt_ref[...] = carry_ref[...]


def _route(h, router_w, router_b, tm):
    t, d = h.shape
    e = router_w.shape[1]
    tri = (jnp.arange(tm)[:, None] < jnp.arange(tm)[None, :]).astype(BF16)
    tok = lambda dt: (pl.BlockSpec((MOE_TOPK, tm), lambda i: (0, i)), jax.ShapeDtypeStruct((MOE_TOPK, t), dt))
    (es, esd), (gs, gsd), (rs, rsd) = tok(jnp.int32), tok(F32), tok(jnp.int32)
    return pl.pallas_call(
        _router_kernel,
        grid=(t // tm,),
        in_specs=[pl.BlockSpec((tm, d), lambda i: (i, 0)), pl.BlockSpec((e, d), lambda i: (0, 0)),
                  pl.BlockSpec((e, 1), lambda i: (0, 0)), pl.BlockSpec((tm, tm), lambda i: (0, 0))],
        out_specs=[es, gs, rs, pl.BlockSpec((e, LANES), lambda i: (0, 0))],
        out_shape=[esd, gsd, rsd, jax.ShapeDtypeStruct((e, LANES), F32)],
        scratch_shapes=[pltpu.VMEM((e, tm), F32), pltpu.VMEM((tm // LANES, e, LANES), F32),
                        pltpu.VMEM((e, LANES), F32)],
        compiler_params=_params(("arbitrary",)),
        name="router",
    )(h, router_w.T, router_b.reshape(e, 1), tri)


def _moe_ffn(h, h_bf, router_w, router_b, wg, wu, wd, layer, tm):
    t, d = h.shape
    expert, gate, rank, counts = _route(h, router_w, router_b, min(512, t))
    n_blocks = t * MOE_TOPK // tm + MOE_EXPERTS
    rows = n_blocks * tm
    counts = counts[:, 0].astype(jnp.int32)
    padded = (counts + tm - 1) // tm * tm
    pad_end = jnp.cumsum(padded)
    onehot = expert[..., None] == jnp.arange(MOE_EXPERTS, dtype=jnp.int32)
    pos = rank + jnp.sum(jnp.where(onehot, pad_end - padded, 0), axis=-1)
    tok = jnp.broadcast_to(jnp.arange(t, dtype=jnp.int32), (MOE_TOPK, t))
    buf_tok = (jnp.arange(rows, dtype=jnp.int32) % t).at[pos.reshape(-1)].set(tok.reshape(-1), unique_indices=True)
    block_start = jnp.arange(n_blocks, dtype=jnp.int32) * tm
    block_e = jnp.minimum(jnp.sum((pad_end[None, :] <= block_start[:, None]).astype(jnp.int32), axis=1),
                          MOE_EXPERTS - 1)
    n_used = (pad_end[-1:] // tm).astype(jnp.int32)
    xg = h_bf[buf_tok]
    yb = _moe_blocks(block_e, n_used, xg, wg, wu, wd, layer, tm)
    return yb[pos], gate.T


def _pad_cols(w, n):
    return jnp.pad(w, ((0, 0), (0, n - w.shape[1])))


def kernel(x, router_w, router_b, even_w_in, even_w_out, diff_lam_q1, diff_lam_k1, diff_lam_q2, diff_lam_k2,
           diff_subln_g, ssm_conv_w, ssm_conv_b, ssm_dt_bias, ssm_a_log, ssm_d, ssm_norm_g, odd_w_in,
           odd_w_out, nsa_pe_k, nsa_pe_v, nsa_ck_w1, nsa_ck_w2, nsa_cv_w1, nsa_cv_w2, ln_mix_g, ln_mix_b,
           ln_ffn_g, ln_ffn_b, moe_w_gate, moe_w_up, moe_w_down):
    b, s, d = x.shape
    t = b * s
    xf = x.reshape(t, d)
    xb = xf.astype(BF16)
    for layer in range(DEPTH):
        i = layer // 2
        if layer % 2 == 0:
            lam_init = 0.8 - 0.6 * math.exp(-0.3 * layer)
            w_in = even_w_in[i]
            n_attn = 3 * DIFF_WIDTH
            col_scale = jnp.where(jnp.arange(n_attn) < DIFF_WIDTH, DIFF_HEAD_DIM ** -0.5 * LOG2E, 1.0)
            qkv = _matmul(xb, (w_in[:, :n_attn] * col_scale).astype(BF16), BF16, 1024, 1024).reshape(b, s, n_attn)
            n_ssm = -(-(w_in.shape[1] - n_attn) // LANES) * LANES
            pf = _matmul(xb, _pad_cols(w_in[:, n_attn:], n_ssm).astype(BF16), F32, 1024, n_ssm // 3)
            pf = pf.reshape(b, s, n_ssm)
            y_attn = _diff_attention(qkv, diff_lam_q1[i], diff_lam_k1[i], diff_lam_q2[i], diff_lam_k2[i],
                                     diff_subln_g[i], lam_init, 512, 512)
            y_ssm = _mamba2_mixer(pf, ssm_conv_w[i], ssm_conv_b[i], ssm_dt_bias[i], ssm_a_log[i], ssm_d[i],
                                  ssm_norm_g[i])
            mix_in = jnp.concatenate([y_attn, y_ssm], axis=-1).reshape(t, -1)
            w_out = even_w_out[i]
        else:
            w_in = odd_w_in[i]
            n_a = NSA_Q_WIDTH + 4 * NSA_KV_WIDTH
            wq = w_in[:, :NSA_Q_WIDTH] * (NSA_HEAD_DIM ** -0.5 * LOG2E)
            wkv =w_in[:, NSA_Q_WIDTH:NSA_Q_WIDTH + 6 * NSA_KV_WIDTH]
            wgate = w_in[:, NSA_Q_WIDTH + 6 * NSA_KV_WIDTH:]
            w_a = jnp.concatenate([wq, wkv[:, 2 * NSA_KV_WIDTH:]], axis=1)
            wgate = jnp.pad(wgate.reshape(d, NSA_KV_GROUPS, 3 * NSA_REP),
                            ((0, 0), (0, 0), (0, LANES - 3 * NSA_REP))).reshape(d, NSA_KV_GROUPS * LANES)
            w_b = jnp.concatenate([wkv[:, :2 * NSA_KV_WIDTH], wgate], axis=1)
            a_bf = _matmul(xb, w_a.astype(BF16), BF16, 1024, 1024).reshape(b, s, n_a)
            b_f32 = _matmul(xb, w_b.astype(BF16), F32, 1024, 512).reshape(b, s, -1)
            k_cmp, v_cmp = _compress(b_f32, nsa_pe_k[i], nsa_pe_v[i], nsa_ck_w1[i], nsa_cv_w1[i],
                                     nsa_ck_w2[i], nsa_cv_w2[i])
            mix_in = _nsa_attention(a_bf, b_f32, k_cmp, v_cmp, 512).reshape(t, -1)
            w_out = odd_w_out[i]
        h, h_bf = _matmul_res_ln(mix_in, w_out.astype(BF16), xf, ln_mix_g[layer], ln_mix_b[layer], 256)
        y2, gates = _moe_ffn(h, h_bf, router_w, router_b, moe_w_gate, moe_w_up, moe_w_down, layer, 256)
        xf, xb = _combine_ln(h, y2, gates, ln_ffn_g[layer], ln_ffn_b[layer], 256)
    return xf.reshape(b, s, d)
```

```python
import functools
import math

import jax
import jax.numpy as jnp
import numpy as np
from jax import lax
from jax.experimental import pallas as pl
from jax.experimental.pallas import tpu as pltpu

F32 = jnp.float32
BF16 = jnp.bfloat16

D_MODEL = 2048
DEPTH = 2
DEEPNORM_ALPHA = (2.0 * DEPTH) ** 0.25
LN_EPS = 1e-5

DIFF_HEADS = 8
DIFF_HEAD_DIM = 64
DIFF_WIDTH = DIFF_HEADS * 2 * DIFF_HEAD_DIM

SSM_INNER = D_MODEL // 2
SSM_HEAD_DIM = 64
SSM_HEADS = SSM_INNER // SSM_HEAD_DIM
SSM_GROUPS = 2
SSM_STATE = 128
SSM_CONV = 4
SSM_CHUNK = 128
SSM_CONV_CH = SSM_INNER + 2 * SSM_GROUPS * SSM_STATE

NSA_HEADS = 16
NSA_KV_GROUPS = 4
NSA_REP = NSA_HEADS // NSA_KV_GROUPS
NSA_HEAD_DIM = 128
NSA_CMP_LEN = 32
NSA_CMP_STRIDE = 16
NSA_SEL_LEN = 64
NSA_SEL_COUNT = 16
NSA_WINDOW = 512
NSA_Q_WIDTH = NSA_HEADS * NSA_HEAD_DIM
NSA_KV_WIDTH = NSA_KV_GROUPS * NSA_HEAD_DIM

MOE_GROUPS = 8
MOE_PER_GROUP = 4
MOE_EXPERTS = MOE_GROUPS * MOE_PER_GROUP
MOE_TOPK = 2
MOE_FF = 1024

NEG = -1e30
BIG = 1e30
LANES = 128
ONES_ROWS = 16
LOG2E = math.log2(math.e)
VMEM_LIMIT = 56 * 1024 * 1024


def _params(sem):
    return pltpu.CompilerParams(dimension_semantics=sem, vmem_limit_bytes=VMEM_LIMIT)


def _dot(a, b):
    return jnp.dot(a, b, preferred_element_type=F32)


def _dot_nt(a, b):
    return lax.dot_general(a, b, (((1,), (1,)), ((), ())), preferred_element_type=F32)


def _mm_kernel(x_ref, w_ref, o_ref):
    o_ref[...] = _dot(x_ref[...], w_ref[...]).astype(o_ref.dtype)


def _matmul(x, w, out_dtype, tm, tn):
    m, k = x.shape
    n = w.shape[1]
    return pl.pallas_call(
        _mm_kernel,
        grid=(n // tn, m // tm),
        in_specs=[pl.BlockSpec((tm, k), lambda j, i: (i, 0)),
                  pl.BlockSpec((k, tn), lambda j, i: (0, j))],
        out_specs=pl.BlockSpec((tm, tn), lambda j, i: (i, j)),
        out_shape=jax.ShapeDtypeStruct((m, n), out_dtype),
        compiler_params=_params(("parallel", "parallel")),
        name="proj_matmul",
    )(x, w)


def _layer_norm_rows(y, g, b):
    mu = jnp.mean(y, -1, keepdims=True)
    yc = y - mu
    var = jnp.mean(yc * yc, -1, keepdims=True)
    return yc * lax.rsqrt(var + LN_EPS) * g + b


def _mm_ln_kernel(a_ref, w_ref, res_ref, g_ref, b_ref, o_ref, ob_ref):
    y = _dot(a_ref[...], w_ref[...]) + DEEPNORM_ALPHA * res_ref[...]
    h = _layer_norm_rows(y, g_ref[...], b_ref[...])
    o_ref[...] = h
    ob_ref[...] = h.astype(BF16)


def _matmul_res_ln(a, w, res, g, b, tm):
    m, k = a.shape
    n = w.shape[1]
    return pl.pallas_call(
        _mm_ln_kernel,
        grid=(m // tm,),
        in_specs=[pl.BlockSpec((tm, k), lambda i: (i, 0)),
                  pl.BlockSpec((k, n), lambda i: (0, 0)),
                  pl.BlockSpec((tm, n), lambda i: (i, 0)),
                  pl.BlockSpec((1, n), lambda i: (0, 0)),
                  pl.BlockSpec((1, n), lambda i: (0, 0))],
        out_specs=[pl.BlockSpec((tm, n), lambda i: (i, 0)),
                   pl.BlockSpec((tm, n), lambda i: (i, 0))],
        out_shape=[jax.ShapeDtypeStruct((m, n), F32), jax.ShapeDtypeStruct((m, n), BF16)],
        compiler_params=_params(("parallel",)),
        name="outproj_ln",
    )(a, w, res, g.reshape(1, n), b.reshape(1, n))


def _combine_ln_kernel(h_ref, y_ref, w_ref, g_ref, b_ref, o_ref, ob_ref):
    w = w_ref[...]
    y = DEEPNORM_ALPHA * h_ref[...] + w[:, 0:1] * y_ref[0].astype(F32) + w[:, 1:2] * y_ref[1].astype(F32)
    x = _layer_norm_rows(y, g_ref[...], b_ref[...])
    o_ref[...] = x
    ob_ref[...] = x.astype(BF16)


def _combine_ln(h, y2, w, g, b, tm):
    m, n = h.shape
    row = pl.BlockSpec((tm, n), lambda i: (i, 0))
    vec = pl.BlockSpec((1, n), lambda i: (0, 0))
    return pl.pallas_call(
        _combine_ln_kernel,
        grid=(m // tm,),
        in_specs=[row, pl.BlockSpec((MOE_TOPK, tm, n), lambda i: (0, i, 0)),
                  pl.BlockSpec((tm, MOE_TOPK), lambda i: (i, 0)), vec, vec],
        out_specs=[row, row],
        out_shape=[jax.ShapeDtypeStruct((m, n), F32), jax.ShapeDtypeStruct((m, n), BF16)],
        compiler_params=_params(("parallel",)),
        name="ffn_combine_ln",
    )(h, y2, w, g.reshape(1, n), b.reshape(1, n))


def _softmax_tile(st_ref, pt_ref, m_ref, al_ref, *, mask=None):
    rows = st_ref.shape[1]
    for c in range(rows // LANES):
        cs = slice(c * LANES, (c + 1) * LANES)
        s = st_ref[:, cs]
        if mask is not None:
            s = mask(s, c)
        m_old = m_ref[:, cs]
        m_new = jnp.maximum(m_old, jnp.max(s, axis=0, keepdims=True))
        pt_ref[:, cs] = jnp.exp2(s - m_new[0:1]).astype(BF16)
        al_ref[:, cs] = jnp.exp2(m_old - m_new)
        m_ref[:, cs] = m_new


def _pv_update(vt_aug, pt_ref, al_ref, acc_ref):
    acc_ref[...] = acc_ref[...] * al_ref[0:1, :] + _dot(vt_aug, pt_ref[...])


def _flash_loop(n, scores, softmax, update):
    scores(0, 0)

    @pl.when(n == 1)
    def _():
        softmax(0, 0, True)
        update(0, 0)

    @pl.when(n >= 2)
    def _():
        scores(1, 1)
        softmax(0, 0, False)
        n_pairs = (n - 2) // 2

        def pair(u, carry):
            j = 2 * u
            scores(j + 2, 0)
            softmax(j + 1, 1, False)
            update(j, 0)
            scores(j + 3, 1)
            softmax(j + 2, 0, False)
            update(j + 1, 1)
            return carry

        lax.fori_loop(0, n_pairs, pair, 0)
        j = 2 * n_pairs

        @pl.when(n - 1 == j + 1)
        def _():
            softmax(j + 1, 1, True)
            update(j, 0)
            update(j + 1, 1)

        @pl.when(n - 1 == j + 2)
        def _():
            scores(j + 2, 0)
            softmax(j + 1, 1, False)
            update(j, 0)
            softmax(j + 2, 0, True)
            update(j + 1, 1)
            update(j + 2, 0)


def _transpose_values(v, tk):
    b, s, h, dv = v.shape
    vt = jnp.transpose(v.reshape(b, s // tk, tk, h, dv), (0, 3, 1, 4, 2))
    return jnp.concatenate([vt, jnp.ones((b, h, s // tk, ONES_ROWS, tk), v.dtype)], axis=3)


def _diff_kernel(lq1_ref, lk1_ref, lq2_ref, lk2_ref, g_ref, q_ref, k_ref, vt_ref, o_ref,
                 q2_ref, st0_ref, st1_ref, pt0_ref, pt1_ref, m_ref, al0_ref, al1_ref, acc_ref, *, tq, tk, lam_init):
    i = pl.program_id(2)
    st, pt, al = (st0_ref, st1_ref), (pt0_ref, pt1_ref), (al0_ref, al1_ref)
    qt = q_ref[...].astype(F32).T
    half = lax.broadcasted_iota(jnp.int32, (LANES, tq), 0) < DIFF_HEAD_DIM
    q2_ref[...] = jnp.concatenate([jnp.where(half, qt, 0.0), jnp.where(half, 0.0, qt)], axis=1).astype(BF16)
    m_ref[...] = jnp.full_like(m_ref, NEG)
    acc_ref[...] = jnp.zeros_like(acc_ref)

    def scores(j, slot):
        kt = k_ref[pl.ds(pl.multiple_of(j * tk, tk), tk), :]
        st[slot][...] = _dot(kt, q2_ref[...])

    def softmax(j, slot, masked):
        def causal(s, c):
            key = j * tk + lax.broadcasted_iota(jnp.int32, (tk, LANES), 0)
            qry = i * tq + ((c * LANES + lax.broadcasted_iota(jnp.int32, (tk, LANES), 1)) & (tq - 1))
            return jnp.where(key <= qry, s, NEG)
        _softmax_tile(st[slot], pt[slot], m_ref, al[slot], mask=causal if masked else None)

    def update(j, slot):
        _pv_update(vt_ref[j], pt[slot], al[slot], acc_ref)

    _flash_loop((i * tq + tq + tk - 1) // tk, scores, softmax, update)
    acc = acc_ref[...]
    o = (acc[:LANES] / acc[LANES:LANES + 1]).T
    lam = (jnp.exp(jnp.sum(lq1_ref[...] * lk1_ref[...], -1, keepdims=True))
           - jnp.exp(jnp.sum(lq2_ref[...] * lk2_ref[...], -1, keepdims=True)) + lam_init)
    od = o[:tq] - lam * o[tq:]
    y = od * lax.rsqrt(jnp.mean(od * od, -1, keepdims=True) + 1e-5) * g_ref[...]
    o_ref[...] = (y * (1.0 - lam_init)).astype(o_ref.dtype)


def _diff_attention(qkv, lq1, lk1, lq2, lk2, subln_g, lam_init, tq, tk):
    b, s, _ = qkv.shape
    rows = 2 * tq
    dv = 2 * DIFF_HEAD_DIM
    assert tk % tq == 0 and s % tk == 0
    vt = _transpose_values(qkv[..., 2 * DIFF_WIDTH:].reshape(b, s, DIFF_HEADS, dv), tk)
    vec64 = pl.BlockSpec((1, DIFF_HEAD_DIM), lambda bb, h, i: (0, 0))
    return pl.pallas_call(
        functools.partial(_diff_kernel, tq=tq, tk=tk, lam_init=lam_init),
        grid=(b, DIFF_HEADS, s // tq),
        in_specs=[vec64, vec64, vec64, vec64,
                  pl.BlockSpec((1, LANES), lambda bb, h, i: (0, 0)),
                  pl.BlockSpec((None, tq, LANES), lambda bb, h, i: (bb, i, h)),
                  pl.BlockSpec((None, s, LANES), lambda bb, h, i: (bb, 0, DIFF_HEADS + h)),
                  pl.BlockSpec((None, None, s // tk, dv + ONES_ROWS, tk), lambda bb, h, i: (bb, h, 0, 0, 0))],
        out_specs=pl.BlockSpec((None, tq, LANES), lambda bb, h, i: (bb, i, h)),
        out_shape=jax.ShapeDtypeStruct((b, s, DIFF_WIDTH), BF16),
        scratch_shapes=[pltpu.VMEM((LANES, rows), BF16), pltpu.VMEM((tk, rows), F32), pltpu.VMEM((tk, rows), F32),
                        pltpu.VMEM((tk, rows), BF16), pltpu.VMEM((tk, rows), BF16),
                        pltpu.VMEM((8, rows), F32), pltpu.VMEM((8, rows), F32), pltpu.VMEM((8, rows), F32),
                        pltpu.VMEM((dv + ONES_ROWS, rows), F32)],
        compiler_params=_params(("parallel", "parallel", "arbitrary")),
        name="diff_attention",
    )(lq1.reshape(1, -1), lk1.reshape(1, -1), lq2.reshape(1, -1), lk2.reshape(1, -1),
      subln_g.reshape(1, -1), qkv, qkv, vt)


def _ssd_kernel(pf_ref, cw_ref, cb_ref, dtb_ref, alog_ref, drep_ref, ng_ref, tri_ref, o_ref,
                xe_ref, tail_ref, state_ref, y_ref):
    L = SSM_CHUNK
    halo = 8

    @pl.when(pl.program_id(1) == 0)
    def _():
        tail_ref[...] = jnp.zeros_like(tail_ref)
        state_ref[...] = jnp.zeros_like(state_ref)

    z = pf_ref[:, 0:SSM_INNER]
    xbc = pf_ref[:, SSM_INNER:SSM_INNER + SSM_CONV_CH]
    dt_raw = pf_ref[:, SSM_INNER + SSM_CONV_CH:SSM_INNER + SSM_CONV_CH + LANES]
    xe_ref[0:halo, :] = tail_ref[...]
    xe_ref[halo:halo + L, :] = xbc
    tail_ref[...] = xbc[L - halo:L, :]
    conv = cb_ref[...] + cw_ref[SSM_CONV - 1:SSM_CONV, :] * xbc
    for k in range(SSM_CONV - 1):
        conv = conv + cw_ref[k:k + 1, :] * xe_ref[pl.ds(halo - (SSM_CONV - 1) + k, L), :]
    xbc = conv * jax.nn.sigmoid(conv)
    xs = xbc[:, :SSM_INNER]
    dt = jax.nn.softplus(dt_raw + dtb_ref[...])
    adt = dt * (-jnp.exp(alog_ref[...]))
    acum = jnp.dot(tri_ref[...], adt, preferred_element_type=F32, precision=lax.Precision.HIGHEST)
    acum_t = acum.T
    left = lax.broadcasted_iota(jnp.int32, (1, LANES), 1) < SSM_HEAD_DIM
    tril = lax.broadcasted_iota(jnp.int32, (L, L), 0) >= lax.broadcasted_iota(jnp.int32, (L, L), 1)
    pairs_per_group = SSM_HEADS // SSM_GROUPS // 2
    for g in range(SSM_GROUPS):
        bm = xbc[:, SSM_INNER + g * SSM_STATE:SSM_INNER + (g + 1) * SSM_STATE]
        cm = xbc[:, SSM_INNER + (SSM_GROUPS + g) * SSM_STATE:SSM_INNER + (SSM_GROUPS + g + 1) * SSM_STATE]
        cb = _dot_nt(cm.astype(BF16), bm.astype(BF16))
        bm_t = bm.T
        for qq in range(pairs_per_group):
            q = g * pairs_per_group + qq
            ls = slice(q * LANES, (q + 1) * LANES)
            xs_p = xs[:, ls]
            h0, h1 = 2 * q, 2 * q + 1
            xd = xs_p * jnp.where(left, dt[:, h0:h0 + 1], dt[:, h1:h1 + 1])
            prev = state_ref[q]
            y = drep_ref[:, ls] * xs_p
            new = jnp.zeros((SSM_STATE, LANES), F32)
            for h, keep in ((h0, left), (h1, jnp.logical_not(left))):
                xd_h = jnp.where(keep, xd, 0.0).astype(BF16)
                prev_h = jnp.where(keep, prev, 0.0).astype(BF16)
                cs_col = acum[:, h:h + 1]
                cs_row = acum_t[h:h + 1, :]
                w = cb * jnp.exp(jnp.where(tril, cs_col - cs_row, -jnp.inf))
                y = y + _dot(w.astype(BF16), xd_h) + _dot((cm * jnp.exp(cs_col)).astype(BF16), prev_h)
                new = new + _dot((bm_t * jnp.exp(acum[L - 1:L, h:h + 1] - cs_row)).astype(BF16), xd_h)
            decay = jnp.where(left, jnp.exp(acum[L - 1:L, h0:h0 + 1]), jnp.exp(acum[L - 1:L, h1:h1 + 1]))
            state_ref[q] = prev * decay + new
            y_ref[:, ls] = y
    y = y_ref[...] * (z * jax.nn.sigmoid(z))
    gw = SSM_INNER // SSM_GROUPS
    for g in range(SSM_GROUPS):
        yg = y[:, g * gw:(g + 1) * gw]
        yg = yg * lax.rsqrt(jnp.mean(yg * yg, -1, keepdims=True) + 1e-5) * ng_ref[:, g * gw:(g + 1) * gw]
        o_ref[:, g * gw:(g + 1) * gw] = yg.astype(o_ref.dtype)


def _mamba2_mixer(pf, conv_w, conv_b, dt_bias, a_log, d_skip, norm_g):
    b, s, w = pf.shape
    L = SSM_CHUNK
    pad_heads = lambda v: jnp.pad(v, (0, LANES - SSM_HEADS)).reshape(1, LANES)
    tri = jnp.tril(jnp.ones((L, L), F32))
    full = lambda shape: pl.BlockSpec(shape, lambda bb, c: (0,) * len(shape))
    return pl.pallas_call(
        _ssd_kernel,
        grid=(b, s // L),
        in_specs=[pl.BlockSpec((None, L, w), lambda bb, c: (bb, c, 0)),
                  full((SSM_CONV, SSM_CONV_CH)), full((1, SSM_CONV_CH)), full((1, LANES)), full((1, LANES)),
                  full((1, SSM_INNER)), full((1, SSM_INNER)), full((L, L))],
        out_specs=pl.BlockSpec((None, L, SSM_INNER), lambda bb, c: (bb, c, 0)),
        out_shape=jax.ShapeDtypeStruct((b, s, SSM_INNER), BF16),
        scratch_shapes=[pltpu.VMEM((L + 8, SSM_CONV_CH), F32), pltpu.VMEM((8, SSM_CONV_CH), F32),
                        pltpu.VMEM((SSM_HEADS // 2, SSM_STATE, LANES), F32), pltpu.VMEM((L, SSM_INNER), F32)],
        compiler_params=_params(("parallel", "arbitrary")),
        name="ssd_mixer",
    )(pf, conv_w, conv_b.reshape(1, -1), pad_heads(dt_bias), pad_heads(a_log),
      jnp.repeat(d_skip, SSM_HEAD_DIM).reshape(1, -1), norm_g.reshape(1, -1), tri)


def _cmp_kernel(x_ref, pek_ref, pev_ref, w1k_ref, w1v_ref, w2k_ref, w2v_ref, ko_ref, vo_ref,
                top_ref, bot_ref, *, nb):
    l = pl.program_id(1)
    g4 = NSA_KV_GROUPS

    @pl.when(l == 0)
    def _():
        top_ref[...] = jnp.zeros_like(top_ref)
        bot_ref[...] = jnp.zeros_like(bot_ref)

    x = x_ref[...]
    for kv, (pe_ref, w1_ref) in enumerate(((pek_ref, w1k_ref), (pev_ref, w1v_ref))):
        x4 = jnp.concatenate([x[:, kv * NSA_KV_WIDTH + g * LANES: kv * NSA_KV_WIDTH + (g + 1) * LANES]
                              for g in range(g4)], axis=0)
        top_ref[kv] += _dot((x4 + pe_ref[pl.ds(l, 1), :]).astype(BF16), w1_ref[l])
        bot_ref[kv] += _dot((x4 + pe_ref[pl.ds(l + NSA_CMP_STRIDE, 1), :]).astype(BF16),
                            w1_ref[l + NSA_CMP_STRIDE])

    @pl.when(l == NSA_CMP_STRIDE - 1)
    def _():
        last = lax.broadcasted_iota(jnp.int32, (nb, LANES), 0) == nb - 1
        for kv, (w2_ref, o_ref) in enumerate(((w2k_ref, ko_ref), (w2v_ref, vo_ref))):
            for g in range(g4):
                top = top_ref[kv, g * nb:(g + 1) * nb, :]
                bot = bot_ref[kv, g * nb:(g + 1) * nb, :]
                hid = top + pltpu.roll(bot, nb - 1, 0)
                out = _dot(jax.nn.gelu(hid, approximate=True).astype(BF16), w2_ref[...])
                o_ref[g] = jnp.where(last, 0.0, out).astype(o_ref.dtype)


def _compress(bf, pe_k, pe_v, w1k, w1v, w2k, w2v):
    b, s, w = bf.shape
    nb = s // NSA_CMP_STRIDE
    xv = bf.reshape(b, nb, NSA_CMP_STRIDE * w)
    full = lambda shape: pl.BlockSpec(shape, lambda bb, l: (0,) * len(shape))
    out_spec = pl.BlockSpec((None, NSA_KV_GROUPS, nb, LANES), lambda bb, l: (bb, 0, 0, 0))
    out_sd = jax.ShapeDtypeStruct((b, NSA_KV_GROUPS, nb, LANES), BF16)
    w1 = lambda a: a.astype(BF16).reshape(NSA_CMP_LEN, NSA_HEAD_DIM, -1)
    return pl.pallas_call(
        functools.partial(_cmp_kernel, nb=nb),
        grid=(b, NSA_CMP_STRIDE),
        in_specs=[pl.BlockSpec((None, nb, w), lambda bb, l: (bb, 0, l)),
                  full((NSA_CMP_LEN, LANES)), full((NSA_CMP_LEN, LANES)),
                  full((NSA_CMP_LEN, NSA_HEAD_DIM, LANES)), full((NSA_CMP_LEN, NSA_HEAD_DIM, LANES)),
                  full((LANES, LANES)), full((LANES, LANES))],
        out_specs=[out_spec, out_spec],
        out_shape=[out_sd, out_sd],
        scratch_shapes=[pltpu.VMEM((2, NSA_KV_GROUPS * nb, LANES), F32),
                        pltpu.VMEM((2, NSA_KV_GROUPS * nb, LANES), F32)],
        compiler_params=_params(("parallel", "arbitrary")),
        name="nsa_compress",
    )(xv, pe_k, pe_v, w1(w1k), w1(w1v), w2k.astype(BF16), w2v.astype(BF16))


def _nsa_kernel(q_ref, gate_ref, kc_ref, vct_ref, ks_ref, vst_ref, kw_ref, vwt_ref, blk_ref, ovt_ref, o_ref,
                qa_ref, sc_ref, pc_ref, ps_ref, oc_ref, imp_ref, st0_ref, st1_ref, pt0_ref, pt1_ref, sw_ref, pw_ref,
                m_ref, al0_ref, al1_ref, acc_ref, mw_ref, alw_ref, accw_ref, *, tq, tk, seq, slab):
    i = pl.program_id(2)
    t0 = i * tq
    rep = NSA_REP
    nq = tq // LANES
    ncp = seq // NSA_CMP_STRIDE
    lane = lax.broadcasted_iota(jnp.int32, (1, LANES), 1)
    t_of = lambda c: t0 + (c % nq) * LANES + lane
    q = q_ref[...].astype(F32)
    for r in range(rep):
        qa_ref[0:LANES, r * tq:(r + 1) * tq] = q[:, r * LANES:(r + 1) * LANES].T.astype(BF16)
    qs = qa_ref[0:LANES, :]

    def compressed(nk):
        sc_ref[0:nk, :] = _dot(kc_ref[0:nk, :], qs)
        n_end = lax.broadcasted_iota(jnp.int32, (nk, LANES), 0) * NSA_CMP_STRIDE + (NSA_CMP_LEN - 1)
        for c in range(rep * nq):
            cs = slice(c * LANES, (c + 1) * LANES)
            qsl = slice((c % nq) * LANES, (c % nq + 1) * LANES)
            s = jnp.where(n_end <= t_of(c), sc_ref[0:nk, cs], NEG)
            e = jnp.exp2(s - jnp.max(s, axis=0, keepdims=True))
            any_visible = t_of(c) >= NSA_CMP_LEN - 1
            p = e * jnp.where(any_visible, 1.0 / jnp.sum(e, axis=0, keepdims=True), 0.0)
            pc_ref[0:nk, cs] = p.astype(BF16)
            ps_ref[0:nk, qsl] = p if c < nq else ps_ref[0:nk, qsl] + p
        oc_ref[...] = _dot(vct_ref[:, 0:nk], pc_ref[0:nk, :])
        ps = ps_ref[0:nk, :]
        p_hi = ps.astype(BF16)
        p_lo = (ps - p_hi.astype(F32)).astype(BF16)
        imp_ref[...] = _dot(ovt_ref[:, 0:nk], p_hi) + _dot(ovt_ref[:, 0:nk], p_lo)

    n_visible = t0 // NSA_CMP_STRIDE + (tq - NSA_CMP_LEN) // NSA_CMP_STRIDE + 1
    for v in range(ncp // LANES):
        @pl.when((n_visible - 1) // LANES == v)
        def _():
            compressed((v + 1) * LANES)
    o_cmp_t = oc_ref[...]
    mi = lax.broadcasted_iota(jnp.int32, (LANES, LANES), 0)
    mf = mi.astype(F32)
    for qc in range(nq):
        cur = t_of(qc) >> 6
        forced = (mi == 0) | (mi == cur) | (mi == cur - 1)
        vals = jnp.where(forced, -jnp.inf, jnp.where(mi <= cur, imp_ref[:, qc * LANES:(qc + 1) * LANES], NEG))
        pen = jnp.where(forced, 0.0, -1e9)
        for _ in range(NSA_SEL_COUNT - 3):
            mx = jnp.max(vals, axis=0, keepdims=True)
            first = jnp.min(jnp.where(vals == mx, mf, float(LANES)), axis=0, keepdims=True)
            hit = mf == first
            pen = jnp.where(hit, 0.0, pen)
            vals = jnp.where(hit, -jnp.inf, vals)
        pen_b = pen.astype(BF16)
        for r in range(rep):
            qa_ref[LANES:2 * LANES, r * tq + qc * LANES:r * tq + (qc + 1) * LANES] = pen_b

    ws = pl.multiple_of(jnp.clip(t0 - NSA_WINDOW, 0, seq - slab), LANES)
    sw_ref[...] = _dot(kw_ref[pl.ds(ws, slab), :], qs)
    mw_ref[...] = jnp.full_like(mw_ref, NEG)
    accw_ref[...] = jnp.zeros_like(accw_ref)
    jb = ws // LANES
    v_slab = jnp.concatenate([vwt_ref[jb + u] for u in range(slab // LANES)], axis=1)

    def window(s, c):
        kpos = ws + lax.broadcasted_iota(jnp.int32, (slab, LANES), 0)
        return jnp.where(kpos <= t_of(c), jnp.where(kpos > t_of(c) - NSA_WINDOW, s, NEG), NEG)

    _softmax_tile(sw_ref, pw_ref, mw_ref, alw_ref, mask=window)
    _pv_update(v_slab, pw_ref, alw_ref, accw_ref)
    accw = accw_ref[...]
    o_win_t = accw[:LANES] / accw[LANES:LANES + 1]

    m_ref[...] = jnp.full_like(m_ref, NEG)
    acc_ref[...] = jnp.zeros_like(acc_ref)
    st, pt, al = (st0_ref, st1_ref), (pt0_ref, pt1_ref), (al0_ref, al1_ref)

    def scores(j, slot):
        k0 = pl.multiple_of(j * tk, tk)
        ka = jnp.concatenate([ks_ref[pl.ds(k0, tk), :], blk_ref[pl.ds(k0, tk), :]], axis=1)
        st[slot][...] = _dot(ka, qa_ref[...])

    def softmax(j, slot, masked):
        def causal(s, c):
            kpos = j * tk + lax.broadcasted_iota(jnp.int32, (tk, LANES), 0)
            return jnp.where(kpos <= t_of(c), s, NEG)
        _softmax_tile(st[slot], pt[slot], m_ref, al[slot], mask=causal if masked else None)

    def update(j, slot):
        _pv_update(vst_ref[j], pt[slot], al[slot], acc_ref)

    _flash_loop((t0 + tq + tk - 1) // tk, scores, softmax, update)
    acc = acc_ref[...]
    o_sel_t = acc[:LANES] / acc[LANES:LANES + 1]

    g_t = jax.nn.sigmoid(gate_ref[...]).T
    for r in range(rep):
        cs = slice(r * tq, (r + 1) * tq)
        out_t = (g_t[3 * r:3 * r + 1] * o_cmp_t[:, cs] + g_t[3 * r + 1:3 * r + 2] * o_sel_t[:, cs]
                 + g_t[3 * r + 2:3 * r + 3] * o_win_t[:, cs])
        o_ref[:, r * LANES:(r + 1) * LANES] = out_t.T.astype(o_ref.dtype)


def _nsa_attention(a_bf, b_f32, k_cmp, v_cmp, tq, tk):
    b, s, _ = a_bf.shape
    g4 = NSA_KV_GROUPS
    rows = NSA_REP * tq
    ncp = s // NSA_CMP_STRIDE
    slab = NSA_WINDOW + tq
    dva = NSA_HEAD_DIM + ONES_ROWS
    assert s >= slab and s % tk == 0 and tk % tq == 0 and tq % LANES == 0 and (s - slab) % LANES == 0
    blk = (jnp.arange(s)[:, None] // NSA_SEL_LEN == jnp.arange(LANES)[None, :]).astype(BF16)
    cs = np.arange(ncp)[:, None] * NSA_CMP_STRIDE
    ss = np.arange(LANES)[None, :] * NSA_SEL_LEN
    ov = np.maximum(np.minimum(cs + NSA_CMP_LEN, ss + NSA_SEL_LEN) - np.maximum(cs, ss), 0) / NSA_CMP_LEN
    ov[ncp - 1] = 0.0
    qoff = NSA_Q_WIDTH // LANES
    group_values = lambda n: a_bf[..., NSA_Q_WIDTH + n * NSA_KV_WIDTH:NSA_Q_WIDTH + (n + 1) * NSA_KV_WIDTH].reshape(
        b, s, g4, NSA_HEAD_DIM)
    vst = _transpose_values(group_values(1), tk)
    vwt = _transpose_values(group_values(3), LANES)
    vct = jnp.swapaxes(v_cmp, 2, 3)
    k_spec = lambda n: pl.BlockSpec((None, s, LANES), lambda bb, g, i: (bb, 0, qoff + n * g4 + g))
    vt_spec = lambda t: pl.BlockSpec((None, None, s // t, dva, t), lambda bb, g, i: (bb, g, 0, 0, 0))
    return pl.pallas_call(
        functools.partial(_nsa_kernel, tq=tq, tk=tk, seq=s, slab=slab),
        grid=(b, g4, s // tq),
        in_specs=[pl.BlockSpec((None, tq, NSA_REP * LANES), lambda bb, g, i: (bb, i, g)),
                  pl.BlockSpec((None, tq, LANES), lambda bb, g, i: (bb, i, 2 * g4 + g)),
                  pl.BlockSpec((None, None, ncp, LANES), lambda bb, g, i: (bb, g, 0, 0)),
                  pl.BlockSpec((None, None, LANES, ncp), lambda bb, g, i: (bb, g, 0, 0)),
                  k_spec(0), vt_spec(tk), k_spec(2), vt_spec(LANES),
                  pl.BlockSpec((s, LANES), lambda bb, g, i: (0, 0)),
                  pl.BlockSpec((LANES, ncp), lambda bb, g, i: (0, 0))],
        out_specs=pl.BlockSpec((None, tq, NSA_REP * LANES), lambda bb, g, i: (bb, i, g)),
        out_shape=jax.ShapeDtypeStruct((b, s, NSA_Q_WIDTH), BF16),
        scratch_shapes=[pltpu.VMEM((2 * LANES, rows), BF16),
                        pltpu.VMEM((ncp, rows), F32), pltpu.VMEM((ncp, rows), BF16), pltpu.VMEM((ncp, tq), F32),
                        pltpu.VMEM((NSA_HEAD_DIM, rows), F32), pltpu.VMEM((LANES, tq), F32),
                        pltpu.VMEM((tk, rows), F32), pltpu.VMEM((tk, rows), F32),
                        pltpu.VMEM((tk, rows), BF16), pltpu.VMEM((tk, rows), BF16),
                        pltpu.VMEM((slab, rows), F32), pltpu.VMEM((slab, rows), BF16),
                        pltpu.VMEM((8, rows), F32), pltpu.VMEM((8, rows), F32), pltpu.VMEM((8, rows), F32),
                        pltpu.VMEM((dva, rows), F32),
                        pltpu.VMEM((8, rows), F32), pltpu.VMEM((8, rows), F32), pltpu.VMEM((dva, rows), F32)],
        compiler_params=_params(("parallel", "parallel", "arbitrary")),
        name="nsa_attention",
    )(a_bf, b_f32, k_cmp, vct, a_bf, vst, a_bf, vwt, blk, jnp.asarray(ov.T, BF16))


def _moe_kernel(be_ref, hs_ref, first_ref, nu_ref, x_ref, wg_ref, wu_ref, wd_ref, o_ref,
                wgb_ref, wub_ref, wdb_ref, part_ref):
    i, j = pl.program_id(0), pl.program_id(1)

    @pl.when(i < nu_ref[0])
    def _():
        @pl.when(first_ref[i] == 1)
        def _():
            wgb_ref[j] = wg_ref[...].astype(BF16)
            wub_ref[j] = wu_ref[...].astype(BF16)
            wdb_ref[j] = wd_ref[...].astype(BF16)

        x = x_ref[...]
        hid = jax.nn.silu(_dot(x, wgb_ref[j])) * _dot(x, wub_ref[j])
        part = _dot(hid.astype(BF16), wdb_ref[j])

        @pl.when(j == 0)
        def _():
            part_ref[...] = part

        @pl.when(j == 1)
        def _():
            o_ref[...] = (part_ref[...] + part).astype(o_ref.dtype)

    @pl.when(i >= nu_ref[0])
    def _():
        o_ref[...] = jnp.zeros_like(o_ref)


def _moe_blocks(block_e, n_used, xg, wg, wu, wd, layer, tm):
    rows, d = xg.shape
    ff = wg.shape[-1]
    ffh = ff // 2
    n_blocks = rows // tm
    blk = jnp.arange(n_blocks, dtype=jnp.int32)
    prev_e = jnp.concatenate([jnp.full((1,), -1, jnp.int32), block_e[:-1]])
    first = ((block_e != prev_e) & (blk < n_used[0])).astype(jnp.int32)
    half = jnp.where(first[:, None] == 1, jnp.arange(2, dtype=jnp.int32)[None, :], 1).reshape(-1)
    grid_spec = pltpu.PrefetchScalarGridSpec(
        num_scalar_prefetch=4,
        grid=(n_blocks, 2),
        in_specs=[pl.BlockSpec((tm, d), lambda i, j, be, hs, fi, nu: (i, 0)),
                  pl.BlockSpec((None, None, d, ffh), lambda i, j, be, hs, fi, nu: (layer, be[i], 0, hs[2 * i + j])),
                  pl.BlockSpec((None, None, d, ffh), lambda i, j, be, hs, fi, nu: (layer, be[i], 0, hs[2 * i + j])),
                  pl.BlockSpec((None, None, ffh, d), lambda i, j, be, hs, fi, nu: (layer, be[i], hs[2 * i + j], 0))],
        out_specs=pl.BlockSpec((tm, d), lambda i, j, be, hs, fi, nu: (i, 0)),
        scratch_shapes=[pltpu.VMEM((2, d, ffh), BF16), pltpu.VMEM((2, d, ffh), BF16), pltpu.VMEM((2, ffh, d), BF16),
                        pltpu.VMEM((tm, d), F32)],
    )
    return pl.pallas_call(
        _moe_kernel,
        grid_spec=grid_spec,
        out_shape=jax.ShapeDtypeStruct((rows, d), BF16),
        compiler_params=_params(("arbitrary", "arbitrary")),
        name="moe_experts",
    )(block_e, half, first, n_used, xg, wg, wu, wd)


def _router_kernel(h_ref, wt_ref, b_ref, tri_ref, e_ref, g_ref, r_ref, cnt_ref, sig_ref, bia_ref, carry_ref):
    @pl.when(pl.program_id(0) == 0)
    def _():
        carry_ref[...] = jnp.zeros_like(carry_ref)

    tm = h_ref.shape[0]
    logits = lax.dot_general(wt_ref[...], h_ref[...], (((1,), (1,)), ((), ())),
                             preferred_element_type=F32, precision=lax.Precision.HIGHEST)
    sig = jax.nn.sigmoid(logits)
    sig_ref[...] = sig
    biased = sig + b_ref[...]
    n_chunks = tm // LANES
    for c in range(n_chunks):
        bia_ref[c] = biased[:, c * LANES:(c + 1) * LANES]
    cand = [jnp.concatenate([bia_ref[c, pl.ds(k, MOE_GROUPS, stride=MOE_PER_GROUP), :] for c in range(n_chunks)],
                            axis=1) for k in range(MOE_PER_GROUP)]

    def top_of(vals):
        best = functools.reduce(jnp.maximum, vals)
        idx = jnp.full(best.shape, MOE_PER_GROUP - 1, jnp.int32)
        for k in range(MOE_PER_GROUP - 2, -1, -1):
            idx = jnp.where(vals[k] == best, k, idx)
        return best, idx

    top1, idx1 = top_of(cand)
    top2, idx2 = top_of([jnp.where(idx1 == k, -jnp.inf, cand[k]) for k in range(MOE_PER_GROUP)])
    gsum = top1 + top2
    gi = lax.broadcasted_iota(jnp.int32, gsum.shape, 0)
    grp = jnp.min(jnp.where(gsum == jnp.max(gsum, axis=0, keepdims=True), gi, MOE_GROUPS), axis=0, keepdims=True)
    chosen = gi == grp
    e1 = grp * MOE_PER_GROUP + jnp.sum(jnp.where(chosen, idx1, 0), axis=0, keepdims=True)
    e2 = grp * MOE_PER_GROUP + jnp.sum(jnp.where(chosen, idx2, 0), axis=0, keepdims=True)
    ei = lax.broadcasted_iota(jnp.int32, (MOE_EXPERTS, tm), 0)
    oh1, oh2 = ei == e1, ei == e2
    sig = sig_ref[...]
    w1 = jnp.sum(jnp.where(oh1, sig, 0.0), axis=0, keepdims=True)
    w2 = jnp.sum(jnp.where(oh2, sig, 0.0), axis=0, keepdims=True)
    e_ref[...] = jnp.concatenate([e1, e2], axis=0)
    g_ref[...] = jnp.concatenate([w1, w2], axis=0) / (w1 + w2)
    both = jnp.where(oh1, 1.0, jnp.where(oh2, 1.0, 0.0))
    base = carry_ref[:, 0:1] + _dot(both.astype(BF16), tri_ref[...])
    r1 = jnp.sum(jnp.where(oh1, base, 0.0), axis=0, keepdims=True)
    r2 = jnp.sum(jnp.where(oh2, base, 0.0), axis=0, keepdims=True)
    r_ref[...] = jnp.concatenate([r1, r2], axis=0).astype(jnp.int32)
    carry_ref[...] += jnp.sum(both, axis=1, keepdims=True)
    cnt_ref[...] = carry_ref[...]


def _route(h, router_w, router_b, tm):
    t, d = h.shape
    e = router_w.shape[1]
    tri = (jnp.arange(tm)[:, None] < jnp.arange(tm)[None, :]).astype(BF16)
    tok = lambda dt: (pl.BlockSpec((MOE_TOPK, tm), lambda i: (0, i)), jax.ShapeDtypeStruct((MOE_TOPK, t), dt))
    (es, esd), (gs, gsd), (rs, rsd) = tok(jnp.int32), tok(F32), tok(jnp.int32)
    return pl.pallas_call(
        _router_kernel,
        grid=(t // tm,),
        in_specs=[pl.BlockSpec((tm, d), lambda i: (i, 0)), pl.BlockSpec((e, d), lambda i: (0, 0)),
                  pl.BlockSpec((e, 1), lambda i: (0, 0)), pl.BlockSpec((tm, tm), lambda i: (0, 0))],
        out_specs=[es, gs, rs, pl.BlockSpec((e, LANES), lambda i: (0, 0))],
        out_shape=[esd, gsd, rsd, jax.ShapeDtypeStruct((e, LANES), F32)],
        scratch_shapes=[pltpu.VMEM((e, tm), F32), pltpu.VMEM((tm // LANES, e, LANES), F32),
                        pltpu.VMEM((e, LANES), F32)],
        compiler_params=_params(("arbitrary",)),
        name="router",
    )(h, router_w.T, router_b.reshape(e, 1), tri)


def _moe_ffn(h, h_bf, router_w, router_b, wg, wu, wd, layer, tm):
    t, d = h.shape
    expert, gate, rank, counts = _route(h, router_w, router_b, min(512, t))
    n_blocks = t * MOE_TOPK // tm + MOE_EXPERTS
    rows = n_blocks * tm
    counts = counts[:, 0].astype(jnp.int32)
    padded = (counts + tm - 1) // tm * tm
    pad_end = jnp.cumsum(padded)
    onehot = expert[..., None] == jnp.arange(MOE_EXPERTS, dtype=jnp.int32)
    pos = rank + jnp.sum(jnp.where(onehot, pad_end - padded, 0), axis=-1)
    tok = jnp.broadcast_to(jnp.arange(t, dtype=jnp.int32), (MOE_TOPK, t))
    buf_tok = (jnp.arange(rows, dtype=jnp.int32) % t).at[pos.reshape(-1)].set(tok.reshape(-1), unique_indices=True)
    block_start = jnp.arange(n_blocks, dtype=jnp.int32) * tm
    block_e = jnp.minimum(jnp.sum((pad_end[None, :] <= block_start[:, None]).astype(jnp.int32), axis=1),
                          MOE_EXPERTS - 1)
    n_used = (pad_end[-1:] // tm).astype(jnp.int32)
    xg = h_bf[buf_tok]
    yb = _moe_blocks(block_e, n_used, xg, wg, wu, wd, layer, tm)
    return yb[pos], gate.T


def _pad_cols(w, n):
    return jnp.pad(w, ((0, 0), (0, n - w.shape[1])))


def kernel(x, router_w, router_b, even_w_in, even_w_out, diff_lam_q1, diff_lam_k1, diff_lam_q2, diff_lam_k2,
           diff_subln_g, ssm_conv_w, ssm_conv_b, ssm_dt_bias, ssm_a_log, ssm_d, ssm_norm_g, odd_w_in,
           odd_w_out, nsa_pe_k, nsa_pe_v, nsa_ck_w1, nsa_ck_w2, nsa_cv_w1, nsa_cv_w2, ln_mix_g, ln_mix_b,
           ln_ffn_g, ln_ffn_b, moe_w_gate, moe_w_up, moe_w_down):
    b, s, d = x.shape
    t = b * s
    xf = x.reshape(t, d)
    xb = xf.astype(BF16)
    for layer in range(DEPTH):
        i = layer // 2
        if layer % 2 == 0:
            lam_init = 0.8 - 0.6 * math.exp(-0.3 * layer)
            w_in = even_w_in[i]
            n_attn = 3 * DIFF_WIDTH
            col_scale = jnp.where(jnp.arange(n_attn) < DIFF_WIDTH, DIFF_HEAD_DIM ** -0.5 * LOG2E, 1.0)
            qkv = _matmul(xb, (w_in[:, :n_attn] * col_scale).astype(BF16), BF16, 1024, 1024).reshape(b, s, n_attn)
            n_ssm = -(-(w_in.shape[1] - n_attn) // LANES) * LANES
            pf = _matmul(xb, _pad_cols(w_in[:, n_attn:], n_ssm).astype(BF16), F32, 1024, n_ssm // 3)
            pf = pf.reshape(b, s, n_ssm)
            y_attn = _diff_attention(qkv, diff_lam_q1[i], diff_lam_k1[i], diff_lam_q2[i], diff_lam_k2[i],
                                     diff_subln_g[i], lam_init, 512, 512)
            y_ssm = _mamba2_mixer(pf, ssm_conv_w[i], ssm_conv_b[i], ssm_dt_bias[i], ssm_a_log[i], ssm_d[i],
                                  ssm_norm_g[i])
            mix_in = jnp.concatenate([y_attn, y_ssm], axis=-1).reshape(t, -1)
            w_out = even_w_out[i]
        else:
            w_in = odd_w_in[i]
            n_a = NSA_Q_WIDTH + 4 * NSA_KV_WIDTH
            wq = w_in[:, :NSA_Q_WIDTH] * (NSA_HEAD_DIM ** -0.5 * LOG2E)
            wkv =w_in[:, NSA_Q_WIDTH:NSA_Q_WIDTH + 6 * NSA_KV_WIDTH]
            wgate = w_in[:, NSA_Q_WIDTH + 6 * NSA_KV_WIDTH:]
            w_a = jnp.concatenate([wq, wkv[:, 2 * NSA_KV_WIDTH:]], axis=1)
            wgate = jnp.pad(wgate.reshape(d, NSA_KV_GROUPS, 3 * NSA_REP),
                            ((0, 0), (0, 0), (0, LANES - 3 * NSA_REP))).reshape(d, NSA_KV_GROUPS * LANES)
            w_b = jnp.concatenate([wkv[:, :2 * NSA_KV_WIDTH], wgate], axis=1)
            a_bf = _matmul(xb, w_a.astype(BF16), BF16, 1024, 1024).reshape(b, s, n_a)
            b_f32 = _matmul(xb, w_b.astype(BF16), F32, 1024, 512).reshape(b, s, -1)
            k_cmp, v_cmp = _compress(b_f32, nsa_pe_k[i], nsa_pe_v[i], nsa_ck_w1[i], nsa_cv_w1[i],
                                     nsa_ck_w2[i], nsa_cv_w2[i])
            mix_in = _nsa_attention(a_bf, b_f32, k_cmp, v_cmp, 256, 512).reshape(t, -1)
            w_out = odd_w_out[i]
        h, h_bf = _matmul_res_ln(mix_in, w_out.astype(BF16), xf, ln_mix_g[layer], ln_mix_b[layer], 256)
        y2, gates = _moe_ffn(h, h_bf, router_w, router_b, moe_w_gate, moe_w_up, moe_w_down, layer, 256)
        xf, xb = _combine_ln(h, y2, gates, ln_ffn_g[layer], ln_ffn_b[layer], 256)
    return xf.reshape(b, s, d)
```

```python
import functools
import math

import jax
import jax.numpy as jnp
import numpy as np
from jax import lax
from jax.experimental import pallas as pl
from jax.experimental.pallas import tpu as pltpu

F32 = jnp.float32
BF16 = jnp.bfloat16

D_MODEL = 2048
DEPTH = 2
DEEPNORM_ALPHA = (2.0 * DEPTH) ** 0.25
LN_EPS = 1e-5

DIFF_HEADS = 8
DIFF_HEAD_DIM = 64
DIFF_WIDTH = DIFF_HEADS * 2 * DIFF_HEAD_DIM

SSM_INNER = D_MODEL // 2
SSM_HEAD_DIM = 64
SSM_HEADS = SSM_INNER // SSM_HEAD_DIM
SSM_GROUPS = 2
SSM_STATE = 128
SSM_CONV = 4
SSM_CHUNK = 128
SSM_CONV_CH = SSM_INNER + 2 * SSM_GROUPS * SSM_STATE

NSA_HEADS = 16
NSA_KV_GROUPS = 4
NSA_REP = NSA_HEADS // NSA_KV_GROUPS
NSA_HEAD_DIM = 128
NSA_CMP_LEN = 32
NSA_CMP_STRIDE = 16
NSA_SEL_LEN = 64
NSA_SEL_COUNT = 16
NSA_WINDOW = 512
NSA_Q_WIDTH = NSA_HEADS * NSA_HEAD_DIM
NSA_KV_WIDTH = NSA_KV_GROUPS * NSA_HEAD_DIM

MOE_GROUPS = 8
MOE_PER_GROUP = 4
MOE_EXPERTS = MOE_GROUPS * MOE_PER_GROUP
MOE_TOPK = 2
MOE_FF = 1024

NEG = -1e30
BIG = 1e30
LANES = 128
ONES_ROWS = 16
LOG2E = math.log2(math.e)
VMEM_LIMIT = 56 * 1024 * 1024


def _params(sem):
    return pltpu.CompilerParams(dimension_semantics=sem, vmem_limit_bytes=VMEM_LIMIT)


def _dot(a, b):
    return jnp.dot(a, b, preferred_element_type=F32)


def _dot_nt(a, b):
    return lax.dot_general(a, b, (((1,), (1,)), ((), ())), preferred_element_type=F32)


def _mm_kernel(x_ref, w_ref, o_ref):
    o_ref[...] = _dot(x_ref[...], w_ref[...]).astype(o_ref.dtype)


def _matmul(x, w, out_dtype, tm, tn):
    m, k = x.shape
    n = w.shape[1]
    return pl.pallas_call(
        _mm_kernel,
        grid=(n // tn, m // tm),
        in_specs=[pl.BlockSpec((tm, k), lambda j, i: (i, 0)),
                  pl.BlockSpec((k, tn), lambda j, i: (0, j))],
        out_specs=pl.BlockSpec((tm, tn), lambda j, i: (i, j)),
        out_shape=jax.ShapeDtypeStruct((m, n), out_dtype),
        compiler_params=_params(("parallel", "parallel")),
        name="proj_matmul",
    )(x, w)


def _layer_norm_rows(y, g, b):
    mu = jnp.mean(y, -1, keepdims=True)
    yc = y - mu
    var = jnp.mean(yc * yc, -1, keepdims=True)
    return yc * lax.rsqrt(var + LN_EPS) * g + b


def _mm_ln_kernel(a_ref, w_ref, res_ref, g_ref, b_ref, o_ref, ob_ref):
    y = _dot(a_ref[...], w_ref[...]) + DEEPNORM_ALPHA * res_ref[...]
    h = _layer_norm_rows(y, g_ref[...], b_ref[...])
    o_ref[...] = h
    ob_ref[...] = h.astype(BF16)


def _matmul_res_ln(a, w, res, g, b, tm):
    m, k = a.shape
    n = w.shape[1]
    return pl.pallas_call(
        _mm_ln_kernel,
        grid=(m // tm,),
        in_specs=[pl.BlockSpec((tm, k), lambda i: (i, 0)),
                  pl.BlockSpec((k, n), lambda i: (0, 0)),
                  pl.BlockSpec((tm, n), lambda i: (i, 0)),
                  pl.BlockSpec((1, n), lambda i: (0, 0)),
                  pl.BlockSpec((1, n), lambda i: (0, 0))],
        out_specs=[pl.BlockSpec((tm, n), lambda i: (i, 0)),
                   pl.BlockSpec((tm, n), lambda i: (i, 0))],
        out_shape=[jax.ShapeDtypeStruct((m, n), F32), jax.ShapeDtypeStruct((m, n), BF16)],
        compiler_params=_params(("parallel",)),
        name="outproj_ln",
    )(a, w, res, g.reshape(1, n), b.reshape(1, n))


def _combine_ln_kernel(h_ref, y_ref, w_ref, g_ref, b_ref, o_ref, ob_ref):
    w = w_ref[...]
    y = DEEPNORM_ALPHA * h_ref[...] + w[:, 0:1] * y_ref[0].astype(F32) + w[:, 1:2] * y_ref[1].astype(F32)
    x = _layer_norm_rows(y, g_ref[...], b_ref[...])
    o_ref[...] = x
    ob_ref[...] = x.astype(BF16)


def _combine_ln(h, y2, w, g, b, tm):
    m, n = h.shape
    row = pl.BlockSpec((tm, n), lambda i: (i, 0))
    vec = pl.BlockSpec((1, n), lambda i: (0, 0))
    return pl.pallas_call(
        _combine_ln_kernel,
        grid=(m // tm,),
        in_specs=[row, pl.BlockSpec((MOE_TOPK, tm, n), lambda i: (0, i, 0)),
                  pl.BlockSpec((tm, MOE_TOPK), lambda i: (i, 0)), vec, vec],
        out_specs=[row, row],
        out_shape=[jax.ShapeDtypeStruct((m, n), F32), jax.ShapeDtypeStruct((m, n), BF16)],
        compiler_params=_params(("parallel",)),
        name="ffn_combine_ln",
    )(h, y2, w, g.reshape(1, n), b.reshape(1, n))


def _softmax_tile(st_ref, pt_ref, m_ref, al_ref, *, mask=None):
    rows = st_ref.shape[1]
    for c in range(rows // LANES):
        cs = slice(c * LANES, (c + 1) * LANES)
        s = st_ref[:, cs]
        if mask is not None:
            s = mask(s, c)
        m_old = m_ref[:, cs]
        m_new = jnp.maximum(m_old, jnp.max(s, axis=0, keepdims=True))
        pt_ref[:, cs] = jnp.exp2(s - m_new[0:1]).astype(BF16)
        al_ref[:, cs] = jnp.exp2(m_old - m_new)
        m_ref[:, cs] = m_new


def _pv_update(vt_aug, pt_ref, al_ref, acc_ref):
    acc_ref[...] = acc_ref[...] * al_ref[0:1, :] + _dot(vt_aug, pt_ref[...])


def _flash_loop(n, scores, softmax, update):
    scores(0, 0)

    @pl.when(n == 1)
    def _():
        softmax(0, 0, True)
        update(0, 0)

    @pl.when(n >= 2)
    def _():
        scores(1, 1)
        softmax(0, 0, False)
        n_pairs = (n - 2) // 2

        def pair(u, carry):
            j = 2 * u
            scores(j + 2, 0)
            softmax(j + 1, 1, False)
            update(j, 0)
            scores(j + 3, 1)
            softmax(j + 2, 0, False)
            update(j + 1, 1)
            return carry

        lax.fori_loop(0, n_pairs, pair, 0)
        j = 2 * n_pairs

        @pl.when(n - 1 == j + 1)
        def _():
            softmax(j + 1, 1, True)
            update(j, 0)
            update(j + 1, 1)

        @pl.when(n - 1 == j + 2)
        def _():
            scores(j + 2, 0)
            softmax(j + 1, 1, False)
            update(j, 0)
            softmax(j + 2, 0, True)
            update(j + 1, 1)
            update(j + 2, 0)


def _transpose_values(v, tk):
    b, s, h, dv = v.shape
    vt = jnp.transpose(v.reshape(b, s // tk, tk, h, dv), (0, 3, 1, 4, 2))
    return jnp.concatenate([vt, jnp.ones((b, h, s // tk, ONES_ROWS, tk), v.dtype)], axis=3)


def _diff_kernel(lq1_ref, lk1_ref, lq2_ref, lk2_ref, g_ref, q_ref, k_ref, vt_ref, o_ref,
                 q2_ref, st0_ref, st1_ref, pt0_ref, pt1_ref, m_ref, al0_ref, al1_ref, acc_ref, *, tq, tk, lam_init):
    i = pl.program_id(2)
    st, pt, al = (st0_ref, st1_ref), (pt0_ref, pt1_ref), (al0_ref, al1_ref)
    qt = q_ref[...].astype(F32).T
    half = lax.broadcasted_iota(jnp.int32, (LANES, tq), 0) < DIFF_HEAD_DIM
    q2_ref[...] = jnp.concatenate([jnp.where(half, qt, 0.0), jnp.where(half, 0.0, qt)], axis=1).astype(BF16)
    m_ref[...] = jnp.full_like(m_ref, NEG)
    acc_ref[...] = jnp.zeros_like(acc_ref)

    def scores(j, slot):
        kt = k_ref[pl.ds(pl.multiple_of(j * tk, tk), tk), :]
        st[slot][...] = _dot(kt, q2_ref[...])

    def softmax(j, slot, masked):
        def causal(s, c):
            key = j * tk + lax.broadcasted_iota(jnp.int32, (tk, LANES), 0)
            qry = i * tq + ((c * LANES + lax.broadcasted_iota(jnp.int32, (tk, LANES), 1)) & (tq - 1))
            return jnp.where(key <= qry, s, NEG)
        _softmax_tile(st[slot], pt[slot], m_ref, al[slot], mask=causal if masked else None)

    def update(j, slot):
        _pv_update(vt_ref[j], pt[slot], al[slot], acc_ref)

    _flash_loop((i * tq + tq + tk - 1) // tk, scores, softmax, update)
    acc = acc_ref[...]
    o = (acc[:LANES] / acc[LANES:LANES + 1]).T
    lam = (jnp.exp(jnp.sum(lq1_ref[...] * lk1_ref[...], -1, keepdims=True))
           - jnp.exp(jnp.sum(lq2_ref[...] * lk2_ref[...], -1, keepdims=True)) + lam_init)
    od = o[:tq] - lam * o[tq:]
    y = od * lax.rsqrt(jnp.mean(od * od, -1, keepdims=True) + 1e-5) * g_ref[...]
    o_ref[...] = (y * (1.0 - lam_init)).astype(o_ref.dtype)


def _diff_attention(qkv, lq1, lk1, lq2, lk2, subln_g, lam_init, tq, tk):
    b, s, _ = qkv.shape
    rows = 2 * tq
    dv = 2 * DIFF_HEAD_DIM
    assert tk % tq == 0 and s % tk == 0
    vt = _transpose_values(qkv[..., 2 * DIFF_WIDTH:].reshape(b, s, DIFF_HEADS, dv), tk)
    vec64 = pl.BlockSpec((1, DIFF_HEAD_DIM), lambda bb, h, i: (0, 0))
    return pl.pallas_call(
        functools.partial(_diff_kernel, tq=tq, tk=tk, lam_init=lam_init),
        grid=(b, DIFF_HEADS, s // tq),
        in_specs=[vec64, vec64, vec64, vec64,
                  pl.BlockSpec((1, LANES), lambda bb, h, i: (0, 0)),
                  pl.BlockSpec((None, tq, LANES), lambda bb, h, i: (bb, i, h)),
                  pl.BlockSpec((None, s, LANES), lambda bb, h, i: (bb, 0, DIFF_HEADS + h)),
                  pl.BlockSpec((None, None, s // tk, dv + ONES_ROWS, tk), lambda bb, h, i: (bb, h, 0, 0, 0))],
        out_specs=pl.BlockSpec((None, tq, LANES), lambda bb, h, i: (bb, i, h)),
        out_shape=jax.ShapeDtypeStruct((b, s, DIFF_WIDTH), BF16),
        scratch_shapes=[pltpu.VMEM((LANES, rows), BF16), pltpu.VMEM((tk, rows), F32), pltpu.VMEM((tk, rows), F32),
                        pltpu.VMEM((tk, rows), BF16), pltpu.VMEM((tk, rows), BF16),
                        pltpu.VMEM((8, rows), F32), pltpu.VMEM((8, rows), F32), pltpu.VMEM((8, rows), F32),
                        pltpu.VMEM((dv + ONES_ROWS, rows), F32)],
        compiler_params=_params(("parallel", "parallel", "arbitrary")),
        name="diff_attention",
    )(lq1.reshape(1, -1), lk1.reshape(1, -1), lq2.reshape(1, -1), lk2.reshape(1, -1),
      subln_g.reshape(1, -1), qkv, qkv, vt)


def _ssd_kernel(pf_ref, cw_ref, cb_ref, dtb_ref, alog_ref, drep_ref, ng_ref, tri_ref, o_ref,
                xe_ref, tail_ref, state_ref, y_ref):
    L = SSM_CHUNK
    halo = 8

    @pl.when(pl.program_id(1) == 0)
    def _():
        tail_ref[...] = jnp.zeros_like(tail_ref)
        state_ref[...] = jnp.zeros_like(state_ref)

    z = pf_ref[:, 0:SSM_INNER]
    xbc = pf_ref[:, SSM_INNER:SSM_INNER + SSM_CONV_CH]
    dt_raw = pf_ref[:, SSM_INNER + SSM_CONV_CH:SSM_INNER + SSM_CONV_CH + LANES]
    xe_ref[0:halo, :] = tail_ref[...]
    xe_ref[halo:halo + L, :] = xbc
    tail_ref[...] = xbc[L - halo:L, :]
    conv = cb_ref[...] + cw_ref[SSM_CONV - 1:SSM_CONV, :] * xbc
    for k in range(SSM_CONV - 1):
        conv = conv + cw_ref[k:k + 1, :] * xe_ref[pl.ds(halo - (SSM_CONV - 1) + k, L), :]
    xbc = conv * jax.nn.sigmoid(conv)
    xs = xbc[:, :SSM_INNER]
    dt = jax.nn.softplus(dt_raw + dtb_ref[...])
    adt = dt * (-jnp.exp(alog_ref[...]))
    acum = jnp.dot(tri_ref[...], adt, preferred_element_type=F32, precision=lax.Precision.HIGHEST)
    acum_t = acum.T
    left = lax.broadcasted_iota(jnp.int32, (1, LANES), 1) < SSM_HEAD_DIM
    tril = lax.broadcasted_iota(jnp.int32, (L, L), 0) >= lax.broadcasted_iota(jnp.int32, (L, L), 1)
    pairs_per_group = SSM_HEADS // SSM_GROUPS // 2
    for g in range(SSM_GROUPS):
        bm = xbc[:, SSM_INNER + g * SSM_STATE:SSM_INNER + (g + 1) * SSM_STATE]
        cm = xbc[:, SSM_INNER + (SSM_GROUPS + g) * SSM_STATE:SSM_INNER + (SSM_GROUPS + g + 1) * SSM_STATE]
        cb = _dot_nt(cm.astype(BF16), bm.astype(BF16))
        bm_t = bm.T
        for qq in range(pairs_per_group):
            q = g * pairs_per_group + qq
            ls = slice(q * LANES, (q + 1) * LANES)
            xs_p = xs[:, ls]
            h0, h1 = 2 * q, 2 * q + 1
            xd = xs_p * jnp.where(left, dt[:, h0:h0 + 1], dt[:, h1:h1 + 1])
            prev = state_ref[q]
            y = drep_ref[:, ls] * xs_p
            new = jnp.zeros((SSM_STATE, LANES), F32)
            for h, keep in ((h0, left), (h1, jnp.logical_not(left))):
                xd_h = jnp.where(keep, xd, 0.0).astype(BF16)
                prev_h = jnp.where(keep, prev, 0.0).astype(BF16)
                cs_col = acum[:, h:h + 1]
                cs_row = acum_t[h:h + 1, :]
                w = cb * jnp.exp(jnp.where(tril, cs_col - cs_row, -jnp.inf))
                y = y + _dot(w.astype(BF16), xd_h) + _dot((cm * jnp.exp(cs_col)).astype(BF16), prev_h)
                new = new + _dot((bm_t * jnp.exp(acum[L - 1:L, h:h + 1] - cs_row)).astype(BF16), xd_h)
            decay = jnp.where(left, jnp.exp(acum[L - 1:L, h0:h0 + 1]), jnp.exp(acum[L - 1:L, h1:h1 + 1]))
            state_ref[q] = prev * decay + new
            y_ref[:, ls] = y
    y = y_ref[...] * (z * jax.nn.sigmoid(z))
    gw = SSM_INNER // SSM_GROUPS
    for g in range(SSM_GROUPS):
        yg = y[:, g * gw:(g + 1) * gw]
        yg = yg * lax.rsqrt(jnp.mean(yg * yg, -1, keepdims=True) + 1e-5) * ng_ref[:, g * gw:(g + 1) * gw]
        o_ref[:, g * gw:(g + 1) * gw] = yg.astype(o_ref.dtype)


def _mamba2_mixer(pf, conv_w, conv_b, dt_bias, a_log, d_skip, norm_g):
    b, s, w = pf.shape
    L = SSM_CHUNK
    pad_heads = lambda v: jnp.pad(v, (0, LANES - SSM_HEADS)).reshape(1, LANES)
    tri = jnp.tril(jnp.ones((L, L), F32))
    full = lambda shape: pl.BlockSpec(shape, lambda bb, c: (0,) * len(shape))
    return pl.pallas_call(
        _ssd_kernel,
        grid=(b, s // L),
        in_specs=[pl.BlockSpec((None, L, w), lambda bb, c: (bb, c, 0)),
                  full((SSM_CONV, SSM_CONV_CH)), full((1, SSM_CONV_CH)), full((1, LANES)), full((1, LANES)),
                  full((1, SSM_INNER)), full((1, SSM_INNER)), full((L, L))],
        out_specs=pl.BlockSpec((None, L, SSM_INNER), lambda bb, c: (bb, c, 0)),
        out_shape=jax.ShapeDtypeStruct((b, s, SSM_INNER), BF16),
        scratch_shapes=[pltpu.VMEM((L + 8, SSM_CONV_CH), F32), pltpu.VMEM((8, SSM_CONV_CH), F32),
                        pltpu.VMEM((SSM_HEADS // 2, SSM_STATE, LANES), F32), pltpu.VMEM((L, SSM_INNER), F32)],
        compiler_params=_params(("parallel", "arbitrary")),
        name="ssd_mixer",
    )(pf, conv_w, conv_b.reshape(1, -1), pad_heads(dt_bias), pad_heads(a_log),
      jnp.repeat(d_skip, SSM_HEAD_DIM).reshape(1, -1), norm_g.reshape(1, -1), tri)


def _cmp_kernel(x_ref, pek_ref, pev_ref, w1k_ref, w1v_ref, w2k_ref, w2v_ref, ko_ref, vo_ref,
                top_ref, bot_ref, *, nb):
    l = pl.program_id(1)
    g4 = NSA_KV_GROUPS

    @pl.when(l == 0)
    def _():
        top_ref[...] = jnp.zeros_like(top_ref)
        bot_ref[...] = jnp.zeros_like(bot_ref)

    x = x_ref[...]
    for kv, (pe_ref, w1_ref) in enumerate(((pek_ref, w1k_ref), (pev_ref, w1v_ref))):
        x4 = jnp.concatenate([x[:, kv * NSA_KV_WIDTH + g * LANES: kv * NSA_KV_WIDTH + (g + 1) * LANES]
                              for g in range(g4)], axis=0)
        top_ref[kv] += _dot((x4 + pe_ref[pl.ds(l, 1), :]).astype(BF16), w1_ref[l])
        bot_ref[kv] += _dot((x4 + pe_ref[pl.ds(l + NSA_CMP_STRIDE, 1), :]).astype(BF16),
                            w1_ref[l + NSA_CMP_STRIDE])

    @pl.when(l == NSA_CMP_STRIDE - 1)
    def _():
        last = lax.broadcasted_iota(jnp.int32, (nb, LANES), 0) == nb - 1
        for kv, (w2_ref, o_ref) in enumerate(((w2k_ref, ko_ref), (w2v_ref, vo_ref))):
            for g in range(g4):
                top = top_ref[kv, g * nb:(g + 1) * nb, :]
                bot = bot_ref[kv, g * nb:(g + 1) * nb, :]
                hid = top + pltpu.roll(bot, nb - 1, 0)
                out = _dot(jax.nn.gelu(hid, approximate=True).astype(BF16), w2_ref[...])
                o_ref[g] = jnp.where(last, 0.0, out).astype(o_ref.dtype)


def _compress(bf, pe_k, pe_v, w1k, w1v, w2k, w2v):
    b, s, w = bf.shape
    nb = s // NSA_CMP_STRIDE
    xv = bf.reshape(b, nb, NSA_CMP_STRIDE * w)
    full = lambda shape: pl.BlockSpec(shape, lambda bb, l: (0,) * len(shape))
    out_spec = pl.BlockSpec((None, NSA_KV_GROUPS, nb, LANES), lambda bb, l: (bb, 0, 0, 0))
    out_sd = jax.ShapeDtypeStruct((b, NSA_KV_GROUPS, nb, LANES), BF16)
    w1 = lambda a: a.astype(BF16).reshape(NSA_CMP_LEN, NSA_HEAD_DIM, -1)
    return pl.pallas_call(
        functools.partial(_cmp_kernel, nb=nb),
        grid=(b, NSA_CMP_STRIDE),
        in_specs=[pl.BlockSpec((None, nb, w), lambda bb, l: (bb, 0, l)),
                  full((NSA_CMP_LEN, LANES)), full((NSA_CMP_LEN, LANES)),
                  full((NSA_CMP_LEN, NSA_HEAD_DIM, LANES)), full((NSA_CMP_LEN, NSA_HEAD_DIM, LANES)),
                  full((LANES, LANES)), full((LANES, LANES))],
        out_specs=[out_spec, out_spec],
        out_shape=[out_sd, out_sd],
        scratch_shapes=[pltpu.VMEM((2, NSA_KV_GROUPS * nb, LANES), F32),
                        pltpu.VMEM((2, NSA_KV_GROUPS * nb, LANES), F32)],
        compiler_params=_params(("parallel", "arbitrary")),
        name="nsa_compress",
    )(xv, pe_k, pe_v, w1(w1k), w1(w1v), w2k.astype(BF16), w2v.astype(BF16))


def _nsa_kernel(q_ref, gate_ref, kc_ref, vct_ref, ks_ref, vst_ref, kw_ref, vwt_ref, blk_ref, ovt_ref, o_ref,
                qa_ref, sc_ref, pc_ref, ps_ref, oc_ref, imp_ref, st0_ref, st1_ref, pt0_ref, pt1_ref, sw_ref, pw_ref,
                m_ref, al0_ref, al1_ref, acc_ref, mw_ref, alw_ref, accw_ref, *, tq, tk, seq, slab):
    i = pl.program_id(2)
    t0 = i * tq
    rep = NSA_REP
    nq = tq // LANES
    ncp = seq // NSA_CMP_STRIDE
    lane = lax.broadcasted_iota(jnp.int32, (1, LANES), 1)
    t_of = lambda c: t0 + (c % nq) * LANES + lane
    q = q_ref[...].astype(F32)
    for r in range(rep):
        qa_ref[0:LANES, r * tq:(r + 1) * tq] = q[:, r * LANES:(r + 1) * LANES].T.astype(BF16)
    qs = qa_ref[0:LANES, :]

    def compressed(nk):
        sc_ref[0:nk, :] = _dot(kc_ref[0:nk, :], qs)
        n_end = lax.broadcasted_iota(jnp.int32, (nk, LANES), 0) * NSA_CMP_STRIDE + (NSA_CMP_LEN - 1)
        for c in range(rep * nq):
            cs = slice(c * LANES, (c + 1) * LANES)
            qsl = slice((c % nq) * LANES, (c % nq + 1) * LANES)
            s = jnp.where(n_end <= t_of(c), sc_ref[0:nk, cs], NEG)
            e = jnp.exp2(s - jnp.max(s, axis=0, keepdims=True))
            any_visible = t_of(c) >= NSA_CMP_LEN - 1
            p = e * jnp.where(any_visible, 1.0 / jnp.sum(e, axis=0, keepdims=True), 0.0)
            pc_ref[0:nk, cs] = p.astype(BF16)
            ps_ref[0:nk, qsl] = p if c < nq else ps_ref[0:nk, qsl] + p
        oc_ref[...] = _dot(vct_ref[:, 0:nk], pc_ref[0:nk, :])
        ps = ps_ref[0:nk, :]
        p_hi = ps.astype(BF16)
        p_lo = (ps - p_hi.astype(F32)).astype(BF16)
        imp_ref[...] = _dot(ovt_ref[:, 0:nk], p_hi) + _dot(ovt_ref[:, 0:nk], p_lo)

    n_visible = t0 // NSA_CMP_STRIDE + (tq - NSA_CMP_LEN) // NSA_CMP_STRIDE + 1
    for v in range(ncp // LANES):
        @pl.when((n_visible - 1) // LANES == v)
        def _():
            compressed((v + 1) * LANES)
    o_cmp_t = oc_ref[...]
    mi = lax.broadcasted_iota(jnp.int32, (LANES, LANES), 0)
    mf = mi.astype(F32)
    for qc in range(nq):
        cur = t_of(qc) >> 6
        forced = (mi == 0) | (mi == cur) | (mi == cur - 1)
        vals = jnp.where(forced, -jnp.inf, jnp.where(mi <= cur, imp_ref[:, qc * LANES:(qc + 1) * LANES], NEG))
        pen = jnp.where(forced, 0.0, -1e9)
        for _ in range(NSA_SEL_COUNT - 3):
            mx = jnp.max(vals, axis=0, keepdims=True)
            first = jnp.min(jnp.where(vals == mx, mf, float(LANES)), axis=0, keepdims=True)
            hit = mf == first
            pen = jnp.where(hit, 0.0, pen)
            vals = jnp.where(hit, -jnp.inf, vals)
        pen_b = pen.astype(BF16)
        for r in range(rep):
            qa_ref[LANES:2 * LANES, r * tq + qc * LANES:r * tq + (qc + 1) * LANES] = pen_b

    ws = pl.multiple_of(jnp.clip(t0 - NSA_WINDOW, 0, seq - slab), LANES)
    sw_ref[...] = _dot(kw_ref[pl.ds(ws, slab), :], qs)
    mw_ref[...] = jnp.full_like(mw_ref, NEG)
    accw_ref[...] = jnp.zeros_like(accw_ref)
    jb = ws // LANES
    v_slab = jnp.concatenate([vwt_ref[jb + u] for u in range(slab // LANES)], axis=1)

    def window(s, c):
        kpos = ws + lax.broadcasted_iota(jnp.int32, (slab, LANES), 0)
        return jnp.where(kpos <= t_of(c), jnp.where(kpos > t_of(c) - NSA_WINDOW, s, NEG), NEG)

    _softmax_tile(sw_ref, pw_ref, mw_ref, alw_ref, mask=window)
    _pv_update(v_slab, pw_ref, alw_ref, accw_ref)
    accw = accw_ref[...]
    o_win_t = accw[:LANES] / accw[LANES:LANES + 1]

    m_ref[...] = jnp.full_like(m_ref, NEG)
    acc_ref[...] = jnp.zeros_like(acc_ref)
    st, pt, al = (st0_ref, st1_ref), (pt0_ref, pt1_ref), (al0_ref, al1_ref)

    def scores(j, slot):
        k0 = pl.multiple_of(j * tk, tk)
        ka = jnp.concatenate([ks_ref[pl.ds(k0, tk), :], blk_ref[pl.ds(k0, tk), :]], axis=1)
        st[slot][...] = _dot(ka, qa_ref[...])

    def softmax(j, slot, masked):
        def causal(s, c):
            kpos = j * tk + lax.broadcasted_iota(jnp.int32, (tk, LANES), 0)
            return jnp.where(kpos <= t_of(c), s, NEG)
        _softmax_tile(st[slot], pt[slot], m_ref, al[slot], mask=causal if masked else None)

    def update(j, slot):
        _pv_update(vst_ref[j], pt[slot], al[slot], acc_ref)

    _flash_loop((t0 + tq + tk - 1) // tk, scores, softmax, update)
    acc = acc_ref[...]
    o_sel_t = acc[:LANES] / acc[LANES:LANES + 1]

    g_t = jax.nn.sigmoid(gate_ref[...]).T
    for r in range(rep):
        cs = slice(r * tq, (r + 1) * tq)
        out_t = (g_t[3 * r:3 * r + 1] * o_cmp_t[:, cs] + g_t[3 * r + 1:3 * r + 2] * o_sel_t[:, cs]
                 + g_t[3 * r + 2:3 * r + 3] * o_win_t[:, cs])
        o_ref[:, r * LANES:(r + 1) * LANES] = out_t.T.astype(o_ref.dtype)


def _nsa_attention(a_bf, b_f32, k_cmp, v_cmp, tq, tk):
    b, s, _ = a_bf.shape
    g4 = NSA_KV_GROUPS
    rows = NSA_REP * tq
    ncp = s // NSA_CMP_STRIDE
    slab = NSA_WINDOW + tq
    dva = NSA_HEAD_DIM + ONES_ROWS
    assert s >= slab and s % tk == 0 and tk % tq == 0 and tq % LANES == 0 and (s - slab) % LANES == 0
    blk = (jnp.arange(s)[:, None] // NSA_SEL_LEN == jnp.arange(LANES)[None, :]).astype(BF16)
    cs = np.arange(ncp)[:, None] * NSA_CMP_STRIDE
    ss = np.arange(LANES)[None, :] * NSA_SEL_LEN
    ov = np.maximum(np.minimum(cs + NSA_CMP_LEN, ss + NSA_SEL_LEN) - np.maximum(cs, ss), 0) / NSA_CMP_LEN
    ov[ncp - 1] = 0.0
    qoff = NSA_Q_WIDTH // LANES
    group_values = lambda n: a_bf[..., NSA_Q_WIDTH + n * NSA_KV_WIDTH:NSA_Q_WIDTH + (n + 1) * NSA_KV_WIDTH].reshape(
        b, s, g4, NSA_HEAD_DIM)
    vst = _transpose_values(group_values(1), tk)
    vwt = _transpose_values(group_values(3), LANES)
    vct = jnp.swapaxes(v_cmp, 2, 3)
    k_spec = lambda n: pl.BlockSpec((None, s, LANES), lambda bb, g, i: (bb, 0, qoff + n * g4 + g))
    vt_spec = lambda t: pl.BlockSpec((None, None, s // t, dva, t), lambda bb, g, i: (bb, g, 0, 0, 0))
    return pl.pallas_call(
        functools.partial(_nsa_kernel, tq=tq, tk=tk, seq=s, slab=slab),
        grid=(b, g4, s // tq),
        in_specs=[pl.BlockSpec((None, tq, NSA_REP * LANES), lambda bb, g, i: (bb, i, g)),
                  pl.BlockSpec((None, tq, LANES), lambda bb, g, i: (bb, i, 2 * g4 + g)),
                  pl.BlockSpec((None, None, ncp, LANES), lambda bb, g, i: (bb, g, 0, 0)),
                  pl.BlockSpec((None, None, LANES, ncp), lambda bb, g, i: (bb, g, 0, 0)),
                  k_spec(0), vt_spec(tk), k_spec(2), vt_spec(LANES),
                  pl.BlockSpec((s, LANES), lambda bb, g, i: (0, 0)),
                  pl.BlockSpec((LANES, ncp), lambda bb, g, i: (0, 0))],
        out_specs=pl.BlockSpec((None, tq, NSA_REP * LANES), lambda bb, g, i: (bb, i, g)),
        out_shape=jax.ShapeDtypeStruct((b, s, NSA_Q_WIDTH), BF16),
        scratch_shapes=[pltpu.VMEM((2 * LANES, rows), BF16),
                        pltpu.VMEM((ncp, rows), F32), pltpu.VMEM((ncp, rows), BF16), pltpu.VMEM((ncp, tq), F32),
                        pltpu.VMEM((NSA_HEAD_DIM, rows), F32), pltpu.VMEM((LANES, tq), F32),
                        pltpu.VMEM((tk, rows), F32), pltpu.VMEM((tk, rows), F32),
                        pltpu.VMEM((tk, rows), BF16), pltpu.VMEM((tk, rows), BF16),
                        pltpu.VMEM((slab, rows), F32), pltpu.VMEM((slab, rows), BF16),
                        pltpu.VMEM((8, rows), F32), pltpu.VMEM((8, rows), F32), pltpu.VMEM((8, rows), F32),
                        pltpu.VMEM((dva, rows), F32),
                        pltpu.VMEM((8, rows), F32), pltpu.VMEM((8, rows), F32), pltpu.VMEM((dva, rows), F32)],
        compiler_params=_params(("parallel", "parallel", "arbitrary")),
        name="nsa_attention",
    )(a_bf, b_f32, k_cmp, vct, a_bf, vst, a_bf, vwt, blk, jnp.asarray(ov.T, BF16))


def _moe_kernel(be_ref, hs_ref, first_ref, nu_ref, x_ref, wg_ref, wu_ref, wd_ref, o_ref,
                wgb_ref, wub_ref, wdb_ref, part_ref):
    i, j = pl.program_id(0), pl.program_id(1)

    @pl.when(i < nu_ref[0])
    def _():
        @pl.when(first_ref[i] == 1)
        def _():
            wgb_ref[j] = wg_ref[...].astype(BF16)
            wub_ref[j] = wu_ref[...].astype(BF16)
            wdb_ref[j] = wd_ref[...].astype(BF16)

        x = x_ref[...]
        hid = jax.nn.silu(_dot(x, wgb_ref[j])) * _dot(x, wub_ref[j])
        part = _dot(hid.astype(BF16), wdb_ref[j])

        @pl.when(j == 0)
        def _():
            part_ref[...] = part

        @pl.when(j == 1)
        def _():
            o_ref[...] = (part_ref[...] + part).astype(o_ref.dtype)

    @pl.when(i >= nu_ref[0])
    def _():
        o_ref[...] = jnp.zeros_like(o_ref)


def _moe_blocks(block_e, n_used, xg, wg, wu, wd, layer, tm):
    rows, d = xg.shape
    ff = wg.shape[-1]
    ffh = ff // 2
    n_blocks = rows // tm
    blk = jnp.arange(n_blocks, dtype=jnp.int32)
    prev_e = jnp.concatenate([jnp.full((1,), -1, jnp.int32), block_e[:-1]])
    first = ((block_e != prev_e) & (blk < n_used[0])).astype(jnp.int32)
    half = jnp.where(first[:, None] == 1, jnp.arange(2, dtype=jnp.int32)[None, :], 1).reshape(-1)
    grid_spec = pltpu.PrefetchScalarGridSpec(
        num_scalar_prefetch=4,
        grid=(n_blocks, 2),
        in_specs=[pl.BlockSpec((tm, d), lambda i, j, be, hs, fi, nu: (i, 0)),
                  pl.BlockSpec((None, None, d, ffh), lambda i, j, be, hs, fi, nu: (layer, be[i], 0, hs[2 * i + j])),
                  pl.BlockSpec((None, None, d, ffh), lambda i, j, be, hs, fi, nu: (layer, be[i], 0, hs[2 * i + j])),
                  pl.BlockSpec((None, None, ffh, d), lambda i, j, be, hs, fi, nu: (layer, be[i], hs[2 * i + j], 0))],
        out_specs=pl.BlockSpec((tm, d), lambda i, j, be, hs, fi, nu: (i, 0)),
        scratch_shapes=[pltpu.VMEM((2, d, ffh), BF16), pltpu.VMEM((2, d, ffh), BF16), pltpu.VMEM((2, ffh, d), BF16),
                        pltpu.VMEM((tm, d), F32)],
    )
    return pl.pallas_call(
        _moe_kernel,
        grid_spec=grid_spec,
        out_shape=jax.ShapeDtypeStruct((rows, d), BF16),
        compiler_params=_params(("arbitrary", "arbitrary")),
        name="moe_experts",
    )(block_e, half, first, n_used, xg, wg, wu, wd)


def _router_kernel(h_ref, wt_ref, b_ref, tri_ref, e_ref, g_ref, r_ref, cnt_ref, sig_ref, bia_ref, carry_ref):
    @pl.when(pl.program_id(0) == 0)
    def _():
        carry_ref[...] = jnp.zeros_like(carry_ref)

    tm = h_ref.shape[0]
    logits = lax.dot_general(wt_ref[...], h_ref[...], (((1,), (1,)), ((), ())),
                             preferred_element_type=F32, precision=lax.Precision.HIGHEST)
    sig = jax.nn.sigmoid(logits)
    sig_ref[...] = sig
    biased = sig + b_ref[...]
    n_chunks = tm // LANES
    for c in range(n_chunks):
        bia_ref[c] = biased[:, c * LANES:(c + 1) * LANES]
    cand = [jnp.concatenate([bia_ref[c, pl.ds(k, MOE_GROUPS, stride=MOE_PER_GROUP), :] for c in range(n_chunks)],
                            axis=1) for k in range(MOE_PER_GROUP)]

    def top_of(vals):
        best = functools.reduce(jnp.maximum, vals)
        idx = jnp.full(best.shape, MOE_PER_GROUP - 1, jnp.int32)
        for k in range(MOE_PER_GROUP - 2, -1, -1):
            idx = jnp.where(vals[k] == best, k, idx)
        return best, idx

    top1, idx1 = top_of(cand)
    top2, idx2 = top_of([jnp.where(idx1 == k, -jnp.inf, cand[k]) for k in range(MOE_PER_GROUP)])
    gsum = top1 + top2
    gi = lax.broadcasted_iota(jnp.int32, gsum.shape, 0)
    grp = jnp.min(jnp.where(gsum == jnp.max(gsum, axis=0, keepdims=True), gi, MOE_GROUPS), axis=0, keepdims=True)
    chosen = gi == grp
    e1 = grp * MOE_PER_GROUP + jnp.sum(jnp.where(chosen, idx1, 0), axis=0, keepdims=True)
    e2 = grp * MOE_PER_GROUP + jnp.sum(jnp.where(chosen, idx2, 0), axis=0, keepdims=True)
    ei = lax.broadcasted_iota(jnp.int32, (MOE_EXPERTS, tm), 0)
    oh1, oh2 = ei == e1, ei == e2
    sig = sig_ref[...]
    w1 = jnp.sum(jnp.where(oh1, sig, 0.0), axis=0, keepdims=True)
    w2 = jnp.sum(jnp.where(oh2, sig, 0.0), axis=0, keepdims=True)
    e_ref[...] = jnp.concatenate([e1, e2], axis=0)
    g_ref[...] = jnp.concatenate([w1, w2], axis=0) / (w1 + w2)
    both = jnp.where(oh1, 1.0, jnp.where(oh2, 1.0, 0.0))
    base = carry_ref[:, 0:1] + _dot(both.astype(BF16), tri_ref[...])
    r1 = jnp.sum(jnp.where(oh1, base, 0.0), axis=0, keepdims=True)
    r2 = jnp.sum(jnp.where(oh2, base, 0.0), axis=0, keepdims=True)
    r_ref[...] = jnp.concatenate([r1, r2], axis=0).astype(jnp.int32)
    carry_ref[...] += jnp.sum(both, axis=1, keepdims=True)
    cnt_ref[...] = carry_ref[...]


def _route(h, router_w, router_b, tm):
    t, d = h.shape
    e = router_w.shape[1]
    tri = (jnp.arange(tm)[:, None] < jnp.arange(tm)[None, :]).astype(BF16)
    tok = lambda dt: (pl.BlockSpec((MOE_TOPK, tm), lambda i: (0, i)), jax.ShapeDtypeStruct((MOE_TOPK, t), dt))
    (es, esd), (gs, gsd), (rs, rsd) = tok(jnp.int32), tok(F32), tok(jnp.int32)
    return pl.pallas_call(
        _router_kernel,
        grid=(t // tm,),
        in_specs=[pl.BlockSpec((tm, d), lambda i: (i, 0)), pl.BlockSpec((e, d), lambda i: (0, 0)),
                  pl.BlockSpec((e, 1), lambda i: (0, 0)), pl.BlockSpec((tm, tm), lambda i: (0, 0))],
        out_specs=[es, gs, rs, pl.BlockSpec((e, LANES), lambda i: (0, 0))],
        out_shape=[esd, gsd, rsd, jax.ShapeDtypeStruct((e, LANES), F32)],
        scratch_shapes=[pltpu.VMEM((e, tm), F32), pltpu.VMEM((tm // LANES, e, LANES), F32),
                        pltpu.VMEM((e, LANES), F32)],
        compiler_params=_params(("arbitrary",)),
        name="router",
    )(h, router_w.T, router_b.reshape(e, 1), tri)


def _moe_ffn(h, h_bf, router_w, router_b, wg, wu, wd, layer, tm):
    t, d = h.shape
    expert, gate, rank, counts = _route(h, router_w, router_b, min(512, t))
    n_blocks = t * MOE_TOPK // tm + MOE_EXPERTS
    rows = n_blocks * tm
    counts = counts[:, 0].astype(jnp.int32)
    padded = (counts + tm - 1) // tm * tm
    pad_end = jnp.cumsum(padded)
    onehot = expert[..., None] == jnp.arange(MOE_EXPERTS, dtype=jnp.int32)
    pos = rank + jnp.sum(jnp.where(onehot, pad_end - padded, 0), axis=-1)
    tok = jnp.broadcast_to(jnp.arange(t, dtype=jnp.int32), (MOE_TOPK, t))
    buf_tok = (jnp.arange(rows, dtype=jnp.int32) % t).at[pos.reshape(-1)].set(tok.reshape(-1), unique_indices=True)
    block_start = jnp.arange(n_blocks, dtype=jnp.int32) * tm
    block_e = jnp.minimum(jnp.sum((pad_end[None, :] <= block_start[:, None]).astype(jnp.int32), axis=1),
                          MOE_EXPERTS - 1)
    n_used = (pad_end[-1:] // tm).astype(jnp.int32)
    xg = h_bf[buf_tok]
    yb = _moe_blocks(block_e, n_used, xg, wg, wu, wd, layer, tm)
    return yb[pos], gate.T


def _pad_cols(w, n):
    return jnp.pad(w, ((0, 0), (0, n - w.shape[1])))


def kernel(x, router_w, router_b, even_w_in, even_w_out, diff_lam_q1, diff_lam_k1, diff_lam_q2, diff_lam_k2,
           diff_subln_g, ssm_conv_w, ssm_conv_b, ssm_dt_bias, ssm_a_log, ssm_d, ssm_norm_g, odd_w_in,
           odd_w_out, nsa_pe_k, nsa_pe_v, nsa_ck_w1, nsa_ck_w2, nsa_cv_w1, nsa_cv_w2, ln_mix_g, ln_mix_b,
           ln_ffn_g, ln_ffn_b, moe_w_gate, moe_w_up, moe_w_down):
    b, s, d = x.shape
    t = b * s
    xf = x.reshape(t, d)
    xb = xf.astype(BF16)
    for layer in range(DEPTH):
        i = layer // 2
        if layer % 2 == 0:
            lam_init = 0.8 - 0.6 * math.exp(-0.3 * layer)
            w_in = even_w_in[i]
            n_attn = 3 * DIFF_WIDTH
            col_scale = jnp.where(jnp.arange(n_attn) < DIFF_WIDTH, DIFF_HEAD_DIM ** -0.5 * LOG2E, 1.0)
            qkv = _matmul(xb, (w_in[:, :n_attn] * col_scale).astype(BF16), BF16, 1024, 1024).reshape(b, s, n_attn)
            n_ssm = -(-(w_in.shape[1] - n_attn) // LANES) * LANES
            pf = _matmul(xb, _pad_cols(w_in[:, n_attn:], n_ssm).astype(BF16), F32, 1024, n_ssm // 3)
            pf = pf.reshape(b, s, n_ssm)
            y_attn = _diff_attention(qkv, diff_lam_q1[i], diff_lam_k1[i], diff_lam_q2[i], diff_lam_k2[i],
                                     diff_subln_g[i], lam_init, 512, 512)
            y_ssm = _mamba2_mixer(pf, ssm_conv_w[i], ssm_conv_b[i], ssm_dt_bias[i], ssm_a_log[i], ssm_d[i],
                                  ssm_norm_g[i])
            mix_in = jnp.concatenate([y_attn, y_ssm], axis=-1).reshape(t, -1)
            w_out = even_w_out[i]
        else:
            w_in = odd_w_in[i]
            n_a = NSA_Q_WIDTH + 4 * NSA_KV_WIDTH
            wq = w_in[:, :NSA_Q_WIDTH] * (NSA_HEAD_DIM ** -0.5 * LOG2E)
            wkv =w_in[:, NSA_Q_WIDTH:NSA_Q_WIDTH + 6 * NSA_KV_WIDTH]
            wgate = w_in[:, NSA_Q_WIDTH + 6 * NSA_KV_WIDTH:]
            w_a = jnp.concatenate([wq, wkv[:, 2 * NSA_KV_WIDTH:]], axis=1)
            wgate = jnp.pad(wgate.reshape(d, NSA_KV_GROUPS, 3 * NSA_REP),
                            ((0, 0), (0, 0), (0, LANES - 3 * NSA_REP))).reshape(d, NSA_KV_GROUPS * LANES)
            w_b = jnp.concatenate([wkv[:, :2 * NSA_KV_WIDTH], wgate], axis=1)
            a_bf = _matmul(xb, w_a.astype(BF16), BF16, 1024, 1024).reshape(b, s, n_a)
            b_f32 = _matmul(xb, w_b.astype(BF16), F32, 1024, 512).reshape(b, s, -1)
            k_cmp, v_cmp = _compress(b_f32, nsa_pe_k[i], nsa_pe_v[i], nsa_ck_w1[i], nsa_cv_w1[i],
                                     nsa_ck_w2[i], nsa_cv_w2[i])
            mix_in = _nsa_attention(a_bf, b_f32, k_cmp, v_cmp, 256, 512).reshape(t, -1)
            w_out = odd_w_out[i]
        h, h_bf = _matmul_res_ln(mix_in, w_out.astype(BF16), xf, ln_mix_g[layer], ln_mix_b[layer], 512)
        y2, gates = _moe_ffn(h, h_bf, router_w, router_b, moe_w_gate, moe_w_up, moe_w_down, layer, 512)
        xf, xb = _combine_ln(h, y2, gates, ln_ffn_g[layer], ln_ffn_b[layer], 256)
    return xf.reshape(b, s, d)
```

```python
import functools
import math

import jax
import jax.numpy as jnp
import numpy as np
from jax import lax
from jax.experimental import pallas as pl
from jax.experimental.pallas import tpu as pltpu

F32 = jnp.float32
BF16 = jnp.bfloat16

D_MODEL = 2048
DEPTH = 2
DEEPNORM_ALPHA = (2.0 * DEPTH) ** 0.25
LN_EPS = 1e-5

DIFF_HEADS = 8
DIFF_HEAD_DIM = 64
DIFF_WIDTH = DIFF_HEADS * 2 * DIFF_HEAD_DIM

SSM_INNER = D_MODEL // 2
SSM_HEAD_DIM = 64
SSM_HEADS = SSM_INNER // SSM_HEAD_DIM
SSM_GROUPS = 2
SSM_STATE = 128
SSM_CONV = 4
SSM_CHUNK = 128
SSM_CONV_CH = SSM_INNER + 2 * SSM_GROUPS * SSM_STATE

NSA_HEADS = 16
NSA_KV_GROUPS = 4
NSA_REP = NSA_HEADS // NSA_KV_GROUPS
NSA_HEAD_DIM = 128
NSA_CMP_LEN = 32
NSA_CMP_STRIDE = 16
NSA_SEL_LEN = 64
NSA_SEL_COUNT = 16
NSA_WINDOW = 512
NSA_Q_WIDTH = NSA_HEADS * NSA_HEAD_DIM
NSA_KV_WIDTH = NSA_KV_GROUPS * NSA_HEAD_DIM

MOE_GROUPS = 8
MOE_PER_GROUP = 4
MOE_EXPERTS = MOE_GROUPS * MOE_PER_GROUP
MOE_TOPK = 2
MOE_FF = 1024

NEG = -1e30
SEL_PENALTY = -1e9
SUBLANES = 8
LANES = 128
ONES_ROWS = 16
LOG2E = math.log2(math.e)
VMEM_LIMIT = 56 * 1024 * 1024


def _params(sem):
    return pltpu.CompilerParams(dimension_semantics=sem, vmem_limit_bytes=VMEM_LIMIT)


def _dot(a, b):
    return jnp.dot(a, b, preferred_element_type=F32)


def _dot_nt(a, b):
    return lax.dot_general(a, b, (((1,), (1,)), ((), ())), preferred_element_type=F32)


def _mm_kernel(x_ref, w_ref, o_ref):
    o_ref[...] = _dot(x_ref[...], w_ref[...]).astype(o_ref.dtype)


def _matmul(x, w, out_dtype, tm, tn):
    m, k = x.shape
    n = w.shape[1]
    return pl.pallas_call(
        _mm_kernel,
        grid=(n // tn, m // tm),
        in_specs=[pl.BlockSpec((tm, k), lambda j, i: (i, 0)),
                  pl.BlockSpec((k, tn), lambda j, i: (0, j))],
        out_specs=pl.BlockSpec((tm, tn), lambda j, i: (i, j)),
        out_shape=jax.ShapeDtypeStruct((m, n), out_dtype),
        compiler_params=_params(("parallel", "parallel")),
        name="proj_matmul",
    )(x, w)


def _layer_norm_rows(y, g, b):
    mu = jnp.mean(y, -1, keepdims=True)
    yc = y - mu
    var = jnp.mean(yc * yc, -1, keepdims=True)
    return yc * lax.rsqrt(var + LN_EPS) * g + b


def _mm_ln_kernel(a_ref, w_ref, res_ref, g_ref, b_ref, o_ref, ob_ref):
    y = _dot(a_ref[...], w_ref[...]) + DEEPNORM_ALPHA * res_ref[...]
    h = _layer_norm_rows(y, g_ref[...], b_ref[...])
    o_ref[...] = h
    ob_ref[...] = h.astype(BF16)


def _matmul_res_ln(a, w, res, g, b, tm):
    m, k = a.shape
    n = w.shape[1]
    return pl.pallas_call(
        _mm_ln_kernel,
        grid=(m // tm,),
        in_specs=[pl.BlockSpec((tm, k), lambda i: (i, 0)),
                  pl.BlockSpec((k, n), lambda i: (0, 0)),
                  pl.BlockSpec((tm, n), lambda i: (i, 0)),
                  pl.BlockSpec((1, n), lambda i: (0, 0)),
                  pl.BlockSpec((1, n), lambda i: (0, 0))],
        out_specs=[pl.BlockSpec((tm, n), lambda i: (i, 0)),
                   pl.BlockSpec((tm, n), lambda i: (i, 0))],
        out_shape=[jax.ShapeDtypeStruct((m, n), F32), jax.ShapeDtypeStruct((m, n), BF16)],
        compiler_params=_params(("parallel",)),
        name="outproj_ln",
    )(a, w, res, g.reshape(1, n), b.reshape(1, n))


def _combine_ln_kernel(h_ref, y_ref, w_ref, g_ref, b_ref, o_ref, ob_ref):
    w = w_ref[...]
    y = DEEPNORM_ALPHA * h_ref[...] + w[:, 0:1] * y_ref[0].astype(F32) + w[:, 1:2] * y_ref[1].astype(F32)
    x = _layer_norm_rows(y, g_ref[...], b_ref[...])
    o_ref[...] = x
    ob_ref[...] = x.astype(BF16)


def _combine_ln(h, y2, w, g, b, tm):
    m, n = h.shape
    row = pl.BlockSpec((tm, n), lambda i: (i, 0))
    vec = pl.BlockSpec((1, n), lambda i: (0, 0))
    return pl.pallas_call(
        _combine_ln_kernel,
        grid=(m // tm,),
        in_specs=[row, pl.BlockSpec((MOE_TOPK, tm, n), lambda i: (0, i, 0)),
                  pl.BlockSpec((tm, MOE_TOPK), lambda i: (i, 0)), vec, vec],
        out_specs=[row, row],
        out_shape=[jax.ShapeDtypeStruct((m, n), F32), jax.ShapeDtypeStruct((m, n), BF16)],
        compiler_params=_params(("parallel",)),
        name="ffn_combine_ln",
    )(h, y2, w, g.reshape(1, n), b.reshape(1, n))


def _softmax_tile(st_ref, pt_ref, m_ref, al_ref, *, mask=None):
    rows = st_ref.shape[1]
    for c in range(rows // LANES):
        cs = slice(c * LANES, (c + 1) * LANES)
        s = st_ref[:, cs]
        if mask is not None:
            s = mask(s, c)
        m_old = m_ref[:, cs]
        m_new = jnp.maximum(m_old, jnp.max(s, axis=0, keepdims=True))
        pt_ref[:, cs] = jnp.exp2(s - m_new[0:1]).astype(BF16)
        al_ref[:, cs] = jnp.exp2(m_old - m_new)
        m_ref[:, cs] = m_new


def _pv_update(vt_aug, pt_ref, al_ref, acc_ref):
    acc_ref[...] = acc_ref[...] * al_ref[0:1, :] + _dot(vt_aug, pt_ref[...])


def _flash_loop(n, scores, softmax, update):
    scores(0, 0)

    @pl.when(n == 1)
    def _():
        softmax(0, 0, True)
        update(0, 0)

    @pl.when(n >= 2)
    def _():
        scores(1, 1)
        softmax(0, 0, False)
        n_pairs = (n - 2) // 2

        def pair(u, carry):
            j = 2 * u
            scores(j + 2, 0)
            softmax(j + 1, 1, False)
            update(j, 0)
            scores(j + 3, 1)
            softmax(j + 2, 0, False)
            update(j + 1, 1)
            return carry

        lax.fori_loop(0, n_pairs, pair, 0)
        j = 2 * n_pairs

        @pl.when(n - 1 == j + 1)
        def _():
            softmax(j + 1, 1, True)
            update(j, 0)
            update(j + 1, 1)

        @pl.when(n - 1 == j + 2)
        def _():
            scores(j + 2, 0)
            softmax(j + 1, 1, False)
            update(j, 0)
            softmax(j + 2, 0, True)
            update(j + 1, 1)
            update(j + 2, 0)


def _transpose_values(v, tk):
    b, s, h, dv = v.shape
    vt = jnp.transpose(v.reshape(b, s // tk, tk, h, dv), (0, 3, 1, 4, 2))
    return jnp.concatenate([vt, jnp.ones((b, h, s // tk, ONES_ROWS, tk), v.dtype)], axis=3)


def _diff_kernel(lq1_ref, lk1_ref, lq2_ref, lk2_ref, g_ref, q_ref, k_ref, vt_ref, o_ref,
                 q2_ref, st0_ref, st1_ref, pt0_ref, pt1_ref, m_ref, al0_ref, al1_ref, acc_ref, *, tq, tk, lam_init):
    i = pl.program_id(2)
    st, pt, al = (st0_ref, st1_ref), (pt0_ref, pt1_ref), (al0_ref, al1_ref)
    qt = q_ref[...].astype(F32).T
    half = lax.broadcasted_iota(jnp.int32, (LANES, tq), 0) < DIFF_HEAD_DIM
    q2_ref[...] = jnp.concatenate([jnp.where(half, qt, 0.0), jnp.where(half, 0.0, qt)], axis=1).astype(BF16)
    m_ref[...] = jnp.full_like(m_ref, NEG)
    acc_ref[...] = jnp.zeros_like(acc_ref)

    def scores(j, slot):
        kt = k_ref[pl.ds(pl.multiple_of(j * tk, tk), tk), :]
        st[slot][...] = _dot(kt, q2_ref[...])

    def softmax(j, slot, masked):
        def causal(s, c):
            key = j * tk + lax.broadcasted_iota(jnp.int32, (tk, LANES), 0)
            qry = i * tq + ((c * LANES + lax.broadcasted_iota(jnp.int32, (tk, LANES), 1)) & (tq - 1))
            return jnp.where(key <= qry, s, NEG)
        _softmax_tile(st[slot], pt[slot], m_ref, al[slot], mask=causal if masked else None)

    def update(j, slot):
        _pv_update(vt_ref[j], pt[slot], al[slot], acc_ref)

    _flash_loop((i * tq + tq + tk - 1) // tk, scores, softmax, update)
    acc = acc_ref[...]
    o = (acc[:LANES] / acc[LANES:LANES + 1]).T
    lam = (jnp.exp(jnp.sum(lq1_ref[...] * lk1_ref[...], -1, keepdims=True))
           - jnp.exp(jnp.sum(lq2_ref[...] * lk2_ref[...], -1, keepdims=True)) + lam_init)
    od = o[:tq] - lam * o[tq:]
    y = od * lax.rsqrt(jnp.mean(od * od, -1, keepdims=True) + 1e-5) * g_ref[...]
    o_ref[...] = (y * (1.0 - lam_init)).astype(o_ref.dtype)


def _diff_attention(qkv, lq1, lk1, lq2, lk2, subln_g, lam_init, tq, tk):
    b, s, _ = qkv.shape
    rows = 2 * tq
    dv = 2 * DIFF_HEAD_DIM
    assert tk % tq == 0 and s % tk == 0
    vt = _transpose_values(qkv[..., 2 * DIFF_WIDTH:].reshape(b, s, DIFF_HEADS, dv), tk)
    vec64 = pl.BlockSpec((1, DIFF_HEAD_DIM), lambda bb, h, i: (0, 0))
    return pl.pallas_call(
        functools.partial(_diff_kernel, tq=tq, tk=tk, lam_init=lam_init),
        grid=(b, DIFF_HEADS, s // tq),
        in_specs=[vec64, vec64, vec64, vec64,
                  pl.BlockSpec((1, LANES), lambda bb, h, i: (0, 0)),
                  pl.BlockSpec((None, tq, LANES), lambda bb, h, i: (bb, i, h)),
                  pl.BlockSpec((None, s, LANES), lambda bb, h, i: (bb, 0, DIFF_HEADS + h)),
                  pl.BlockSpec((None, None, s // tk, dv + ONES_ROWS, tk), lambda bb, h, i: (bb, h, 0, 0, 0))],
        out_specs=pl.BlockSpec((None, tq, LANES), lambda bb, h, i: (bb, i, h)),
        out_shape=jax.ShapeDtypeStruct((b, s, DIFF_WIDTH), BF16),
        scratch_shapes=[pltpu.VMEM((LANES, rows), BF16), pltpu.VMEM((tk, rows), F32), pltpu.VMEM((tk, rows), F32),
                        pltpu.VMEM((tk, rows), BF16), pltpu.VMEM((tk, rows), BF16),
                        pltpu.VMEM((8, rows), F32), pltpu.VMEM((8, rows), F32), pltpu.VMEM((8, rows), F32),
                        pltpu.VMEM((dv + ONES_ROWS, rows), F32)],
        compiler_params=_params(("parallel", "parallel", "arbitrary")),
        name="diff_attention",
    )(lq1.reshape(1, -1), lk1.reshape(1, -1), lq2.reshape(1, -1), lk2.reshape(1, -1),
      subln_g.reshape(1, -1), qkv, qkv, vt)


def _ssd_kernel(pf_ref, cw_ref, cb_ref, dtb_ref, alog_ref, drep_ref, ng_ref, tri_ref, o_ref,
                xe_ref, tail_ref, state_ref, y_ref):
    L = SSM_CHUNK
    halo = SUBLANES

    @pl.when(pl.program_id(1) == 0)
    def _():
        tail_ref[...] = jnp.zeros_like(tail_ref)
        state_ref[...] = jnp.zeros_like(state_ref)

    z = pf_ref[:, 0:SSM_INNER]
    xbc = pf_ref[:, SSM_INNER:SSM_INNER + SSM_CONV_CH]
    dt_raw = pf_ref[:, SSM_INNER + SSM_CONV_CH:SSM_INNER + SSM_CONV_CH + LANES]
    xe_ref[0:halo, :] = tail_ref[...]
    xe_ref[halo:halo + L, :] = xbc
    tail_ref[...] = xbc[L - halo:L, :]
    conv = cb_ref[...] + cw_ref[SSM_CONV - 1:SSM_CONV, :] * xbc
    for k in range(SSM_CONV - 1):
        conv = conv + cw_ref[k:k + 1, :] * xe_ref[pl.ds(halo - (SSM_CONV - 1) + k, L), :]
    xbc = conv * jax.nn.sigmoid(conv)
    xs = xbc[:, :SSM_INNER]
    dt = jax.nn.softplus(dt_raw + dtb_ref[...])
    adt = dt * (-jnp.exp(alog_ref[...]))
    acum = jnp.dot(tri_ref[...], adt, preferred_element_type=F32, precision=lax.Precision.HIGHEST)
    acum_t = acum.T
    left = lax.broadcasted_iota(jnp.int32, (1, LANES), 1) < SSM_HEAD_DIM
    tril = lax.broadcasted_iota(jnp.int32, (L, L), 0) >= lax.broadcasted_iota(jnp.int32, (L, L), 1)
    pairs_per_group = SSM_HEADS // SSM_GROUPS // 2
    for g in range(SSM_GROUPS):
        bm = xbc[:, SSM_INNER + g * SSM_STATE:SSM_INNER + (g + 1) * SSM_STATE]
        cm = xbc[:, SSM_INNER + (SSM_GROUPS + g) * SSM_STATE:SSM_INNER + (SSM_GROUPS + g + 1) * SSM_STATE]
        cb = _dot_nt(cm.astype(BF16), bm.astype(BF16))
        bm_t = bm.T
        for qq in range(pairs_per_group):
            q = g * pairs_per_group + qq
            ls = slice(q * LANES, (q + 1) * LANES)
            xs_p = xs[:, ls]
            h0, h1 = 2 * q, 2 * q + 1
            xd = xs_p * jnp.where(left, dt[:, h0:h0 + 1], dt[:, h1:h1 + 1])
            prev = state_ref[q]
            y = drep_ref[:, ls] * xs_p
            new = jnp.zeros((SSM_STATE, LANES), F32)
            for h, keep in ((h0, left), (h1, jnp.logical_not(left))):
                xd_h = jnp.where(keep, xd, 0.0).astype(BF16)
                prev_h = jnp.where(keep, prev, 0.0).astype(BF16)
                cs_col = acum[:, h:h + 1]
                cs_row = acum_t[h:h + 1, :]
                w = cb * jnp.exp(jnp.where(tril, cs_col - cs_row, -jnp.inf))
                y = y + _dot(w.astype(BF16), xd_h) + _dot((cm * jnp.exp(cs_col)).astype(BF16), prev_h)
                new = new + _dot((bm_t * jnp.exp(acum[L - 1:L, h:h + 1] - cs_row)).astype(BF16), xd_h)
            decay = jnp.where(left, jnp.exp(acum[L - 1:L, h0:h0 + 1]), jnp.exp(acum[L - 1:L, h1:h1 + 1]))
            state_ref[q] = prev * decay + new
            y_ref[:, ls] = y
    y = y_ref[...] * (z * jax.nn.sigmoid(z))
    gw = SSM_INNER // SSM_GROUPS
    for g in range(SSM_GROUPS):
        yg = y[:, g * gw:(g + 1) * gw]
        yg = yg * lax.rsqrt(jnp.mean(yg * yg, -1, keepdims=True) + 1e-5) * ng_ref[:, g * gw:(g + 1) * gw]
        o_ref[:, g * gw:(g + 1) * gw] = yg.astype(o_ref.dtype)


def _mamba2_mixer(pf, conv_w, conv_b, dt_bias, a_log, d_skip, norm_g):
    b, s, w = pf.shape
    L = SSM_CHUNK
    pad_heads = lambda v: jnp.pad(v, (0, LANES - SSM_HEADS)).reshape(1, LANES)
    tri = jnp.tril(jnp.ones((L, L), F32))
    full = lambda shape: pl.BlockSpec(shape, lambda bb, c: (0,) * len(shape))
    return pl.pallas_call(
        _ssd_kernel,
        grid=(b, s // L),
        in_specs=[pl.BlockSpec((None, L, w), lambda bb, c: (bb, c, 0)),
                  full((SSM_CONV, SSM_CONV_CH)), full((1, SSM_CONV_CH)), full((1, LANES)), full((1, LANES)),
                  full((1, SSM_INNER)), full((1, SSM_INNER)), full((L, L))],
        out_specs=pl.BlockSpec((None, L, SSM_INNER), lambda bb, c: (bb, c, 0)),
        out_shape=jax.ShapeDtypeStruct((b, s, SSM_INNER), BF16),
        scratch_shapes=[pltpu.VMEM((L + 8, SSM_CONV_CH), F32), pltpu.VMEM((8, SSM_CONV_CH), F32),
                        pltpu.VMEM((SSM_HEADS // 2, SSM_STATE, LANES), F32), pltpu.VMEM((L, SSM_INNER), F32)],
        compiler_params=_params(("parallel", "arbitrary")),
        name="ssd_mixer",
    )(pf, conv_w, conv_b.reshape(1, -1), pad_heads(dt_bias), pad_heads(a_log),
      jnp.repeat(d_skip, SSM_HEAD_DIM).reshape(1, -1), norm_g.reshape(1, -1), tri)


def _cmp_kernel(x_ref, pek_ref, pev_ref, w1k_ref, w1v_ref, w2k_ref, w2v_ref, ko_ref, vo_ref,
                top_ref, bot_ref, *, nb):
    l = pl.program_id(1)
    g4 = NSA_KV_GROUPS

    @pl.when(l == 0)
    def _():
        top_ref[...] = jnp.zeros_like(top_ref)
        bot_ref[...] = jnp.zeros_like(bot_ref)

    x = x_ref[...]
    for kv, (pe_ref, w1_ref) in enumerate(((pek_ref, w1k_ref), (pev_ref, w1v_ref))):
        x4 = jnp.concatenate([x[:, kv * NSA_KV_WIDTH + g * LANES: kv * NSA_KV_WIDTH + (g + 1) * LANES]
                              for g in range(g4)], axis=0)
        top_ref[kv] += _dot((x4 + pe_ref[pl.ds(l, 1), :]).astype(BF16), w1_ref[l])
        bot_ref[kv] += _dot((x4 + pe_ref[pl.ds(l + NSA_CMP_STRIDE, 1), :]).astype(BF16),
                            w1_ref[l + NSA_CMP_STRIDE])

    @pl.when(l == NSA_CMP_STRIDE - 1)
    def _():
        last = lax.broadcasted_iota(jnp.int32, (nb, LANES), 0) == nb - 1
        for kv, (w2_ref, o_ref) in enumerate(((w2k_ref, ko_ref), (w2v_ref, vo_ref))):
            for g in range(g4):
                top = top_ref[kv, g * nb:(g + 1) * nb, :]
                bot = bot_ref[kv, g * nb:(g + 1) * nb, :]
                hid = top + pltpu.roll(bot, nb - 1, 0)
                out = _dot(jax.nn.gelu(hid, approximate=True).astype(BF16), w2_ref[...])
                o_ref[g] = jnp.where(last, 0.0, out).astype(o_ref.dtype)


def _compress(bf, pe_k, pe_v, w1k, w1v, w2k, w2v):
    b, s, w = bf.shape
    nb = s // NSA_CMP_STRIDE
    xv = bf.reshape(b, nb, NSA_CMP_STRIDE * w)
    full = lambda shape: pl.BlockSpec(shape, lambda bb, l: (0,) * len(shape))
    out_spec = pl.BlockSpec((None, NSA_KV_GROUPS, nb, LANES), lambda bb, l: (bb, 0, 0, 0))
    out_sd = jax.ShapeDtypeStruct((b, NSA_KV_GROUPS, nb, LANES), BF16)
    w1 = lambda a: a.astype(BF16).reshape(NSA_CMP_LEN, NSA_HEAD_DIM, -1)
    return pl.pallas_call(
        functools.partial(_cmp_kernel, nb=nb),
        grid=(b, NSA_CMP_STRIDE),
        in_specs=[pl.BlockSpec((None, nb, w), lambda bb, l: (bb, 0, l)),
                  full((NSA_CMP_LEN, LANES)), full((NSA_CMP_LEN, LANES)),
                  full((NSA_CMP_LEN, NSA_HEAD_DIM, LANES)), full((NSA_CMP_LEN, NSA_HEAD_DIM, LANES)),
                  full((LANES, LANES)), full((LANES, LANES))],
        out_specs=[out_spec, out_spec],
        out_shape=[out_sd, out_sd],
        scratch_shapes=[pltpu.VMEM((2, NSA_KV_GROUPS * nb, LANES), F32),
                        pltpu.VMEM((2, NSA_KV_GROUPS * nb, LANES), F32)],
        compiler_params=_params(("parallel", "arbitrary")),
        name="nsa_compress",
    )(xv, pe_k, pe_v, w1(w1k), w1(w1v), w2k.astype(BF16), w2v.astype(BF16))


def _nsa_kernel(q_ref, gate_ref, kc_ref, vct_ref, ks_ref, vst_ref, kw_ref, vwt_ref, blk_ref, ovt_ref, o_ref,
                qa_ref, sc_ref, pc_ref, ps_ref, oc_ref, imp_ref, st0_ref, st1_ref, pt0_ref, pt1_ref, sw_ref, pw_ref,
                m_ref, al0_ref, al1_ref, acc_ref, mw_ref, alw_ref, accw_ref, *, tq, tk, seq, slab):
    i = pl.program_id(2)
    t0 = i * tq
    rep = NSA_REP
    nq = tq // LANES
    ncp = seq // NSA_CMP_STRIDE
    lane = lax.broadcasted_iota(jnp.int32, (1, LANES), 1)
    t_of = lambda c: t0 + (c % nq) * LANES + lane
    q = q_ref[...].astype(F32)
    for r in range(rep):
        qa_ref[0:LANES, r * tq:(r + 1) * tq] = q[:, r * LANES:(r + 1) * LANES].T.astype(BF16)
    qs = qa_ref[0:LANES, :]

    def compressed(nk):
        sc_ref[0:nk, :] = _dot(kc_ref[0:nk, :], qs)
        n_end = lax.broadcasted_iota(jnp.int32, (nk, LANES), 0) * NSA_CMP_STRIDE + (NSA_CMP_LEN - 1)
        for c in range(rep * nq):
            cs = slice(c * LANES, (c + 1) * LANES)
            qsl = slice((c % nq) * LANES, (c % nq + 1) * LANES)
            s = jnp.where(n_end <= t_of(c), sc_ref[0:nk, cs], NEG)
            e = jnp.exp2(s - jnp.max(s, axis=0, keepdims=True))
            any_visible = t_of(c) >= NSA_CMP_LEN - 1
            p = e * jnp.where(any_visible, 1.0 / jnp.sum(e, axis=0, keepdims=True), 0.0)
            pc_ref[0:nk, cs] = p.astype(BF16)
            ps_ref[0:nk, qsl] = p if c < nq else ps_ref[0:nk, qsl] + p
        oc_ref[...] = _dot(vct_ref[:, 0:nk], pc_ref[0:nk, :])
        ps = ps_ref[0:nk, :]
        p_hi = ps.astype(BF16)
        p_lo = (ps - p_hi.astype(F32)).astype(BF16)
        imp_ref[...] = _dot(ovt_ref[:, 0:nk], p_hi) + _dot(ovt_ref[:, 0:nk], p_lo)

    n_visible = t0 // NSA_CMP_STRIDE + (tq - NSA_CMP_LEN) // NSA_CMP_STRIDE + 1
    for v in range(ncp // LANES):
        @pl.when((n_visible - 1) // LANES == v)
        def _():
            compressed((v + 1) * LANES)
    o_cmp_t = oc_ref[...]
    mi = lax.broadcasted_iota(jnp.int32, (LANES, LANES), 0)
    mf = mi.astype(F32)
    for qc in range(nq):
        cur = t_of(qc) // NSA_SEL_LEN
        forced = (mi == 0) | (mi == cur) | (mi == cur - 1)
        vals = jnp.where(forced, -jnp.inf, jnp.where(mi <= cur, imp_ref[:, qc * LANES:(qc + 1) * LANES], NEG))
        pen = jnp.where(forced, 0.0, SEL_PENALTY)
        for _ in range(NSA_SEL_COUNT - 3):
            mx = jnp.max(vals, axis=0, keepdims=True)
            first = jnp.min(jnp.where(vals == mx, mf, float(LANES)), axis=0, keepdims=True)
            hit = mf == first
            pen = jnp.where(hit, 0.0, pen)
            vals = jnp.where(hit, -jnp.inf, vals)
        pen_b = pen.astype(BF16)
        for r in range(rep):
            qa_ref[LANES:2 * LANES, r * tq + qc * LANES:r * tq + (qc + 1) * LANES] = pen_b

    ws = pl.multiple_of(jnp.clip(t0 - NSA_WINDOW, 0, seq - slab), LANES)
    sw_ref[...] = _dot(kw_ref[pl.ds(ws, slab), :], qs)
    mw_ref[...] = jnp.full_like(mw_ref, NEG)
    accw_ref[...] = jnp.zeros_like(accw_ref)
    jb = ws // LANES
    v_slab = jnp.concatenate([vwt_ref[jb + u] for u in range(slab // LANES)], axis=1)

    def window(s, c):
        kpos = ws + lax.broadcasted_iota(jnp.int32, (slab, LANES), 0)
        return jnp.where(kpos <= t_of(c), jnp.where(kpos > t_of(c) - NSA_WINDOW, s, NEG), NEG)

    _softmax_tile(sw_ref, pw_ref, mw_ref, alw_ref, mask=window)
    _pv_update(v_slab, pw_ref, alw_ref, accw_ref)
    accw = accw_ref[...]
    o_win_t = accw[:LANES] / accw[LANES:LANES + 1]

    m_ref[...] = jnp.full_like(m_ref, NEG)
    acc_ref[...] = jnp.zeros_like(acc_ref)
    st, pt, al = (st0_ref, st1_ref), (pt0_ref, pt1_ref), (al0_ref, al1_ref)

    def scores(j, slot):
        k0 = pl.multiple_of(j * tk, tk)
        ka = jnp.concatenate([ks_ref[pl.ds(k0, tk), :], blk_ref[pl.ds(k0, tk), :]], axis=1)
        st[slot][...] = _dot(ka, qa_ref[...])

    def softmax(j, slot, masked):
        def causal(s, c):
            kpos = j * tk + lax.broadcasted_iota(jnp.int32, (tk, LANES), 0)
            return jnp.where(kpos <= t_of(c), s, NEG)
        _softmax_tile(st[slot], pt[slot], m_ref, al[slot], mask=causal if masked else None)

    def update(j, slot):
        _pv_update(vst_ref[j], pt[slot], al[slot], acc_ref)

    _flash_loop((t0 + tq + tk - 1) // tk, scores, softmax, update)
    acc = acc_ref[...]
    o_sel_t = acc[:LANES] / acc[LANES:LANES + 1]

    g_t = jax.nn.sigmoid(gate_ref[...]).T
    for r in range(rep):
        cs = slice(r * tq, (r + 1) * tq)
        out_t = (g_t[3 * r:3 * r + 1] * o_cmp_t[:, cs] + g_t[3 * r + 1:3 * r + 2] * o_sel_t[:, cs]
                 + g_t[3 * r + 2:3 * r + 3] * o_win_t[:, cs])
        o_ref[:, r * LANES:(r + 1) * LANES] = out_t.T.astype(o_ref.dtype)


def _nsa_attention(a_bf, b_f32, k_cmp, v_cmp, tq, tk):
    b, s, _ = a_bf.shape
    g4 = NSA_KV_GROUPS
    rows = NSA_REP * tq
    ncp = s // NSA_CMP_STRIDE
    slab = NSA_WINDOW + tq
    dva = NSA_HEAD_DIM + ONES_ROWS
    assert s >= slab and s % tk == 0 and tk % tq == 0 and tq % LANES == 0 and (s - slab) % LANES == 0
    assert s // NSA_SEL_LEN <= LANES and ncp % LANES == 0
    blk = (jnp.arange(s)[:, None] // NSA_SEL_LEN == jnp.arange(LANES)[None, :]).astype(BF16)
    cs = np.arange(ncp)[:, None] * NSA_CMP_STRIDE
    ss = np.arange(LANES)[None, :] * NSA_SEL_LEN
    ov = np.maximum(np.minimum(cs + NSA_CMP_LEN, ss + NSA_SEL_LEN) - np.maximum(cs, ss), 0) / NSA_CMP_LEN
    ov[ncp - 1] = 0.0
    qoff = NSA_Q_WIDTH // LANES
    group_values = lambda n: a_bf[..., NSA_Q_WIDTH + n * NSA_KV_WIDTH:NSA_Q_WIDTH + (n + 1) * NSA_KV_WIDTH].reshape(
        b, s, g4, NSA_HEAD_DIM)
    vst = _transpose_values(group_values(1), tk)
    vwt = _transpose_values(group_values(3), LANES)
    vct = jnp.swapaxes(v_cmp, 2, 3)
    k_spec = lambda n: pl.BlockSpec((None, s, LANES), lambda bb, g, i: (bb, 0, qoff + n * g4 + g))
    vt_spec = lambda t: pl.BlockSpec((None, None, s // t, dva, t), lambda bb, g, i: (bb, g, 0, 0, 0))
    return pl.pallas_call(
        functools.partial(_nsa_kernel, tq=tq, tk=tk, seq=s, slab=slab),
        grid=(b, g4, s // tq),
        in_specs=[pl.BlockSpec((None, tq, NSA_REP * LANES), lambda bb, g, i: (bb, i, g)),
                  pl.BlockSpec((None, tq, LANES), lambda bb, g, i: (bb, i, 2 * g4 + g)),
                  pl.BlockSpec((None, None, ncp, LANES), lambda bb, g, i: (bb, g, 0, 0)),
                  pl.BlockSpec((None, None, LANES, ncp), lambda bb, g, i: (bb, g, 0, 0)),
                  k_spec(0), vt_spec(tk), k_spec(2), vt_spec(LANES),
                  pl.BlockSpec((s, LANES), lambda bb, g, i: (0, 0)),
                  pl.BlockSpec((LANES, ncp), lambda bb, g, i: (0, 0))],
        out_specs=pl.BlockSpec((None, tq, NSA_REP * LANES), lambda bb, g, i: (bb, i, g)),
        out_shape=jax.ShapeDtypeStruct((b, s, NSA_Q_WIDTH), BF16),
        scratch_shapes=[pltpu.VMEM((2 * LANES, rows), BF16),
                        pltpu.VMEM((ncp, rows), F32), pltpu.VMEM((ncp, rows), BF16), pltpu.VMEM((ncp, tq), F32),
                        pltpu.VMEM((NSA_HEAD_DIM, rows), F32), pltpu.VMEM((LANES, tq), F32),
                        pltpu.VMEM((tk, rows), F32), pltpu.VMEM((tk, rows), F32),
                        pltpu.VMEM((tk, rows), BF16), pltpu.VMEM((tk, rows), BF16),
                        pltpu.VMEM((slab, rows), F32), pltpu.VMEM((slab, rows), BF16),
                        pltpu.VMEM((8, rows), F32), pltpu.VMEM((8, rows), F32), pltpu.VMEM((8, rows), F32),
                        pltpu.VMEM((dva, rows), F32),
                        pltpu.VMEM((8, rows), F32), pltpu.VMEM((8, rows), F32), pltpu.VMEM((dva, rows), F32)],
        compiler_params=_params(("parallel", "parallel", "arbitrary")),
        name="nsa_attention",
    )(a_bf, b_f32, k_cmp, vct, a_bf, vst, a_bf, vwt, blk, jnp.asarray(ov.T, BF16))


def _moe_kernel(be_ref, hs_ref, first_ref, nu_ref, x_ref, wg_ref, wu_ref, wd_ref, o_ref,
                wgb_ref, wub_ref, wdb_ref, part_ref):
    i, j = pl.program_id(0), pl.program_id(1)

    @pl.when(i < nu_ref[0])
    def _():
        @pl.when(first_ref[i] == 1)
        def _():
            wgb_ref[j] = wg_ref[...].astype(BF16)
            wub_ref[j] = wu_ref[...].astype(BF16)
            wdb_ref[j] = wd_ref[...].astype(BF16)

        x = x_ref[...]
        hid = jax.nn.silu(_dot(x, wgb_ref[j])) * _dot(x, wub_ref[j])
        part = _dot(hid.astype(BF16), wdb_ref[j])

        @pl.when(j == 0)
        def _():
            part_ref[...] = part

        @pl.when(j == 1)
        def _():
            o_ref[...] = (part_ref[...] + part).astype(o_ref.dtype)

    @pl.when(i >= nu_ref[0])
    def _():
        o_ref[...] = jnp.zeros_like(o_ref)


def _moe_blocks(block_e, n_used, xg, wg, wu, wd, layer, tm):
    rows, d = xg.shape
    ff = wg.shape[-1]
    ffh = ff // 2
    n_blocks = rows // tm
    blk = jnp.arange(n_blocks, dtype=jnp.int32)
    prev_e = jnp.concatenate([jnp.full((1,), -1, jnp.int32), block_e[:-1]])
    first = ((block_e != prev_e) & (blk < n_used[0])).astype(jnp.int32)
    half = jnp.where(first[:, None] == 1, jnp.arange(2, dtype=jnp.int32)[None, :], 1).reshape(-1)
    grid_spec = pltpu.PrefetchScalarGridSpec(
        num_scalar_prefetch=4,
        grid=(n_blocks, 2),
        in_specs=[pl.BlockSpec((tm, d), lambda i, j, be, hs, fi, nu: (i, 0)),
                  pl.BlockSpec((None, None, d, ffh), lambda i, j, be, hs, fi, nu: (layer, be[i], 0, hs[2 * i + j])),
                  pl.BlockSpec((None, None, d, ffh), lambda i, j, be, hs, fi, nu: (layer, be[i], 0, hs[2 * i + j])),
                  pl.BlockSpec((None, None, ffh, d), lambda i, j, be, hs, fi, nu: (layer, be[i], hs[2 * i + j], 0))],
        out_specs=pl.BlockSpec((tm, d), lambda i, j, be, hs, fi, nu: (i, 0)),
        scratch_shapes=[pltpu.VMEM((2, d, ffh), BF16), pltpu.VMEM((2, d, ffh), BF16), pltpu.VMEM((2, ffh, d), BF16),
                        pltpu.VMEM((tm, d), F32)],
    )
    return pl.pallas_call(
        _moe_kernel,
        grid_spec=grid_spec,
        out_shape=jax.ShapeDtypeStruct((rows, d), BF16),
        compiler_params=_params(("arbitrary", "arbitrary")),
        name="moe_experts",
    )(block_e, half, first, n_used, xg, wg, wu, wd)


def _router_kernel(h_ref, wt_ref, b_ref, tri_ref, e_ref, g_ref, r_ref, cnt_ref, sig_ref, bia_ref, carry_ref):
    @pl.when(pl.program_id(0) == 0)
    def _():
        carry_ref[...] = jnp.zeros_like(carry_ref)

    tm = h_ref.shape[0]
    logits = lax.dot_general(wt_ref[...], h_ref[...], (((1,), (1,)), ((), ())),
                             preferred_element_type=F32, precision=lax.Precision.HIGHEST)
    sig = jax.nn.sigmoid(logits)
    sig_ref[...] = sig
    biased = sig + b_ref[...]
    n_chunks = tm // LANES
    for c in range(n_chunks):
        bia_ref[c] = biased[:, c * LANES:(c + 1) * LANES]
    cand = [jnp.concatenate([bia_ref[c, pl.ds(k, MOE_GROUPS, stride=MOE_PER_GROUP), :] for c in range(n_chunks)],
                            axis=1) for k in range(MOE_PER_GROUP)]

    def top_of(vals):
        best = functools.reduce(jnp.maximum, vals)
        idx = jnp.full(best.shape, MOE_PER_GROUP - 1, jnp.int32)
        for k in range(MOE_PER_GROUP - 2, -1, -1):
            idx = jnp.where(vals[k] == best, k, idx)
        return best, idx

    top1, idx1 = top_of(cand)
    top2, idx2 = top_of([jnp.where(idx1 == k, -jnp.inf, cand[k]) for k in range(MOE_PER_GROUP)])
    gsum = top1 + top2
    gi = lax.broadcasted_iota(jnp.int32, gsum.shape, 0)
    grp = jnp.min(jnp.where(gsum == jnp.max(gsum, axis=0, keepdims=True), gi, MOE_GROUPS), axis=0, keepdims=True)
    chosen = gi == grp
    e1 = grp * MOE_PER_GROUP + jnp.sum(jnp.where(chosen, idx1, 0), axis=0, keepdims=True)
    e2 = grp * MOE_PER_GROUP + jnp.sum(jnp.where(chosen, idx2, 0), axis=0, keepdims=True)
    ei = lax.broadcasted_iota(jnp.int32, (MOE_EXPERTS, tm), 0)
    oh1, oh2 = ei == e1, ei == e2
    sig = sig_ref[...]
    w1 = jnp.sum(jnp.where(oh1, sig, 0.0), axis=0, keepdims=True)
    w2 = jnp.sum(jnp.where(oh2, sig, 0.0), axis=0, keepdims=True)
    e_ref[...] = jnp.concatenate([e1, e2], axis=0)
    g_ref[...] = jnp.concatenate([w1, w2], axis=0) / (w1 + w2)
    both = jnp.where(oh1, 1.0, jnp.where(oh2, 1.0, 0.0))
    base = carry_ref[:, 0:1] + _dot(both.astype(BF16), tri_ref[...])
    r1 = jnp.sum(jnp.where(oh1, base, 0.0), axis=0, keepdims=True)
    r2 = jnp.sum(jnp.where(oh2, base, 0.0), axis=0, keepdims=True)
    r_ref[...] = jnp.concatenate([r1, r2], axis=0).astype(jnp.int32)
    carry_ref[...] += jnp.sum(both, axis=1, keepdims=True)
    cnt_ref[...] = carry_ref[...]


def _route(h, router_w, router_b, tm):
    t, d = h.shape
    e = router_w.shape[1]
    tri = (jnp.arange(tm)[:, None] < jnp.arange(tm)[None, :]).astype(BF16)
    tok = lambda dt: (pl.BlockSpec((MOE_TOPK, tm), lambda i: (0, i)), jax.ShapeDtypeStruct((MOE_TOPK, t), dt))
    (es, esd), (gs, gsd), (rs, rsd) = tok(jnp.int32), tok(F32), tok(jnp.int32)
    return pl.pallas_call(
        _router_kernel,
        grid=(t // tm,),
        in_specs=[pl.BlockSpec((tm, d), lambda i: (i, 0)), pl.BlockSpec((e, d), lambda i: (0, 0)),
                  pl.BlockSpec((e, 1), lambda i: (0, 0)), pl.BlockSpec((tm, tm), lambda i: (0, 0))],
        out_specs=[es, gs, rs, pl.BlockSpec((e, LANES), lambda i: (0, 0))],
        out_shape=[esd, gsd, rsd, jax.ShapeDtypeStruct((e, LANES), F32)],
        scratch_shapes=[pltpu.VMEM((e, tm), F32), pltpu.VMEM((tm // LANES, e, LANES), F32),
                        pltpu.VMEM((e, LANES), F32)],
        compiler_params=_params(("arbitrary",)),
        name="router",
    )(h, router_w.T, router_b.reshape(e, 1), tri)


def _moe_ffn(h, h_bf, router_w, router_b, wg, wu, wd, layer, tm):
    t, d = h.shape
    expert, gate, rank, counts = _route(h, router_w, router_b, min(512, t))
    n_blocks = t * MOE_TOPK // tm + MOE_EXPERTS
    rows = n_blocks * tm
    counts = counts[:, 0].astype(jnp.int32)
    padded = (counts + tm - 1) // tm * tm
    pad_end = jnp.cumsum(padded)
    onehot = expert[..., None] == jnp.arange(MOE_EXPERTS, dtype=jnp.int32)
    pos = rank + jnp.sum(jnp.where(onehot, pad_end - padded, 0), axis=-1)
    tok = jnp.broadcast_to(jnp.arange(t, dtype=jnp.int32), (MOE_TOPK, t))
    buf_tok = (jnp.arange(rows, dtype=jnp.int32) % t).at[pos.reshape(-1)].set(tok.reshape(-1), unique_indices=True)
    block_start = jnp.arange(n_blocks, dtype=jnp.int32) * tm
    block_e = jnp.minimum(jnp.sum((pad_end[None, :] <= block_start[:, None]).astype(jnp.int32), axis=1),
                          MOE_EXPERTS - 1)
    n_used = (pad_end[-1:] // tm).astype(jnp.int32)
    xg = h_bf[buf_tok]
    yb = _moe_blocks(block_e, n_used, xg, wg, wu, wd, layer, tm)
    return yb[pos], gate.T


def _pad_cols(w, n):
    return jnp.pad(w, ((0, 0), (0, n - w.shape[1])))


def kernel(x, router_w, router_b, even_w_in, even_w_out, diff_lam_q1, diff_lam_k1, diff_lam_q2, diff_lam_k2,
           diff_subln_g, ssm_conv_w, ssm_conv_b, ssm_dt_bias, ssm_a_log, ssm_d, ssm_norm_g, odd_w_in,
           odd_w_out, nsa_pe_k, nsa_pe_v, nsa_ck_w1, nsa_ck_w2, nsa_cv_w1, nsa_cv_w2, ln_mix_g, ln_mix_b,
           ln_ffn_g, ln_ffn_b, moe_w_gate, moe_w_up, moe_w_down):
    b, s, d = x.shape
    t = b * s
    xf = x.reshape(t, d)
    xb = xf.astype(BF16)
    for layer in range(DEPTH):
        i = layer // 2
        if layer % 2 == 0:
            lam_init = 0.8 - 0.6 * math.exp(-0.3 * layer)
            w_in = even_w_in[i]
            n_attn = 3 * DIFF_WIDTH
            col_scale = jnp.where(jnp.arange(n_attn) < DIFF_WIDTH, DIFF_HEAD_DIM ** -0.5 * LOG2E, 1.0)
            qkv = _matmul(xb, (w_in[:, :n_attn] * col_scale).astype(BF16), BF16, 1024, 1024).reshape(b, s, n_attn)
            n_ssm = -(-(w_in.shape[1] - n_attn) // LANES) * LANES
            pf = _matmul(xb, _pad_cols(w_in[:, n_attn:], n_ssm).astype(BF16), F32, 1024, n_ssm // 3)
            pf = pf.reshape(b, s, n_ssm)
            y_attn = _diff_attention(qkv, diff_lam_q1[i], diff_lam_k1[i], diff_lam_q2[i], diff_lam_k2[i],
                                     diff_subln_g[i], lam_init, 512, 512)
            y_ssm = _mamba2_mixer(pf, ssm_conv_w[i], ssm_conv_b[i], ssm_dt_bias[i], ssm_a_log[i], ssm_d[i],
                                  ssm_norm_g[i])
            mix_in = jnp.concatenate([y_attn, y_ssm], axis=-1).reshape(t, -1)
            w_out = even_w_out[i]
        else:
            w_in = odd_w_in[i]
            n_a = NSA_Q_WIDTH + 4 * NSA_KV_WIDTH
            wq = w_in[:, :NSA_Q_WIDTH] * (NSA_HEAD_DIM ** -0.5 * LOG2E)
            wkv =w_in[:, NSA_Q_WIDTH:NSA_Q_WIDTH + 6 * NSA_KV_WIDTH]
            wgate = w_in[:, NSA_Q_WIDTH + 6 * NSA_KV_WIDTH:]
            w_a = jnp.concatenate([wq, wkv[:, 2 * NSA_KV_WIDTH:]], axis=1)
            wgate = jnp.pad(wgate.reshape(d, NSA_KV_GROUPS, 3 * NSA_REP),
                            ((0, 0), (0, 0), (0, LANES - 3 * NSA_REP))).reshape(d, NSA_KV_GROUPS * LANES)
            w_b = jnp.concatenate([wkv[:, :2 * NSA_KV_WIDTH], wgate], axis=1)
            a_bf = _matmul(xb, w_a.astype(BF16), BF16, 1024, 1024).reshape(b, s, n_a)
            b_f32 = _matmul(xb, w_b.astype(BF16), F32, 1024, 512).reshape(b, s, -1)
            k_cmp, v_cmp = _compress(b_f32, nsa_pe_k[i], nsa_pe_v[i], nsa_ck_w1[i], nsa_cv_w1[i],
                                     nsa_ck_w2[i], nsa_cv_w2[i])
            mix_in = _nsa_attention(a_bf, b_f32, k_cmp, v_cmp, 256, 512).reshape(t, -1)
            w_out = odd_w_out[i]
        h, h_bf = _matmul_res_ln(mix_in, w_out.astype(BF16), xf, ln_mix_g[layer], ln_mix_b[layer], 512)
        y2, gates = _moe_ffn(h, h_bf, router_w, router_b, moe_w_gate, moe_w_up, moe_w_down, layer, 512)
        xf, xb = _combine_ln(h, y2, gates, ln_ffn_g[layer], ln_ffn_b[layer], 256)
    return xf.reshape(b, s, d)
```
